```python
import math
import jax
import jax.numpy as jnp
from jax import lax
import numpy as np

D_MODEL = 1024
BATCH = 2
SEQ = 8192
DEPTH = 2

GRID_W = 64
CTX_LEN = 256
MIX_WIDTH = D_MODEL
EPS = 1e-6

HY_WIDTH = MIX_WIDTH // 4
HY_ORDER = 2
HY_POS_BANDS = 16
HY_EMB = 2 * HY_POS_BANDS + 1
HY_FILT_HIDDEN = 64
HY_DECAY_TARGET = 1e-2
HY_FAST_DECAY_PCT = 0.3
HY_SLOW_DECAY_PCT = 1.5

SG_HEADS = 4
SG_WIDTH = MIX_WIDTH // 4
SG_HEAD_DIM = SG_WIDTH // SG_HEADS
CHUNK = 128

DA_HEADS = 4
DA_WIDTH = MIX_WIDTH // 2
DA_V_DIM = DA_WIDTH // DA_HEADS
DA_HEAD_DIM = DA_V_DIM // 2
Q_BLOCK = 128
ROPE_BASE = 10000.0

HY_IN = (HY_ORDER + 1) * HY_WIDTH
SG_IN = 2 * SG_WIDTH
DA_IN = 3 * DA_WIDTH
IN_WIDTH = HY_IN + SG_IN + DA_IN
FFN_HIDDEN = (-(-8 * D_MODEL // 3) + 255) // 256 * 256

kernel_name = 'hybrid_hyena_gmlp_diffattn_block'


def rmsnorm(x, g):
    xf = x.astype(jnp.float32)
    y = xf * lax.rsqrt(jnp.mean(xf * xf, axis=-1, keepdims=True) + EPS)
    return (y * g.astype(jnp.float32)).astype(x.dtype)


def grid_positions(rows):
    row = jnp.repeat(jnp.arange(rows, dtype=jnp.int32), GRID_W)
    col = jnp.tile(jnp.arange(GRID_W, dtype=jnp.int32), rows)
    return row, col


def _rope_axis(x, pos):
    half = x.shape[-1] // 2
    inv_freq = ROPE_BASE ** (-jnp.arange(half, dtype=jnp.float32) / half)
    ang = pos.astype(jnp.float32)[:, None] * inv_freq[None, :]
    cos = jnp.cos(ang)[:, None, None, :]
    sin = jnp.sin(ang)[:, None, None, :]
    x1, x2 = x[..., :half], x[..., half:]
    return jnp.concatenate([x1 * cos - x2 * sin, x2 * cos + x1 * sin], axis=-1)


def rope_2d(x, row, col):
    r = x.shape[-1] // 2
    xf = x.astype(jnp.float32)
    out = jnp.concatenate([_rope_axis(xf[..., :r], row), _rope_axis(xf[..., r:], col)], axis=-1)
    return out.astype(x.dtype)


def short_conv3(p, w, b):
    pp = jnp.pad(p, ((0, 0), (1, 1), (0, 0)))
    return pp[:, :-2] * w[0] + pp[:, 1:-1] * w[1] + pp[:, 2:] * w[2] + b


def hyena_filters(L, w1, b1, w2, b2, w3, freq):
    t = jnp.linspace(0.0, 1.0, L, dtype=jnp.float32)[:, None]
    bands = jnp.linspace(1e-4, HY_POS_BANDS - 1, HY_POS_BANDS, dtype=jnp.float32)
    ang = (2.0 * math.pi / L) * jnp.arange(L, dtype=jnp.float32)[:, None] * bands[None, :]
    feats = jnp.concatenate([t, jnp.cos(ang), -jnp.sin(ang)], axis=-1)
    h = jnp.sin(freq[0] * (feats @ w1 + b1))
    h = jnp.sin(freq[1] * (h @ w2 + b2))
    h = (h @ w3).astype(jnp.float32).reshape(L, HY_ORDER, 2, HY_WIDTH)
    min_decay = math.log(HY_DECAY_TARGET) / HY_SLOW_DECAY_PCT
    max_decay = math.log(HY_DECAY_TARGET) / HY_FAST_DECAY_PCT
    deltas = jnp.abs(jnp.linspace(min_decay, max_decay, HY_WIDTH, dtype=jnp.float32))
    h = h * jnp.exp(-t[:, :, None, None] * deltas)
    zero = jnp.zeros((1, HY_ORDER, HY_WIDTH), jnp.float32)
    return jnp.concatenate([h[:, :, 0], zero, h[:0:-1, :, 1]], axis=0)


def fftconv(u, k, bias):
    L = u.shape[1]
    uf = u.astype(jnp.float32)
    spec = jnp.fft.rfft(uf, n=2 * L, axis=1) * jnp.fft.rfft(k, n=2 * L, axis=0)[None]
    y = jnp.fft.irfft(spec, n=2 * L, axis=1)[:, :L]
    return (y + uf * bias.astype(jnp.float32)).astype(u.dtype)


def hyena(p, conv_w, conv_b, w1, b1, w2, b2, w3, freq, bias):
    L = p.shape[1]
    x1, x2, v = jnp.split(short_conv3(p, conv_w, conv_b), 3, axis=-1)
    k = hyena_filters(L, w1, b1, w2, b2, w3, freq)
    z = x1 * fftconv(v, k[:, 0], bias[0])
    return x2 * fftconv(z, k[:, 1], bias[1])


def sgu(p, norm_g, w_s, b_s):
    B, L, _ = p.shape
    u, v = jnp.split(jax.nn.gelu(p, approximate=False), 2, axis=-1)
    v = rmsnorm(v, norm_g).reshape(B, L // CHUNK, CHUNK, SG_HEADS, SG_HEAD_DIM)
    mixed = jnp.einsum('hpq,bnqhd->bnphd', w_s, v) + b_s.T[None, None, :, :, None]
    return u * mixed.reshape(B, L, SG_WIDTH)


def qk_heads(p, g):
    B, L, _ = p.shape
    return rmsnorm(p.reshape(B, L, DA_HEADS, 2, DA_HEAD_DIM), g)


def v_heads(p):
    B, L, _ = p.shape
    return p.reshape(B, L, DA_HEADS, DA_V_DIM)


def diff_attn_block(q, k, v, lam):
    s = jnp.einsum('bqhcd,bkhcd->bhcqk', q, k).astype(jnp.float32) * (DA_HEAD_DIM ** -0.5)
    a = jax.nn.softmax(s, axis=-1)
    w = a[:, :, 0] - lam * a[:, :, 1]
    return jnp.einsum('bhqk,bkhd->bqhd', w.astype(v.dtype), v)


def diff_attention(q, k, v, lam, subln_g, lam_init):
    B, Lq = q.shape[:2]
    qb = q.reshape(B, Lq // Q_BLOCK, Q_BLOCK, DA_HEADS, 2, DA_HEAD_DIM).swapaxes(0, 1)
    o = lax.map(lambda qq: diff_attn_block(qq, k, v, lam), qb)
    o = o.swapaxes(0, 1).reshape(B, Lq, DA_HEADS, DA_V_DIM)
    o = rmsnorm(o, subln_g) * (1.0 - lam_init)
    return o.reshape(B, Lq, DA_WIDTH)


def mix_out(p, att, hy_params, sg_params, w_out):
    hy = hyena(p[..., :HY_IN], *hy_params)
    sg = sgu(p[..., HY_IN:HY_IN + SG_IN], *sg_params)
    return jnp.concatenate([hy, sg, att], axis=-1) @ w_out


def swiglu(h, w1, w2):
    g, u = jnp.split(h @ w1, 2, axis=-1)
    return (jax.nn.silu(g) * u) @ w2


def adaln(cvec, w, b):
    return jnp.split(jax.nn.silu(cvec) @ w + b, 6, axis=-1)


def setup_inputs(seed: int = 0) -> dict:
    key = jax.random.key(seed)
    ks = iter(jax.random.split(key, 32))

    def nrm(shape, scale):
        return jax.random.normal(next(ks), shape, jnp.float32) * scale

    return {
        'x': nrm((BATCH, SEQ, D_MODEL), 1.0),
        'c': nrm((BATCH, D_MODEL), 1.0),
        'ctx': nrm((BATCH, CTX_LEN, D_MODEL), 1.0),
        'c_ctx': nrm((D_MODEL,), 1.0),
        'norm1_g': 1.0 + nrm((DEPTH, D_MODEL), 0.02),
        'norm2_g': 1.0 + nrm((DEPTH, D_MODEL), 0.02),
        'ada_w': nrm((DEPTH, D_MODEL, 6 * D_MODEL), 0.01),
        'ada_b': nrm((DEPTH, 6 * D_MODEL), 0.01),
        'w_in': nrm((DEPTH, D_MODEL, IN_WIDTH), D_MODEL ** -0.5),
        'hy_conv_w': nrm((DEPTH, 3, HY_IN), 3 ** -0.5),
        'hy_conv_b': nrm((DEPTH, HY_IN), 0.01),
        'hy_w1': nrm((DEPTH, HY_EMB, HY_FILT_HIDDEN), HY_EMB ** -0.5),
        'hy_b1': nrm((DEPTH, HY_FILT_HIDDEN), 0.1),
        'hy_w2': nrm((DEPTH, HY_FILT_HIDDEN, HY_FILT_HIDDEN), HY_FILT_HIDDEN ** -0.5),
        'hy_b2': nrm((DEPTH, HY_FILT_HIDDEN), 0.1),
        'hy_w3': nrm((DEPTH, HY_FILT_HIDDEN, HY_ORDER * 2 * HY_WIDTH), 0.02 * HY_FILT_HIDDEN ** -0.5),
        'hy_freq': 1.0 + nrm((DEPTH, 2, HY_FILT_HIDDEN), 0.02),
        'hy_bias': nrm((DEPTH, HY_ORDER, HY_WIDTH), 1.0),
        'sg_norm_g': 1.0 + nrm((DEPTH, SG_WIDTH), 0.02),
        'sg_w': nrm((DEPTH, SG_HEADS, CHUNK, CHUNK), CHUNK ** -0.5),
        'sg_b': 1.0 + nrm((DEPTH, SG_HEADS, CHUNK), 0.1),
        'qn_g': 1.0 + nrm((DEPTH, DA_HEAD_DIM), 0.02),
        'kn_g': 1.0 + nrm((DEPTH, DA_HEAD_DIM), 0.02),
        'lam_p': nrm((DEPTH, 4, DA_HEAD_DIM), 0.1),
        'subln_g': 1.0 + nrm((DEPTH, DA_V_DIM), 0.02),
        'w_out': nrm((DEPTH, MIX_WIDTH, D_MODEL), MIX_WIDTH ** -0.5),
        'ffn_w1': nrm((DEPTH, D_MODEL, 2 * FFN_HIDDEN), D_MODEL ** -0.5),
        'ffn_w2': nrm((DEPTH, FFN_HIDDEN, D_MODEL), FFN_HIDDEN ** -0.5),
    }


def reference(x, c, ctx, c_ctx, norm1_g, norm2_g, ada_w, ada_b, w_in, hy_conv_w, hy_conv_b,
              hy_w1, hy_b1, hy_w2, hy_b2, hy_w3, hy_freq, hy_bias, sg_norm_g, sg_w, sg_b,
              qn_g, kn_g, lam_p, subln_g, w_out, ffn_w1, ffn_w2):
    rows = x.shape[1] // GRID_W
    row, col = grid_positions(rows)
    da_off = HY_IN + SG_IN
    for i in range(DEPTH):
        last = i == DEPTH - 1
        lam_init = 0.8 - 0.6 * math.exp(-0.3 * i)
        sh1_l, sc1_l, g1_l, sh2_l, sc2_l, g2_l = [m[:, None, :] for m in adaln(c, ada_w[i], ada_b[i])]
        sh1_c, sc1_c, g1_c, sh2_c, sc2_c, g2_c = adaln(c_ctx, ada_w[i], ada_b[i])
        hy_params = (hy_conv_w[i], hy_conv_b[i], hy_w1[i], hy_b1[i], hy_w2[i], hy_b2[i],
                     hy_w3[i], hy_freq[i], hy_bias[i])
        sg_params = (sg_norm_g[i], sg_w[i], sg_b[i])
        lp = lam_p[i].astype(jnp.float32)
        lam = jnp.exp(jnp.dot(lp[0], lp[1])) - jnp.exp(jnp.dot(lp[2], lp[3])) + lam_init

        h_l = rmsnorm(x, norm1_g[i]) * (1.0 + sc1_l) + sh1_l
        h_c = rmsnorm(ctx, norm1_g[i]) * (1.0 + sc1_c) + sh1_c
        p_l = h_l @ w_in[i]
        p_c = h_c @ w_in[i]
        a_l = p_l[..., da_off:]
        a_c = p_c[..., da_off:]
        q_l = rope_2d(qk_heads(a_l[..., :DA_WIDTH], qn_g[i]), row, col)
        k_l = rope_2d(qk_heads(a_l[..., DA_WIDTH:2 * DA_WIDTH], kn_g[i]), row, col)
        v_l = v_heads(a_l[..., 2 * DA_WIDTH:])
        k_c = qk_heads(a_c[..., DA_WIDTH:2 * DA_WIDTH], kn_g[i])
        v_c = v_heads(a_c[..., 2 * DA_WIDTH:])
        k_all = jnp.concatenate([k_c, k_l], axis=1)
        v_all = jnp.concatenate([v_c, v_l], axis=1)
        att_l = diff_attention(q_l, k_all, v_all, lam, subln_g[i], lam_init)
        x = x + g1_l * mix_out(p_l, att_l, hy_params, sg_params, w_out[i])
        x = x + g2_l * swiglu(rmsnorm(x, norm2_g[i]) * (1.0 + sc2_l) + sh2_l, ffn_w1[i], ffn_w2[i])

        if not last:
            q_c = qk_heads(a_c[..., :DA_WIDTH], qn_g[i])
            att_c = diff_attention(q_c, k_c, v_c, lam, subln_g[i], lam_init)
            ctx = ctx + g1_c * mix_out(p_c, att_c, hy_params, sg_params, w_out[i])
            ctx = ctx + g2_c * swiglu(rmsnorm(ctx, norm2_g[i]) * (1.0 + sc2_c) + sh2_c, ffn_w1[i], ffn_w2[i])
    return x
```

```python
import functools
import math

import numpy as np
import jax
import jax.numpy as jnp
from jax import lax
from jax.experimental import pallas as pl
from jax.experimental.pallas import tpu as pltpu

F32 = jnp.float32
BF16 = jnp.bfloat16

GRID_W = 64
EPS = 1e-6
HY_WIDTH = 256
HY_ORDER = 2
HY_POS_BANDS = 16
HY_DECAY_TARGET = 1e-2
HY_FAST_DECAY_PCT = 0.3
HY_SLOW_DECAY_PCT = 1.5
SG_HEADS = 4
SG_WIDTH = 256
CHUNK = 128
DA_HEADS = 4
DA_WIDTH = 512
DA_V_DIM = 128
DA_HEAD_DIM = 64
ROPE_BASE = 10000.0
HY_IN = 3 * HY_WIDTH
SG_IN = 2 * SG_WIDTH
LANES = 128
DFT_N2 = 128
VMEM_LIMIT = 56 * 1024 * 1024
NEG_BIG = -1e30


def _cparams(*sem):
    return pltpu.CompilerParams(dimension_semantics=sem, vmem_limit_bytes=VMEM_LIMIT)


def _dot(a, b):
    return jnp.dot(a, b, preferred_element_type=F32)


def _sigmoid(x):
    return 1.0 / (1.0 + jnp.exp(-x))


def _modulated_norm(x, g, sc, sh):
    y = x * lax.rsqrt(jnp.mean(x * x, axis=-1, keepdims=True) + EPS)
    return (y * g) * (1.0 + sc) + sh


def _ada_kernel(c_ref, w_ref, b_ref, o_ref):
    cv = c_ref[...]
    s = (cv * _sigmoid(cv)).astype(BF16)
    o_ref[...] = _dot(s, w_ref[...].astype(BF16)) + b_ref[...]


def _ada(cvecs, ada_w, ada_b):
    depth, d, n = ada_w.shape
    tn = 1536
    return pl.pallas_call(
        _ada_kernel,
        grid=(depth, n // tn),
        in_specs=[pl.BlockSpec((8, d), lambda i, j: (0, 0)),
                  pl.BlockSpec((None, d, tn), lambda i, j: (i, 0, j)),
                  pl.BlockSpec((None, 1, tn), lambda i, j: (i, 0, j))],
        out_specs=pl.BlockSpec((None, 8, tn), lambda i, j: (i, 0, j)),
        out_shape=jax.ShapeDtypeStruct((depth, 8, n), F32),
        compiler_params=_cparams("parallel", "parallel"),
        name="adaln",
    )(cvecs, ada_w, ada_b)


def _group_mean_sq(t, seg):
    sq = t * t
    hi = sq.astype(BF16)
    lo = (sq - hi.astype(F32)).astype(BF16)
    return _dot(hi, seg) + _dot(lo, seg)


def _norm_rope(t, gain, seg, cos, sin, swap_fwd):
    tn = t * lax.rsqrt(_group_mean_sq(t, seg) + EPS) * gain
    rot = jnp.where(swap_fwd, pltpu.roll(tn, LANES - 16, 1), pltpu.roll(tn, 16, 1))
    return tn * cos + rot * sin


def _inproj_kernel(x_ref, g_ref, sc_ref, sh_ref, w_ref, cos_ref, sin_ref, qg_ref, kg_ref, seg_ref,
                   sgg_ref, sgw_ref, sgb_ref, hy_ref, sg_ref, q_ref, k_ref, v_ref):
    tm = x_ref.shape[0]
    hb = _modulated_norm(x_ref[...], g_ref[...], sc_ref[...], sh_ref[...]).astype(BF16)

    hy_ref[...] = _dot(hb, w_ref[:, :HY_IN])

    psg = _dot(hb, w_ref[:, HY_IN:HY_IN + SG_IN])
    ge = 0.5 * psg * (1.0 + lax.erf(psg * np.float32(math.sqrt(0.5))))
    u = ge[:, :SG_WIDTH]
    vv = ge[:, SG_WIDTH:]
    vn = (vv * lax.rsqrt(jnp.mean(vv * vv, axis=-1, keepdims=True) + EPS) * sgg_ref[...]).astype(BF16)
    head_of_lane = lax.broadcasted_iota(jnp.int32, (CHUNK, SG_WIDTH), 1) // (SG_WIDTH // SG_HEADS)
    for c in range(tm // CHUNK):
        rows = slice(c * CHUNK, (c + 1) * CHUNK)
        r = _dot(sgw_ref[...], vn[rows])
        mixed = sgb_ref[...]
        for h in range(SG_HEADS):
            mixed = mixed + jnp.where(head_of_lane == h, r[h * CHUNK:(h + 1) * CHUNK], 0.0)
        sg_ref[rows, :] = (u[rows] * mixed).astype(BF16)

    off = HY_IN + SG_IN
    lane = lax.broadcasted_iota(jnp.int32, (1, LANES), 1)
    swap_fwd = (lane % 32) < 16
    cos = cos_ref[...]
    sin = sin_ref[...]
    seg = seg_ref[...]
    for h in range(DA_HEADS):
        cols = slice(h * LANES, (h + 1) * LANES)
        pq = _dot(hb, w_ref[:, off + h * LANES: off + (h + 1) * LANES])
        q = _norm_rope(pq, qg_ref[...], seg, cos, sin, swap_fwd)
        q_ref[:, cols] = (q * (DA_HEAD_DIM ** -0.5)).astype(BF16)
        pk = _dot(hb, w_ref[:, off + DA_WIDTH + h * LANES: off + DA_WIDTH + (h + 1) * LANES])
        k_ref[:, cols] = _norm_rope(pk, kg_ref[...], seg, cos, sin, swap_fwd).astype(BF16)
    v_ref[...] = _dot(hb, w_ref[:, off + 2 * DA_WIDTH:]).astype(BF16)


def _inproj(x2d, seq, g, sc, sh, w, cos, sin, qg, kg, seg, sgg, sgw, sgb):
    m, d = x2d.shape
    n = w.shape[1]
    tm = min(512, seq)
    per = seq // tm
    row = lambda i: (i, 0)
    fixed = lambda i: (0, 0)
    bat = lambda i: (i // per, 0, 0)
    pos = lambda i: (i % per, 0)
    outs = [(HY_IN, F32), (SG_WIDTH, BF16), (DA_WIDTH, BF16), (DA_WIDTH, BF16), (DA_WIDTH, BF16)]
    return pl.pallas_call(
        _inproj_kernel,
        grid=(m // tm,),
        in_specs=[pl.BlockSpec((tm, d), row),
                  pl.BlockSpec((1, d), fixed),
                  pl.BlockSpec((None, 1, d), bat),
                  pl.BlockSpec((None, 1, d), bat),
                  pl.BlockSpec((d, n), fixed),
                  pl.BlockSpec((tm, LANES), pos),
                  pl.BlockSpec((tm, LANES), pos),
                  pl.BlockSpec((1, LANES), fixed),
                  pl.BlockSpec((1, LANES), fixed),
                  pl.BlockSpec((LANES, LANES), fixed),
                  pl.BlockSpec((1, SG_WIDTH), fixed),
                  pl.BlockSpec((SG_HEADS * CHUNK, CHUNK), fixed),
                  pl.BlockSpec((CHUNK, SG_WIDTH), fixed)],
        out_specs=[pl.BlockSpec((tm, c), row) for c, _ in outs],
        out_shape=[jax.ShapeDtypeStruct((m, c), dt) for c, dt in outs],
        compiler_params=_cparams("parallel"),
        name="inproj",
    )(x2d, g, sc, sh, w, cos, sin, qg, kg, seg, sgg, sgw, sgb)


def _attn_kernel(lp_ref, g_ref, q_ref, k_ref, v_ref, o_ref, *, tk, lam_init):
    tq = q_ref.shape[0]
    nk = k_ref.shape[0] // tk
    lp = lp_ref[...]
    lam = (jnp.exp(jnp.sum(lp[0:1] * lp[1:2], axis=-1, keepdims=True))
           - jnp.exp(jnp.sum(lp[2:3] * lp[3:4], axis=-1, keepdims=True)) + lam_init)
    q = q_ref[...]
    lane = lax.broadcasted_iota(jnp.int32, (1, LANES), 1)
    zero = jnp.zeros_like(q)
    qs = (jnp.where(lane < DA_HEAD_DIM, q, zero), jnp.where(lane >= DA_HEAD_DIM, q, zero))
    nt = (((1,), (1,)), ((), ()))

    def body(i, carry):
        start = pl.multiple_of(i * tk, tk)
        kb = k_ref[pl.ds(start, tk), :]
        vb = v_ref[pl.ds(start, tk), :]
        new = []
        for c in range(2):
            m, l, acc = carry[c]
            s = lax.dot_general(qs[c], kb, nt, preferred_element_type=F32)
            m_new = jnp.maximum(m, jnp.max(s, axis=-1, keepdims=True))
            alpha = jnp.exp(m - m_new)
            p = jnp.exp(s - m_new)
            l_new = alpha * l + jnp.sum(p, axis=-1, keepdims=True)
            acc_new = alpha * acc + _dot(p.astype(BF16), vb)
            new.append((m_new, l_new, acc_new))
        return tuple(new)

    init = tuple((jnp.full((tq, 1), NEG_BIG, F32), jnp.zeros((tq, 1), F32), jnp.zeros((tq, DA_V_DIM), F32))
                 for _ in range(2))
    (_, l1, a1), (_, l2, a2) = lax.fori_loop(0, nk, body, init)
    o = a1 / l1 - lam * (a2 / l2)
    o = o * lax.rsqrt(jnp.mean(o * o, axis=-1, keepdims=True) + EPS) * g_ref[...]
    o_ref[...] = (o * (1.0 - lam_init)).astype(BF16)


def _pick_tile(n, candidates):
    for c in candidates:
        if n % c == 0:
            return c
    raise ValueError(f"no tile for {n}")


def _attention(q, k, v, lam_p, subln_g, lam_init):
    b, lq, _ = q.shape
    lk = k.shape[1]
    tq = _pick_tile(lq, (256, 128))
    tk = _pick_tile(lk, (768, 512, 256, 128))
    kern = functools.partial(_attn_kernel, tk=tk, lam_init=lam_init)
    return pl.pallas_call(
        kern,
        grid=(b, DA_HEADS, lq // tq),
        in_specs=[pl.BlockSpec((4, DA_HEAD_DIM), lambda bi, h, i: (0, 0)),
                  pl.BlockSpec((1, DA_V_DIM), lambda bi, h, i: (0, 0)),
                  pl.BlockSpec((None, tq, LANES), lambda bi, h, i: (bi, i, h)),
                  pl.BlockSpec((None, lk, LANES), lambda bi, h, i: (bi, 0, h)),
                  pl.BlockSpec((None, lk, LANES), lambda bi, h, i: (bi, 0, h))],
        out_specs=pl.BlockSpec((None, tq, LANES), lambda bi, h, i: (bi, i, h)),
        out_shape=jax.ShapeDtypeStruct((b, lq, DA_WIDTH), BF16),
        compiler_params=_cparams("parallel", "parallel", "parallel"),
        name="diff_attention",
    )(lam_p, subln_g, q, k, v)


def _shortconv_kernel(p_ref, w_ref, b_ref, o_ref, *, tile):
    seq = p_ref.shape[0]
    w0 = w_ref[0:1, :]
    w1 = w_ref[1:2, :]
    w2 = w_ref[2:3, :]
    rid = lax.broadcasted_iota(jnp.int32, (tile, 1), 0)
    zero_row = jnp.zeros((1, p_ref.shape[1]), F32)
    for t in range(seq // tile):
        r0 = t * tile
        cur = p_ref[r0:r0 + tile, :]
        before = zero_row if t == 0 else p_ref[r0 - 1:r0, :]
        after = zero_row if r0 + tile == seq else p_ref[r0 + tile:r0 + tile + 1, :]
        prev = jnp.where(rid == 0, before, pltpu.roll(cur, 1, 0))
        nxt = jnp.where(rid == tile - 1, after, pltpu.roll(cur, tile - 1, 0))
        o_ref[r0:r0 + tile, :] = prev * w0 + cur * w1 + nxt * w2 + b_ref[...]


def _shortconv(p, w, bias):
    b, seq, c = p.shape
    tile = min(512, seq)
    return pl.pallas_call(
        functools.partial(_shortconv_kernel, tile=tile),
        grid=(b, c // LANES),
        in_specs=[pl.BlockSpec((None, seq, LANES), lambda bi, j: (bi, 0, j)),
                  pl.BlockSpec((3, LANES), lambda bi, j: (0, j)),
                  pl.BlockSpec((1, LANES), lambda bi, j: (0, j))],
        out_specs=pl.BlockSpec((None, seq, LANES), lambda bi, j: (bi, 0, j)),
        out_shape=jax.ShapeDtypeStruct((b, seq, c), F32),
        compiler_params=_cparams("parallel", "parallel"),
        name="hyena_shortconv",
    )(p, w, bias)


def _filter_kernel(f_ref, w1_ref, b1_ref, w2_ref, b2_ref, w3_ref, fr_ref, dl_ref, o_ref, *, seq):
    tl = f_ref.shape[0]
    feats = f_ref[...]
    h = jnp.sin(fr_ref[0:1, :] * (_dot(feats.astype(BF16), w1_ref[...]) + b1_ref[...]))
    h = jnp.sin(fr_ref[1:2, :] * (_dot(h.astype(BF16), w2_ref[...]) + b2_ref[...]))
    h = _dot(h.astype(BF16), w3_ref[...])
    window = jnp.exp(-feats[:, 0:1] * dl_ref[...])
    row = pl.program_id(0) * tl + lax.broadcasted_iota(jnp.int32, (tl, 1), 0)
    for o in range(HY_ORDER):
        base = o * 2 * HY_WIDTH
        fwd = h[:, base:base + HY_WIDTH]
        bwd = h[:, base + HY_WIDTH:base + 2 * HY_WIDTH]
        kf = jnp.where(row < seq, fwd, bwd) * window
        o_ref[:, o * HY_WIDTH:(o + 1) * HY_WIDTH] = jnp.where(row == seq, 0.0, kf)


def _hyena_features(seq):
    t = jnp.linspace(0.0, 1.0, seq, dtype=F32)[:, None]
    bands = jnp.linspace(1e-4, HY_POS_BANDS - 1, HY_POS_BANDS, dtype=F32)
    ang = (2.0 * math.pi / seq) * jnp.arange(seq, dtype=F32)[:, None] * bands[None, :]
    feats = jnp.concatenate([t, jnp.cos(ang), -jnp.sin(ang)], axis=-1)
    ext = jnp.concatenate([feats, feats[:1], feats[:0:-1]], axis=0)
    return jnp.pad(ext, ((0, 0), (0, LANES - ext.shape[1])))


def _hyena_filters(seq, w1, b1, w2, b2, w3, freq):
    feats = _hyena_features(seq)
    hid = w1.shape[1]
    w1p = jnp.zeros((LANES, LANES), F32).at[:w1.shape[0], :hid].set(w1).astype(BF16)
    w2p = jnp.zeros((LANES, LANES), F32).at[:hid, :hid].set(w2).astype(BF16)
    w3p = jnp.zeros((LANES, w3.shape[1]), F32).at[:hid].set(w3).astype(BF16)
    pad = lambda a: jnp.pad(a.reshape(-1, hid), ((0, 0), (0, LANES - hid)))
    min_decay = math.log(HY_DECAY_TARGET) / HY_SLOW_DECAY_PCT
    max_decay = math.log(HY_DECAY_TARGET) / HY_FAST_DECAY_PCT
    deltas = jnp.abs(jnp.linspace(min_decay, max_decay, HY_WIDTH, dtype=F32))[None, :]
    n = 2 * seq
    tl = min(512, n)
    fixed = lambda i: (0, 0)
    nout = HY_ORDER * HY_WIDTH
    return pl.pallas_call(
        functools.partial(_filter_kernel, seq=seq),
        grid=(n // tl,),
        in_specs=[pl.BlockSpec((tl, LANES), lambda i: (i, 0)),
                  pl.BlockSpec((LANES, LANES), fixed), pl.BlockSpec((1, LANES), fixed),
                  pl.BlockSpec((LANES, LANES), fixed), pl.BlockSpec((1, LANES), fixed),
                  pl.BlockSpec((LANES, w3.shape[1]), fixed), pl.BlockSpec((2, LANES), fixed),
                  pl.BlockSpec((1, HY_WIDTH), fixed)],
        out_specs=pl.BlockSpec((tl, nout), lambda i: (i, 0)),
        out_shape=jax.ShapeDtypeStruct((n, nout), F32),
        compiler_params=_cparams("parallel"),
        name="hyena_filter_mlp",
    )(feats, w1p, pad(b1), w2p, pad(b2), w3p, pad(freq), deltas)


def _dft_tables(seq):
    n = 2 * seq
    n1 = n // DFT_N2
    idx1 = np.arange(n1)
    idx2 = np.arange(DFT_N2)
    f1 = np.exp(-2j * np.pi * np.outer(idx1, idx1) / n1)
    tw = np.exp(-2j * np.pi * np.outer(idx1, idx2) / n)
    f2 = np.exp(-2j * np.pi * np.outer(idx2, idx2) / DFT_N2)
    return n1, f1, tw, f2


def _rows_block(nin):
    return max(8, LANES // nin)


def _stage_a_kernel(u_ref, kr_ref, ki_ref, ar_ref, ai_ref):
    nin, s, c = u_ref.shape
    u = u_ref[...].reshape(nin * s, c).astype(BF16)
    ar_ref[...] = _dot(kr_ref[...], u).reshape(ar_ref.shape)
    ai_ref[...] = _dot(ki_ref[...], u).reshape(ai_ref.shape)


def _stage_a(u, seq):
    b, rows, c = u.shape
    n1, f1, _, _ = _dft_tables(seq)
    nin = rows // DFT_N2
    s = _rows_block(nin)
    eye = np.eye(s)
    kr = jnp.asarray(np.kron(f1.real[:, :nin], eye), F32).astype(BF16)
    ki = jnp.asarray(np.kron(f1.imag[:, :nin], eye), F32).astype(BF16)
    u4 = u.reshape(b, nin, DFT_N2, c)
    fixed = lambda bi, j: (0, 0)
    out_spec = pl.BlockSpec((None, n1, s, c), lambda bi, j: (bi, 0, j, 0))
    out_shape = jax.ShapeDtypeStruct((b, n1, DFT_N2, c), F32)
    return pl.pallas_call(
        _stage_a_kernel,
        grid=(b, DFT_N2 // s),
        in_specs=[pl.BlockSpec((None, nin, s, c), lambda bi, j: (bi, 0, j, 0)),
                  pl.BlockSpec((n1 * s, nin * s), fixed),
                  pl.BlockSpec((n1 * s, nin * s), fixed)],
        out_specs=[out_spec, out_spec],
        out_shape=[out_shape, out_shape],
        compiler_params=_cparams("parallel", "parallel"),
        name="hyena_dft_rows",
    )(u4, kr, ki)


def _slab_forward(ar, ai, tr, ti, fcat):
    a_re = (ar * tr - ai * ti).astype(BF16)
    a_im = (ar * ti + ai * tr).astype(BF16)
    p = _dot(fcat, a_re)
    q = _dot(fcat, a_im)
    return p[:DFT_N2] - q[DFT_N2:], q[:DFT_N2] + p[DFT_N2:]


def _spectrum_kernel(ar_ref, ai_ref, tr_ref, ti_ref, f_ref, xr_ref, xi_ref, *, scale):
    for j in range(ar_ref.shape[0]):
        xr, xi = _slab_forward(ar_ref[j], ai_ref[j], tr_ref[j], ti_ref[j], f_ref[...])
        xr_ref[j] = xr * scale
        xi_ref[j] = xi * scale


def _slab_conv_kernel(ar_ref, ai_ref, kr_ref, ki_ref, tr_ref, ti_ref, f_ref, dr_ref, di_ref):
    fcat = f_ref[...]
    for j in range(ar_ref.shape[0]):
        tr = tr_ref[j]
        ti = ti_ref[j]
        xr, xi = _slab_forward(ar_ref[j], ai_ref[j], tr, ti, fcat)
        kr = kr_ref[j]
        ki = ki_ref[j]
        zr = (xr * kr - xi * ki).astype(BF16)
        zi = (xr * ki + xi * kr).astype(BF16)
        p = _dot(fcat, zr)
        q = _dot(fcat, zi)
        dr = p[:DFT_N2] + q[DFT_N2:]
        di = q[:DFT_N2] - p[DFT_N2:]
        dr_ref[j] = dr * tr + di * ti
        di_ref[j] = di * tr - dr * ti


def _slab_consts(seq):
    n1, _, tw, f2 = _dft_tables(seq)
    tr = jnp.asarray(tw.real, F32)[:, :, None]
    ti = jnp.asarray(tw.imag, F32)[:, :, None]
    fcat = jnp.asarray(np.concatenate([f2.real, f2.imag], axis=0), F32).astype(BF16)
    return n1, tr, ti, fcat


def _spectrum(ar, ai, seq):
    n1, tr, ti, fcat = _slab_consts(seq)
    c = ar.shape[-1]
    g = min(8, n1)
    slab = pl.BlockSpec((None, g, DFT_N2, c), lambda i: (0, i, 0, 0))
    twid = pl.BlockSpec((g, DFT_N2, 1), lambda i: (i, 0, 0))
    out_shape = jax.ShapeDtypeStruct((n1, DFT_N2, c), F32)
    out_spec = pl.BlockSpec((g, DFT_N2, c), lambda i: (i, 0, 0))
    return pl.pallas_call(
        functools.partial(_spectrum_kernel, scale=1.0 / (2 * seq)),
        grid=(n1 // g,),
        in_specs=[slab, slab, twid, twid, pl.BlockSpec((2 * DFT_N2, DFT_N2), lambda i: (0, 0))],
        out_specs=[out_spec, out_spec],
        out_shape=[out_shape, out_shape],
        compiler_params=_cparams("parallel"),
        name="hyena_filter_spectrum",
    )(ar, ai, tr, ti, fcat)


def _slab_conv(ar, ai, kr, ki, order, seq):
    n1, tr, ti, fcat = _slab_consts(seq)
    b, _, _, c = ar.shape
    g = min(8, n1)
    slab = pl.BlockSpec((None, g, DFT_N2, c), lambda bi, i: (bi, i, 0, 0))
    filt = pl.BlockSpec((g, DFT_N2, c), lambda bi, i: (i, 0, order))
    twid = pl.BlockSpec((g, DFT_N2, 1), lambda bi, i: (i, 0, 0))
    out_shape = jax.ShapeDtypeStruct(ar.shape, F32)
    return pl.pallas_call(
        _slab_conv_kernel,
        grid=(b, n1 // g),
        in_specs=[slab, slab, filt, filt, twid, twid,
                  pl.BlockSpec((2 * DFT_N2, DFT_N2), lambda bi, i: (0, 0))],
        out_specs=[slab, slab],
        out_shape=[out_shape, out_shape],
        compiler_params=_cparams("parallel", "parallel"),
        name="hyena_dft_slabs",
    )(ar, ai, kr, ki, tr, ti, fcat)


def _stage_c_kernel(dr_ref, di_ref, kr_ref, ki_ref, u_ref, gate_ref, bias_ref, o_ref):
    n1, s, c = dr_ref.shape
    dr = dr_ref[...].reshape(n1 * s, c).astype(BF16)
    di = di_ref[...].reshape(n1 * s, c).astype(BF16)
    y = (_dot(kr_ref[...], dr) + _dot(ki_ref[...], di)).reshape(u_ref.shape)
    u = u_ref[...]
    o_ref[...] = (gate_ref[...] * (y + u * bias_ref[...])).astype(o_ref.dtype)


def _stage_c(dr, di, u, gate, bias, seq, out_dtype):
    b, n1, _, c = dr.shape
    _, f1, _, _ = _dft_tables(seq)
    nout = seq // DFT_N2
    s = _rows_block(nout)
    eye = np.eye(s)
    kr = jnp.asarray(np.kron(f1.real[:nout], eye), F32).astype(BF16)
    ki = jnp.asarray(np.kron(f1.imag[:nout], eye), F32).astype(BF16)
    fixed = lambda bi, j: (0, 0)
    dspec = pl.BlockSpec((None, n1, s, c), lambda bi, j: (bi, 0, j, 0))
    uspec = pl.BlockSpec((None, nout, s, c), lambda bi, j: (bi, 0, j, 0))
    y = pl.pallas_call(
        _stage_c_kernel,
        grid=(b, DFT_N2 // s),
        in_specs=[dspec, dspec,
                  pl.BlockSpec((nout * s, n1 * s), fixed), pl.BlockSpec((nout * s, n1 * s), fixed),
                  uspec, uspec, pl.BlockSpec((1, 1, c), lambda bi, j: (0, 0, 0))],
        out_specs=uspec,
        out_shape=jax.ShapeDtypeStruct((b, nout, DFT_N2, c), out_dtype),
        compiler_params=_cparams("parallel", "parallel"),
        name="hyena_idft_rows",
    )(dr, di, kr, ki, u.reshape(b, nout, DFT_N2, c), gate.reshape(b, nout, DFT_N2, c), bias.reshape(1, 1, c))
    return y.reshape(b, seq, c)


def _hyena(p, conv_w, conv_b, w1, b1, w2, b2, w3, freq, bias):
    b, seq, _ = p.shape
    pc = _shortconv(p, conv_w, conv_b.reshape(1, -1))
    x1 = pc[..., :HY_WIDTH]
    x2 = pc[..., HY_WIDTH:2 * HY_WIDTH]
    v = pc[..., 2 * HY_WIDTH:]
    kt = _hyena_filters(seq, w1, b1, w2, b2, w3, freq)
    far, fai = _stage_a(kt[None], seq)
    kr, ki = _spectrum(far, fai, seq)
    ar, ai = _stage_a(v, seq)
    dr, di = _slab_conv(ar, ai, kr, ki, 0, seq)
    z = _stage_c(dr, di, v, x1, bias[0], seq, F32)
    ar, ai = _stage_a(z, seq)
    dr, di = _slab_conv(ar, ai, kr, ki, 1, seq)
    return _stage_c(dr, di, z, x2, bias[1], seq, BF16)


def _outproj_kernel(x_ref, hy_ref, sg_ref, at_ref, w_ref, g_ref, o_ref):
    mix = _dot(hy_ref[...], w_ref[:HY_WIDTH])
    mix = mix + _dot(sg_ref[...], w_ref[HY_WIDTH:HY_WIDTH + SG_WIDTH])
    mix = mix + _dot(at_ref[...], w_ref[HY_WIDTH + SG_WIDTH:])
    o_ref[...] = x_ref[...] + g_ref[...] * mix


def _outproj(x2d, seq, hy, sg, att, w, gate):
    m, d = x2d.shape
    tm = min(512, seq)
    per = seq // tm
    row = lambda i: (i, 0)
    return pl.pallas_call(
        _outproj_kernel,
        grid=(m // tm,),
        in_specs=[pl.BlockSpec((tm, d), row),
                  pl.BlockSpec((tm, HY_WIDTH), row),
                  pl.BlockSpec((tm, SG_WIDTH), row),
                  pl.BlockSpec((tm, DA_WIDTH), row),
                  pl.BlockSpec(w.shape, lambda i: (0, 0)),
                  pl.BlockSpec((None, 1, d), lambda i: (i // per, 0, 0))],
        out_specs=pl.BlockSpec((tm, d), row),
        out_shape=jax.ShapeDtypeStruct((m, d), F32),
        compiler_params=_cparams("parallel"),
        name="outproj",
    )(x2d, hy, sg, att, w, gate)


def _ffn_kernel(x_ref, g_ref, sc_ref, sh_ref, gate_ref, w1_ref, w2_ref, o_ref, *, chunk):
    x = x_ref[...]
    hb = _modulated_norm(x, g_ref[...], sc_ref[...], sh_ref[...]).astype(BF16)
    hidden = w2_ref.shape[0]
    acc = jnp.zeros(x.shape, F32)
    for c in range(hidden // chunk):
        a = _dot(hb, w1_ref[:, c * chunk:(c + 1) * chunk])
        up = _dot(hb, w1_ref[:, hidden + c * chunk:hidden + (c + 1) * chunk])
        act = (a * _sigmoid(a) * up).astype(BF16)
        acc = acc + _dot(act, w2_ref[c * chunk:(c + 1) * chunk, :])
    o_ref[...] = x + gate_ref[...] * acc


def _ffn(x2d, seq, g, sc, sh, gate, w1, w2):
    m, d = x2d.shape
    tm = min(512, seq)
    per = seq // tm
    row = lambda i: (i, 0)
    fixed = lambda i: (0, 0)
    bat = lambda i: (i // per, 0, 0)
    return pl.pallas_call(
        functools.partial(_ffn_kernel, chunk=256),
        grid=(m // tm,),
        in_specs=[pl.BlockSpec((tm, d), row),
                  pl.BlockSpec((1, d), fixed),
                  pl.BlockSpec((None, 1, d), bat),
                  pl.BlockSpec((None, 1, d), bat),
                  pl.BlockSpec((None, 1, d), bat),
                  pl.BlockSpec(w1.shape, fixed, pipeline_mode=pl.Buffered(1)),
                  pl.BlockSpec(w2.shape, fixed, pipeline_mode=pl.Buffered(1))],
        out_specs=pl.BlockSpec((tm, d), row),
        out_shape=jax.ShapeDtypeStruct((m, d), F32),
        compiler_params=_cparams("parallel"),
        name="ffn",
    )(x2d, g, sc, sh, gate, w1, w2)


def _rope_tables(seq):
    t = jnp.arange(seq, dtype=jnp.int32)
    half = DA_HEAD_DIM // 4
    inv_freq = ROPE_BASE ** (-jnp.arange(half, dtype=F32) / half)
    ang_row = (t // GRID_W).astype(F32)[:, None] * inv_freq[None, :]
    ang_col = (t % GRID_W).astype(F32)[:, None] * inv_freq[None, :]
    cos = jnp.concatenate([jnp.cos(ang_row)] * 2 + [jnp.cos(ang_col)] * 2, axis=-1)
    sin = jnp.concatenate([-jnp.sin(ang_row), jnp.sin(ang_row), -jnp.sin(ang_col), jnp.sin(ang_col)], axis=-1)
    return jnp.tile(cos, (1, 2)), jnp.tile(sin, (1, 2))


def kernel(x, c, ctx, c_ctx, norm1_g, norm2_g, ada_w, ada_b, w_in, hy_conv_w, hy_conv_b, hy_w1, hy_b1,
           hy_w2, hy_b2, hy_w3, hy_freq, hy_bias, sg_norm_g, sg_w, sg_b, qn_g, kn_g, lam_p, subln_g,
           w_out, ffn_w1, ffn_w2):
    batch, seq, d = x.shape
    ctx_len = ctx.shape[1]
    depth = w_in.shape[0]
    assert batch + 1 <= 8 and seq % CHUNK == 0 and ctx_len % CHUNK == 0

    cvecs = jnp.zeros((8, d), F32).at[:batch].set(c).at[batch].set(c_ctx)
    mods = _ada(cvecs, ada_w, ada_b[:, None, :])

    cos_l, sin_l = _rope_tables(seq)
    cos_c = jnp.ones((ctx_len, LANES), F32)
    sin_c = jnp.zeros((ctx_len, LANES), F32)
    seg_np = np.kron(np.eye(LANES // DA_HEAD_DIM), np.full((DA_HEAD_DIM, DA_HEAD_DIM), 1.0 / DA_HEAD_DIM))
    seg = jnp.asarray(seg_np, F32).astype(BF16)

    xs = x.reshape(batch * seq, d)
    cs = ctx.reshape(batch * ctx_len, d)
    for i in range(depth):
        last = i == depth - 1
        lam_init = 0.8 - 0.6 * math.exp(-0.3 * i)
        mod_l = [m[:, None, :] for m in jnp.split(mods[i, :batch], 6, axis=-1)]
        mod_c = [jnp.broadcast_to(m[None], (batch, 1, d)) for m in jnp.split(mods[i, batch:batch + 1], 6, axis=-1)]
        w_in_b = w_in[i].astype(BF16)
        w_out_b = w_out[i].astype(BF16)
        w1_b = ffn_w1[i].astype(BF16)
        w2_b = ffn_w2[i].astype(BF16)
        qg = jnp.tile(qn_g[i], 2)[None, :]
        kg = jnp.tile(kn_g[i], 2)[None, :]
        sgw = sg_w[i].reshape(SG_HEADS * CHUNK, CHUNK).astype(BF16)
        sgb = jnp.repeat(sg_b[i].T, SG_WIDTH // SG_HEADS, axis=1)
        hy_params = (hy_conv_w[i], hy_conv_b[i], hy_w1[i], hy_b1[i], hy_w2[i], hy_b2[i], hy_w3[i],
                     hy_freq[i], hy_bias[i])

        def project(tokens, n_tok, mod, cos, sin):
            return _inproj(tokens, n_tok, norm1_g[i][None, :], mod[1], mod[0], w_in_b, cos, sin, qg, kg, seg,
                           sg_norm_g[i][None, :], sgw, sgb)

        def finish(tokens, n_tok, mod, p_hy, sg, att):
            hy = _hyena(p_hy.reshape(batch, n_tok, HY_IN), *hy_params).reshape(batch * n_tok, HY_WIDTH)
            y = _outproj(tokens, n_tok, hy, sg, att.reshape(batch * n_tok, DA_WIDTH), w_out_b, mod[2])
            return _ffn(y, n_tok, norm2_g[i][None, :], mod[4], mod[3], mod[5], w1_b, w2_b)

        hy_l, sg_l, q_l, k_l, v_l = project(xs, seq, mod_l, cos_l, sin_l)
        hy_c, sg_c, q_c, k_c, v_c = project(cs, ctx_len, mod_c, cos_c, sin_c)
        k_c3 = k_c.reshape(batch, ctx_len, DA_WIDTH)
        v_c3 = v_c.reshape(batch, ctx_len, DA_WIDTH)
        k_all = jnp.concatenate([k_c3, k_l.reshape(batch, seq, DA_WIDTH)], axis=1)
        v_all = jnp.concatenate([v_c3, v_l.reshape(batch, seq, DA_WIDTH)], axis=1)
        att_l = _attention(q_l.reshape(batch, seq, DA_WIDTH), k_all, v_all, lam_p[i], subln_g[i][None, :], lam_init)
        xs = finish(xs, seq, mod_l, hy_l, sg_l, att_l)
        if not last:
            att_c = _attention(q_c.reshape(batch, ctx_len, DA_WIDTH), k_c3, v_c3, lam_p[i], subln_g[i][None, :],
                               lam_init)
            cs = finish(cs, ctx_len, mod_c, hy_c, sg_c, att_c)
    return xs.reshape(batch, seq, d)
```

```python
import functools
import math

import numpy as np
import jax
import jax.numpy as jnp
from jax import lax
from jax.experimental import pallas as pl
from jax.experimental.pallas import tpu as pltpu

F32 = jnp.float32
BF16 = jnp.bfloat16

GRID_W = 64
EPS = 1e-6
HY_WIDTH = 256
HY_ORDER = 2
HY_POS_BANDS = 16
HY_DECAY_TARGET = 1e-2
HY_FAST_DECAY_PCT = 0.3
HY_SLOW_DECAY_PCT = 1.5
SG_HEADS = 4
SG_WIDTH = 256
CHUNK = 128
DA_HEADS = 4
DA_WIDTH = 512
DA_V_DIM = 128
DA_HEAD_DIM = 64
ROPE_BASE = 10000.0
HY_IN = 3 * HY_WIDTH
SG_IN = 2 * SG_WIDTH
LANES = 128
DFT_N2 = 128
VMEM_LIMIT = 56 * 1024 * 1024
NEG_BIG = -1e30
Q_SCALE = math.log2(math.e) * DA_HEAD_DIM ** -0.5


def _cparams(*sem):
    return pltpu.CompilerParams(dimension_semantics=sem, vmem_limit_bytes=VMEM_LIMIT)


def _dot(a, b):
    return jnp.dot(a, b, preferred_element_type=F32)


def _sigmoid(x):
    return 1.0 / (1.0 + jnp.exp(-x))


def _modulated_norm(x, g, sc, sh):
    y = x * lax.rsqrt(jnp.mean(x * x, axis=-1, keepdims=True) + EPS)
    return (y * g) * (1.0 + sc) + sh


def _ada_kernel(c_ref, w_ref, b_ref, o_ref):
    cv = c_ref[...]
    s = (cv * _sigmoid(cv)).astype(BF16)
    o_ref[...] = _dot(s, w_ref[...].astype(BF16)) + b_ref[...]


def _ada(cvecs, ada_w, ada_b):
    depth, d, n = ada_w.shape
    tn = 1536
    return pl.pallas_call(
        _ada_kernel,
        grid=(depth, n // tn),
        in_specs=[pl.BlockSpec((8, d), lambda i, j: (0, 0)),
                  pl.BlockSpec((None, d, tn), lambda i, j: (i, 0, j)),
                  pl.BlockSpec((None, 1, tn), lambda i, j: (i, 0, j))],
        out_specs=pl.BlockSpec((None, 8, tn), lambda i, j: (i, 0, j)),
        out_shape=jax.ShapeDtypeStruct((depth, 8, n), F32),
        compiler_params=_cparams("parallel", "parallel"),
        name="adaln",
    )(cvecs, ada_w, ada_b)


def _group_mean_sq(t, seg):
    sq = t * t
    hi = sq.astype(BF16)
    lo = (sq - hi.astype(F32)).astype(BF16)
    return _dot(hi, seg) + _dot(lo, seg)


def _norm_rope(t, gain, seg, cos, sin, swap_fwd):
    tn = t * lax.rsqrt(_group_mean_sq(t, seg) + EPS) * gain
    rot = jnp.where(swap_fwd, pltpu.roll(tn, LANES - 16, 1), pltpu.roll(tn, 16, 1))
    return tn * cos + rot * sin


def _inproj_kernel(x_ref, g_ref, sc_ref, sh_ref, w_ref, cos_ref, sin_ref, qg_ref, kg_ref, seg_ref,
                   sgg_ref, sgw_ref, sgb_ref, hy_ref, sg_ref, q_ref, k_ref, v_ref):
    tm = x_ref.shape[0]
    hb = _modulated_norm(x_ref[...], g_ref[...], sc_ref[...], sh_ref[...]).astype(BF16)

    hy_ref[...] = _dot(hb, w_ref[:, :HY_IN])

    psg = _dot(hb, w_ref[:, HY_IN:HY_IN + SG_IN])
    ge = 0.5 * psg * (1.0 + lax.erf(psg * np.float32(math.sqrt(0.5))))
    u = ge[:, :SG_WIDTH]
    vv = ge[:, SG_WIDTH:]
    vn = (vv * lax.rsqrt(jnp.mean(vv * vv, axis=-1, keepdims=True) + EPS) * sgg_ref[...]).astype(BF16)
    head_of_lane = lax.broadcasted_iota(jnp.int32, (CHUNK, SG_WIDTH), 1) // (SG_WIDTH // SG_HEADS)
    for c in range(tm // CHUNK):
        rows = slice(c * CHUNK, (c + 1) * CHUNK)
        r = _dot(sgw_ref[...], vn[rows])
        mixed = sgb_ref[...]
        for h in range(SG_HEADS):
            mixed = mixed + jnp.where(head_of_lane == h, r[h * CHUNK:(h + 1) * CHUNK], 0.0)
        sg_ref[rows, :] = (u[rows] * mixed).astype(BF16)

    off = HY_IN + SG_IN
    lane = lax.broadcasted_iota(jnp.int32, (1, LANES), 1)
    swap_fwd = (lane % 32) < 16
    cos = cos_ref[...]
    sin = sin_ref[...]
    seg = seg_ref[...]
    for h in range(DA_HEADS):
        cols = slice(h * LANES, (h + 1) * LANES)
        pq = _dot(hb, w_ref[:, off + h * LANES: off + (h + 1) * LANES])
        q = _norm_rope(pq, qg_ref[...], seg, cos, sin, swap_fwd)
        q_ref[:, cols] = (q * Q_SCALE).astype(BF16)
        pk = _dot(hb, w_ref[:, off + DA_WIDTH + h * LANES: off + DA_WIDTH + (h + 1) * LANES])
        k_ref[:, cols] = _norm_rope(pk, kg_ref[...], seg, cos, sin, swap_fwd).astype(BF16)
    v_ref[...] = _dot(hb, w_ref[:, off + 2 * DA_WIDTH:]).astype(BF16)


def _inproj(x2d, seq, g, sc, sh, w, cos, sin, qg, kg, seg, sgg, sgw, sgb):
    m, d = x2d.shape
    n = w.shape[1]
    tm = min(512, seq)
    per = seq // tm
    row = lambda i: (i, 0)
    fixed = lambda i: (0, 0)
    bat = lambda i: (i // per, 0, 0)
    pos = lambda i: (i % per, 0)
    outs = [(HY_IN, F32), (SG_WIDTH, BF16), (DA_WIDTH, BF16), (DA_WIDTH, BF16), (DA_WIDTH, BF16)]
    return pl.pallas_call(
        _inproj_kernel,
        grid=(m // tm,),
        in_specs=[pl.BlockSpec((tm, d), row),
                  pl.BlockSpec((1, d), fixed),
                  pl.BlockSpec((None, 1, d), bat),
                  pl.BlockSpec((None, 1, d), bat),
                  pl.BlockSpec((d, n), fixed),
                  pl.BlockSpec((tm, LANES), pos),
                  pl.BlockSpec((tm, LANES), pos),
                  pl.BlockSpec((1, LANES), fixed),
                  pl.BlockSpec((1, LANES), fixed),
                  pl.BlockSpec((LANES, LANES), fixed),
                  pl.BlockSpec((1, SG_WIDTH), fixed),
                  pl.BlockSpec((SG_HEADS * CHUNK, CHUNK), fixed),
                  pl.BlockSpec((CHUNK, SG_WIDTH), fixed)],
        out_specs=[pl.BlockSpec((tm, c), row) for c, _ in outs],
        out_shape=[jax.ShapeDtypeStruct((m, c), dt) for c, dt in outs],
        compiler_params=_cparams("parallel"),
        name="inproj",
    )(x2d, g, sc, sh, w, cos, sin, qg, kg, seg, sgg, sgw, sgb)


_NT = (((1,), (1,)), ((), ()))


def _lambda(lp_ref, lam_init):
    lp = lp_ref[...]
    return (jnp.exp(jnp.sum(lp[0:1] * lp[1:2], axis=-1, keepdims=True))
            - jnp.exp(jnp.sum(lp[2:3] * lp[3:4], axis=-1, keepdims=True)) + lam_init)


def _stacked_components(q):
    lane = lax.broadcasted_iota(jnp.int32, (1, LANES), 1)
    zero = jnp.zeros_like(q)
    return jnp.concatenate([jnp.where(lane < DA_HEAD_DIM, q, zero), jnp.where(lane >= DA_HEAD_DIM, q, zero)],
                           axis=0)


def _attn_finish(o1, o2, lam, g_ref, o_ref, lam_init):
    o = o1 - lam * o2
    o = o * lax.rsqrt(jnp.mean(o * o, axis=-1, keepdims=True) + EPS) * g_ref[...]
    o_ref[...] = (o * (1.0 - lam_init)).astype(BF16)


def _attn_bounded_kernel(lp_ref, g_ref, q_ref, k_ref, v_ref, o_ref, *, tk, lam_init):
    tq = q_ref.shape[0]
    nk = k_ref.shape[0] // tk
    qs = _stacked_components(q_ref[...])
    ones = jnp.ones((tk, LANES), BF16)

    acc = jnp.zeros((2 * tq, 2 * LANES), F32)
    for i in range(nk):
        kb = k_ref[i * tk:(i + 1) * tk, :]
        vb = jnp.concatenate([v_ref[i * tk:(i + 1) * tk, :], ones], axis=1)
        s = lax.dot_general(qs, kb, _NT, preferred_element_type=F32)
        acc = acc + _dot(jnp.exp2(s).astype(BF16), vb)
    o1 = acc[:tq, :LANES] / acc[:tq, LANES:]
    o2 = acc[tq:, :LANES] / acc[tq:, LANES:]
    _attn_finish(o1, o2, _lambda(lp_ref, lam_init), g_ref, o_ref, lam_init)


def _attn_online_kernel(lp_ref, g_ref, q_ref, k_ref, v_ref, o_ref, *, tk, lam_init):
    tq = q_ref.shape[0]
    nk = k_ref.shape[0] // tk
    qs = _stacked_components(q_ref[...])

    def body(i, carry):
        m, l, acc = carry
        start = pl.multiple_of(i * tk, tk)
        kb = k_ref[pl.ds(start, tk), :]
        vb = v_ref[pl.ds(start, tk), :]
        s = lax.dot_general(qs, kb, _NT, preferred_element_type=F32)
        m_new = jnp.maximum(m, jnp.max(s, axis=-1, keepdims=True))
        alpha = jnp.exp2(m - m_new)
        p = jnp.exp2(s - m_new)
        l_new = alpha * l + jnp.sum(p, axis=-1, keepdims=True)
        return m_new, l_new, alpha * acc + _dot(p.astype(BF16), vb)

    init = (jnp.full((2 * tq, 1), NEG_BIG, F32), jnp.zeros((2 * tq, 1), F32),
            jnp.zeros((2 * tq, DA_V_DIM), F32))
    _, l, acc = lax.fori_loop(0, nk, body, init)
    o = acc / l
    _attn_finish(o[:tq], o[tq:], _lambda(lp_ref, lam_init), g_ref, o_ref, lam_init)


def _pick_tile(n, candidates):
    for c in candidates:
        if n % c == 0:
            return c
    raise ValueError(f"no tile for {n}")


BOUNDED_SCORE_LIMIT = 56.0


def _attention(q, k, v, lam_p, subln_g, lam_init, qn_g, kn_g):
    bound = (math.sqrt(DA_HEAD_DIM) * math.log2(math.e) * 1.02) * jnp.max(jnp.abs(qn_g)) * jnp.max(jnp.abs(kn_g))
    return lax.cond(bound <= BOUNDED_SCORE_LIMIT,
                    functools.partial(_attention_call, _attn_bounded_kernel, lam_init),
                    functools.partial(_attention_call, _attn_online_kernel, lam_init),
                    q, k, v, lam_p, subln_g)


def _attention_call(body, lam_init, q, k, v, lam_p, subln_g):
    b, lq, _ = q.shape
    lk = k.shape[1]
    tq = _pick_tile(lq, (512, 256, 128))
    tk = _pick_tile(lk, (768, 512, 256, 128))
    kern = functools.partial(body, tk=tk, lam_init=lam_init)
    return pl.pallas_call(
        kern,
        grid=(b, DA_HEADS, lq // tq),
        in_specs=[pl.BlockSpec((4, DA_HEAD_DIM), lambda bi, h, i: (0, 0)),
                  pl.BlockSpec((1, DA_V_DIM), lambda bi, h, i: (0, 0)),
                  pl.BlockSpec((None, tq, LANES), lambda bi, h, i: (bi, i, h)),
                  pl.BlockSpec((None, lk, LANES), lambda bi, h, i: (bi, 0, h)),
                  pl.BlockSpec((None, lk, LANES), lambda bi, h, i: (bi, 0, h))],
        out_specs=pl.BlockSpec((None, tq, LANES), lambda bi, h, i: (bi, i, h)),
        out_shape=jax.ShapeDtypeStruct((b, lq, DA_WIDTH), BF16),
        compiler_params=_cparams("parallel", "parallel", "parallel"),
        name="diff_attention",
    )(lam_p, subln_g, q, k, v)


def _shortconv_kernel(p_ref, w_ref, b_ref, o_ref, *, tile):
    seq = p_ref.shape[0]
    w0 = w_ref[0:1, :]
    w1 = w_ref[1:2, :]
    w2 = w_ref[2:3, :]
    rid = lax.broadcasted_iota(jnp.int32, (tile, 1), 0)
    zero_row = jnp.zeros((1, p_ref.shape[1]), F32)
    for t in range(seq // tile):
        r0 = t * tile
        cur = p_ref[r0:r0 + tile, :]
        before = zero_row if t == 0 else p_ref[r0 - 1:r0, :]
        after = zero_row if r0 + tile == seq else p_ref[r0 + tile:r0 + tile + 1, :]
        prev = jnp.where(rid == 0, before, pltpu.roll(cur, 1, 0))
        nxt = jnp.where(rid == tile - 1, after, pltpu.roll(cur, tile - 1, 0))
        o_ref[r0:r0 + tile, :] = prev * w0 + cur * w1 + nxt * w2 + b_ref[...]


def _shortconv(p, w, bias):
    b, seq, c = p.shape
    tile = min(512, seq)
    return pl.pallas_call(
        functools.partial(_shortconv_kernel, tile=tile),
        grid=(b, c // LANES),
        in_specs=[pl.BlockSpec((None, seq, LANES), lambda bi, j: (bi, 0, j)),
                  pl.BlockSpec((3, LANES), lambda bi, j: (0, j)),
                  pl.BlockSpec((1, LANES), lambda bi, j: (0, j))],
        out_specs=pl.BlockSpec((None, seq, LANES), lambda bi, j: (bi, 0, j)),
        out_shape=jax.ShapeDtypeStruct((b, seq, c), F32),
        compiler_params=_cparams("parallel", "parallel"),
        name="hyena_shortconv",
    )(p, w, bias)


def _filter_kernel(f_ref, w1_ref, b1_ref, w2_ref, b2_ref, w3_ref, fr_ref, dl_ref, o_ref, *, seq):
    tl = f_ref.shape[0]
    feats = f_ref[...]
    h = jnp.sin(fr_ref[0:1, :] * (_dot(feats.astype(BF16), w1_ref[...]) + b1_ref[...]))
    h = jnp.sin(fr_ref[1:2, :] * (_dot(h.astype(BF16), w2_ref[...]) + b2_ref[...]))
    h = _dot(h.astype(BF16), w3_ref[...])
    window = jnp.exp(-feats[:, 0:1] * dl_ref[...])
    row = pl.program_id(0) * tl + lax.broadcasted_iota(jnp.int32, (tl, 1), 0)
    for o in range(HY_ORDER):
        base = o * 2 * HY_WIDTH
        fwd = h[:, base:base + HY_WIDTH]
        bwd = h[:, base + HY_WIDTH:base + 2 * HY_WIDTH]
        kf = jnp.where(row < seq, fwd, bwd) * window
        o_ref[:, o * HY_WIDTH:(o + 1) * HY_WIDTH] = jnp.where(row == seq, 0.0, kf)


def _hyena_features(seq):
    t = jnp.linspace(0.0, 1.0, seq, dtype=F32)[:, None]
    bands = jnp.linspace(1e-4, HY_POS_BANDS - 1, HY_POS_BANDS, dtype=F32)
    ang = (2.0 * math.pi / seq) * jnp.arange(seq, dtype=F32)[:, None] * bands[None, :]
    feats = jnp.concatenate([t, jnp.cos(ang), -jnp.sin(ang)], axis=-1)
    ext = jnp.concatenate([feats, feats[:1], feats[:0:-1]], axis=0)
    return jnp.pad(ext, ((0, 0), (0, LANES - ext.shape[1])))


def _hyena_filters(seq, w1, b1, w2, b2, w3, freq):
    feats = _hyena_features(seq)
    hid = w1.shape[1]
    w1p = jnp.zeros((LANES, LANES), F32).at[:w1.shape[0], :hid].set(w1).astype(BF16)
    w2p = jnp.zeros((LANES, LANES), F32).at[:hid, :hid].set(w2).astype(BF16)
    w3p = jnp.zeros((LANES, w3.shape[1]), F32).at[:hid].set(w3).astype(BF16)
    pad = lambda a: jnp.pad(a.reshape(-1, hid), ((0, 0), (0, LANES - hid)))
    min_decay = math.log(HY_DECAY_TARGET) / HY_SLOW_DECAY_PCT
    max_decay = math.log(HY_DECAY_TARGET) / HY_FAST_DECAY_PCT
    deltas = jnp.abs(jnp.linspace(min_decay, max_decay, HY_WIDTH, dtype=F32))[None, :]
    n = 2 * seq
    tl = min(512, n)
    fixed = lambda i: (0, 0)
    nout = HY_ORDER * HY_WIDTH
    return pl.pallas_call(
        functools.partial(_filter_kernel, seq=seq),
        grid=(n // tl,),
        in_specs=[pl.BlockSpec((tl, LANES), lambda i: (i, 0)),
                  pl.BlockSpec((LANES, LANES), fixed), pl.BlockSpec((1, LANES), fixed),
                  pl.BlockSpec((LANES, LANES), fixed), pl.BlockSpec((1, LANES), fixed),
                  pl.BlockSpec((LANES, w3.shape[1]), fixed), pl.BlockSpec((2, LANES), fixed),
                  pl.BlockSpec((1, HY_WIDTH), fixed)],
        out_specs=pl.BlockSpec((tl, nout), lambda i: (i, 0)),
        out_shape=jax.ShapeDtypeStruct((n, nout), F32),
        compiler_params=_cparams("parallel"),
        name="hyena_filter_mlp",
    )(feats, w1p, pad(b1), w2p, pad(b2), w3p, pad(freq), deltas)


def _dft_tables(seq):
    n = 2 * seq
    n1 = n // DFT_N2
    idx1 = np.arange(n1)
    idx2 = np.arange(DFT_N2)
    f1 = np.exp(-2j * np.pi * np.outer(idx1, idx1) / n1)
    tw = np.exp(-2j * np.pi * np.outer(idx1, idx2) / n)
    f2 = np.exp(-2j * np.pi * np.outer(idx2, idx2) / DFT_N2)
    return n1, f1, tw, f2


def _rows_block(nin):
    return max(8, LANES // nin)


def _stage_a_kernel(u_ref, kr_ref, ki_ref, ar_ref, ai_ref):
    nin, s, c = u_ref.shape
    u = u_ref[...].reshape(nin * s, c).astype(BF16)
    ar_ref[...] = _dot(kr_ref[...], u).reshape(ar_ref.shape)
    ai_ref[...] = _dot(ki_ref[...], u).reshape(ai_ref.shape)


def _stage_a(u, seq):
    b, rows, c = u.shape
    n1, f1, _, _ = _dft_tables(seq)
    nin = rows // DFT_N2
    s = _rows_block(nin)
    eye = np.eye(s)
    kr = jnp.asarray(np.kron(f1.real[:, :nin], eye), F32).astype(BF16)
    ki = jnp.asarray(np.kron(f1.imag[:, :nin], eye), F32).astype(BF16)
    u4 = u.reshape(b, nin, DFT_N2, c)
    fixed = lambda bi, j: (0, 0)
    out_spec = pl.BlockSpec((None, n1, s, c), lambda bi, j: (bi, 0, j, 0))
    out_shape = jax.ShapeDtypeStruct((b, n1, DFT_N2, c), F32)
    return pl.pallas_call(
        _stage_a_kernel,
        grid=(b, DFT_N2 // s),
        in_specs=[pl.BlockSpec((None, nin, s, c), lambda bi, j: (bi, 0, j, 0)),
                  pl.BlockSpec((n1 * s, nin * s), fixed),
                  pl.BlockSpec((n1 * s, nin * s), fixed)],
        out_specs=[out_spec, out_spec],
        out_shape=[out_shape, out_shape],
        compiler_params=_cparams("parallel", "parallel"),
        name="hyena_dft_rows",
    )(u4, kr, ki)


def _slab_forward(ar, ai, tr, ti, fcat):
    a_re = (ar * tr - ai * ti).astype(BF16)
    a_im = (ar * ti + ai * tr).astype(BF16)
    p = _dot(fcat, a_re)
    q = _dot(fcat, a_im)
    return p[:DFT_N2] - q[DFT_N2:], q[:DFT_N2] + p[DFT_N2:]


def _spectrum_kernel(ar_ref, ai_ref, tr_ref, ti_ref, f_ref, xr_ref, xi_ref, *, scale):
    for j in range(ar_ref.shape[0]):
        xr, xi = _slab_forward(ar_ref[j], ai_ref[j], tr_ref[j], ti_ref[j], f_ref[...])
        xr_ref[j] = xr * scale
        xi_ref[j] = xi * scale


def _slab_conv_kernel(ar_ref, ai_ref, kr_ref, ki_ref, tr_ref, ti_ref, f_ref, dr_ref, di_ref):
    fcat = f_ref[...]
    for j in range(ar_ref.shape[0]):
        tr = tr_ref[j]
        ti = ti_ref[j]
        xr, xi = _slab_forward(ar_ref[j], ai_ref[j], tr, ti, fcat)
        kr = kr_ref[j]
        ki = ki_ref[j]
        zr = (xr * kr - xi * ki).astype(BF16)
        zi = (xr * ki + xi * kr).astype(BF16)
        p = _dot(fcat, zr)
        q = _dot(fcat, zi)
        dr = p[:DFT_N2] + q[DFT_N2:]
        di = q[:DFT_N2] - p[DFT_N2:]
        dr_ref[j] = dr * tr + di * ti
        di_ref[j] = di * tr - dr * ti


def _slab_consts(seq):
    n1, _, tw, f2 = _dft_tables(seq)
    tr = jnp.asarray(tw.real, F32)[:, :, None]
    ti = jnp.asarray(tw.imag, F32)[:, :, None]
    fcat = jnp.asarray(np.concatenate([f2.real, f2.imag], axis=0), F32).astype(BF16)
    return n1, tr, ti, fcat


def _spectrum(ar, ai, seq):
    n1, tr, ti, fcat = _slab_consts(seq)
    c = ar.shape[-1]
    g = min(8, n1)
    slab = pl.BlockSpec((None, g, DFT_N2, c), lambda i: (0, i, 0, 0))
    twid = pl.BlockSpec((g, DFT_N2, 1), lambda i: (i, 0, 0))
    out_shape = jax.ShapeDtypeStruct((n1, DFT_N2, c), F32)
    out_spec = pl.BlockSpec((g, DFT_N2, c), lambda i: (i, 0, 0))
    return pl.pallas_call(
        functools.partial(_spectrum_kernel, scale=1.0 / (2 * seq)),
        grid=(n1 // g,),
        in_specs=[slab, slab, twid, twid, pl.BlockSpec((2 * DFT_N2, DFT_N2), lambda i: (0, 0))],
        out_specs=[out_spec, out_spec],
        out_shape=[out_shape, out_shape],
        compiler_params=_cparams("parallel"),
        name="hyena_filter_spectrum",
    )(ar, ai, tr, ti, fcat)


def _slab_conv(ar, ai, kr, ki, order, seq):
    n1, tr, ti, fcat = _slab_consts(seq)
    b, _, _, c = ar.shape
    g = min(8, n1)
    slab = pl.BlockSpec((None, g, DFT_N2, c), lambda bi, i: (bi, i, 0, 0))
    filt = pl.BlockSpec((g, DFT_N2, c), lambda bi, i: (i, 0, order))
    twid = pl.BlockSpec((g, DFT_N2, 1), lambda bi, i: (i, 0, 0))
    out_shape = jax.ShapeDtypeStruct(ar.shape, F32)
    return pl.pallas_call(
        _slab_conv_kernel,
        grid=(b, n1 // g),
        in_specs=[slab, slab, filt, filt, twid, twid,
                  pl.BlockSpec((2 * DFT_N2, DFT_N2), lambda bi, i: (0, 0))],
        out_specs=[slab, slab],
        out_shape=[out_shape, out_shape],
        compiler_params=_cparams("parallel", "parallel"),
        name="hyena_dft_slabs",
    )(ar, ai, kr, ki, tr, ti, fcat)


def _stage_c_kernel(dr_ref, di_ref, kr_ref, ki_ref, u_ref, gate_ref, bias_ref, o_ref):
    n1, s, c = dr_ref.shape
    dr = dr_ref[...].reshape(n1 * s, c).astype(BF16)
    di = di_ref[...].reshape(n1 * s, c).astype(BF16)
    y = (_dot(kr_ref[...], dr) + _dot(ki_ref[...], di)).reshape(u_ref.shape)
    u = u_ref[...]
    o_ref[...] = (gate_ref[...] * (y + u * bias_ref[...])).astype(o_ref.dtype)


def _stage_c(dr, di, u, gate, bias, seq, out_dtype):
    b, n1, _, c = dr.shape
    _, f1, _, _ = _dft_tables(seq)
    nout = seq // DFT_N2
    s = _rows_block(nout)
    eye = np.eye(s)
    kr = jnp.asarray(np.kron(f1.real[:nout], eye), F32).astype(BF16)
    ki = jnp.asarray(np.kron(f1.imag[:nout], eye), F32).astype(BF16)
    fixed = lambda bi, j: (0, 0)
    dspec = pl.BlockSpec((None, n1, s, c), lambda bi, j: (bi, 0, j, 0))
    uspec = pl.BlockSpec((None, nout, s, c), lambda bi, j: (bi, 0, j, 0))
    y = pl.pallas_call(
        _stage_c_kernel,
        grid=(b, DFT_N2 // s),
        in_specs=[dspec, dspec,
                  pl.BlockSpec((nout * s, n1 * s), fixed), pl.BlockSpec((nout * s, n1 * s), fixed),
                  uspec, uspec, pl.BlockSpec((1, 1, c), lambda bi, j: (0, 0, 0))],
        out_specs=uspec,
        out_shape=jax.ShapeDtypeStruct((b, nout, DFT_N2, c), out_dtype),
        compiler_params=_cparams("parallel", "parallel"),
        name="hyena_idft_rows",
    )(dr, di, kr, ki, u.reshape(b, nout, DFT_N2, c), gate.reshape(b, nout, DFT_N2, c), bias.reshape(1, 1, c))
    return y.reshape(b, seq, c)


def _hyena(p, conv_w, conv_b, w1, b1, w2, b2, w3, freq, bias):
    b, seq, _ = p.shape
    pc = _shortconv(p, conv_w, conv_b.reshape(1, -1))
    x1 = pc[..., :HY_WIDTH]
    x2 = pc[..., HY_WIDTH:2 * HY_WIDTH]
    v = pc[..., 2 * HY_WIDTH:]
    kt = _hyena_filters(seq, w1, b1, w2, b2, w3, freq)
    far, fai = _stage_a(kt[None], seq)
    kr, ki = _spectrum(far, fai, seq)
    ar, ai = _stage_a(v, seq)
    dr, di = _slab_conv(ar, ai, kr, ki, 0, seq)
    z = _stage_c(dr, di, v, x1, bias[0], seq, F32)
    ar, ai = _stage_a(z, seq)
    dr, di = _slab_conv(ar, ai, kr, ki, 1, seq)
    return _stage_c(dr, di, z, x2, bias[1], seq, BF16)


def _outproj_kernel(x_ref, hy_ref, sg_ref, at_ref, w_ref, g_ref, o_ref):
    mix = _dot(hy_ref[...], w_ref[:HY_WIDTH])
    mix = mix + _dot(sg_ref[...], w_ref[HY_WIDTH:HY_WIDTH + SG_WIDTH])
    mix = mix + _dot(at_ref[...], w_ref[HY_WIDTH + SG_WIDTH:])
    o_ref[...] = x_ref[...] + g_ref[...] * mix


def _outproj(x2d, seq, hy, sg, att, w, gate):
    m, d = x2d.shape
    tm = min(512, seq)
    per = seq // tm
    row = lambda i: (i, 0)
    return pl.pallas_call(
        _outproj_kernel,
        grid=(m // tm,),
        in_specs=[pl.BlockSpec((tm, d), row),
                  pl.BlockSpec((tm, HY_WIDTH), row),
                  pl.BlockSpec((tm, SG_WIDTH), row),
                  pl.BlockSpec((tm, DA_WIDTH), row),
                  pl.BlockSpec(w.shape, lambda i: (0, 0)),
                  pl.BlockSpec((None, 1, d), lambda i: (i // per, 0, 0))],
        out_specs=pl.BlockSpec((tm, d), row),
        out_shape=jax.ShapeDtypeStruct((m, d), F32),
        compiler_params=_cparams("parallel"),
        name="outproj",
    )(x2d, hy, sg, att, w, gate)


def _ffn_kernel(x_ref, g_ref, sc_ref, sh_ref, gate_ref, w1_ref, w2_ref, o_ref, *, chunk):
    x = x_ref[...]
    hb = _modulated_norm(x, g_ref[...], sc_ref[...], sh_ref[...]).astype(BF16)
    hidden = w2_ref.shape[0]
    acc = jnp.zeros(x.shape, F32)
    for c in range(hidden // chunk):
        a = _dot(hb, w1_ref[:, c * chunk:(c + 1) * chunk])
        up = _dot(hb, w1_ref[:, hidden + c * chunk:hidden + (c + 1) * chunk])
        act = (a * _sigmoid(a) * up).astype(BF16)
        acc = acc + _dot(act, w2_ref[c * chunk:(c + 1) * chunk, :])
    o_ref[...] = x + gate_ref[...] * acc


def _ffn(x2d, seq, g, sc, sh, gate, w1, w2):
    m, d = x2d.shape
    tm = min(512, seq)
    per = seq // tm
    row = lambda i: (i, 0)
    fixed = lambda i: (0, 0)
    bat = lambda i: (i // per, 0, 0)
    return pl.pallas_call(
        functools.partial(_ffn_kernel, chunk=256),
        grid=(m // tm,),
        in_specs=[pl.BlockSpec((tm, d), row),
                  pl.BlockSpec((1, d), fixed),
                  pl.BlockSpec((None, 1, d), bat),
                  pl.BlockSpec((None, 1, d), bat),
                  pl.BlockSpec((None, 1, d), bat),
                  pl.BlockSpec(w1.shape, fixed, pipeline_mode=pl.Buffered(1)),
                  pl.BlockSpec(w2.shape, fixed, pipeline_mode=pl.Buffered(1))],
        out_specs=pl.BlockSpec((tm, d), row),
        out_shape=jax.ShapeDtypeStruct((m, d), F32),
        compiler_params=_cparams("parallel"),
        name="ffn",
    )(x2d, g, sc, sh, gate, w1, w2)


def _rope_tables(seq):
    t = jnp.arange(seq, dtype=jnp.int32)
    half = DA_HEAD_DIM // 4
    inv_freq = ROPE_BASE ** (-jnp.arange(half, dtype=F32) / half)
    ang_row = (t // GRID_W).astype(F32)[:, None] * inv_freq[None, :]
    ang_col = (t % GRID_W).astype(F32)[:, None] * inv_freq[None, :]
    cos = jnp.concatenate([jnp.cos(ang_row)] * 2 + [jnp.cos(ang_col)] * 2, axis=-1)
    sin = jnp.concatenate([-jnp.sin(ang_row), jnp.sin(ang_row), -jnp.sin(ang_col), jnp.sin(ang_col)], axis=-1)
    return jnp.tile(cos, (1, 2)), jnp.tile(sin, (1, 2))


def kernel(x, c, ctx, c_ctx, norm1_g, norm2_g, ada_w, ada_b, w_in, hy_conv_w, hy_conv_b, hy_w1, hy_b1,
           hy_w2, hy_b2, hy_w3, hy_freq, hy_bias, sg_norm_g, sg_w, sg_b, qn_g, kn_g, lam_p, subln_g,
           w_out, ffn_w1, ffn_w2):
    batch, seq, d = x.shape
    ctx_len = ctx.shape[1]
    depth = w_in.shape[0]
    assert batch + 1 <= 8 and seq % CHUNK == 0 and ctx_len % CHUNK == 0

    cvecs = jnp.zeros((8, d), F32).at[:batch].set(c).at[batch].set(c_ctx)
    mods = _ada(cvecs, ada_w, ada_b[:, None, :])

    cos_l, sin_l = _rope_tables(seq)
    cos_c = jnp.ones((ctx_len, LANES), F32)
    sin_c = jnp.zeros((ctx_len, LANES), F32)
    seg_np = np.kron(np.eye(LANES // DA_HEAD_DIM), np.full((DA_HEAD_DIM, DA_HEAD_DIM), 1.0 / DA_HEAD_DIM))
    seg = jnp.asarray(seg_np, F32).astype(BF16)

    xs = x.reshape(batch * seq, d)
    cs = ctx.reshape(batch * ctx_len, d)
    for i in range(depth):
        last = i == depth - 1
        lam_init = 0.8 - 0.6 * math.exp(-0.3 * i)
        mod_l = [m[:, None, :] for m in jnp.split(mods[i, :batch], 6, axis=-1)]
        mod_c = [jnp.broadcast_to(m[None], (batch, 1, d)) for m in jnp.split(mods[i, batch:batch + 1], 6, axis=-1)]
        w_in_b = w_in[i].astype(BF16)
        w_out_b = w_out[i].astype(BF16)
        w1_b = ffn_w1[i].astype(BF16)
        w2_b = ffn_w2[i].astype(BF16)
        qg = jnp.tile(qn_g[i], 2)[None, :]
        kg = jnp.tile(kn_g[i], 2)[None, :]
        sgw = sg_w[i].reshape(SG_HEADS * CHUNK, CHUNK).astype(BF16)
        sgb = jnp.repeat(sg_b[i].T, SG_WIDTH // SG_HEADS, axis=1)
        hy_params = (hy_conv_w[i], hy_conv_b[i], hy_w1[i], hy_b1[i], hy_w2[i], hy_b2[i], hy_w3[i],
                     hy_freq[i], hy_bias[i])

        def project(tokens, n_tok, mod, cos, sin):
            return _inproj(tokens, n_tok, norm1_g[i][None, :], mod[1], mod[0], w_in_b, cos, sin, qg, kg, seg,
                           sg_norm_g[i][None, :], sgw, sgb)

        def finish(tokens, n_tok, mod, p_hy, sg, att):
            hy = _hyena(p_hy.reshape(batch, n_tok, HY_IN), *hy_params).reshape(batch * n_tok, HY_WIDTH)
            y = _outproj(tokens, n_tok, hy, sg, att.reshape(batch * n_tok, DA_WIDTH), w_out_b, mod[2])
            return _ffn(y, n_tok, norm2_g[i][None, :], mod[4], mod[3], mod[5], w1_b, w2_b)

        hy_l, sg_l, q_l, k_l, v_l = project(xs, seq, mod_l, cos_l, sin_l)
        hy_c, sg_c, q_c, k_c, v_c = project(cs, ctx_len, mod_c, cos_c, sin_c)
        k_c3 = k_c.reshape(batch, ctx_len, DA_WIDTH)
        v_c3 = v_c.reshape(batch, ctx_len, DA_WIDTH)
        k_all = jnp.concatenate([k_c3, k_l.reshape(batch, seq, DA_WIDTH)], axis=1)
        v_all = jnp.concatenate([v_c3, v_l.reshape(batch, seq, DA_WIDTH)], axis=1)
        att_l = _attention(q_l.reshape(batch, seq, DA_WIDTH), k_all, v_all, lam_p[i], subln_g[i][None, :], lam_init,
                           qn_g[i], kn_g[i])
        xs = finish(xs, seq, mod_l, hy_l, sg_l, att_l)
        if not last:
            att_c = _attention(q_c.reshape(batch, ctx_len, DA_WIDTH), k_c3, v_c3, lam_p[i], subln_g[i][None, :],
                               lam_init, qn_g[i], kn_g[i])
            cs = finish(cs, ctx_len, mod_c, hy_c, sg_c, att_c)
    return xs.reshape(batch, seq, d)
```

```python
import functools
import math

import numpy as np
import jax
import jax.numpy as jnp
from jax import lax
from jax.experimental import pallas as pl
from jax.experimental.pallas import tpu as pltpu

F32 = jnp.float32
BF16 = jnp.bfloat16

GRID_W = 64
EPS = 1e-6
HY_WIDTH = 256
HY_ORDER = 2
HY_POS_BANDS = 16
HY_DECAY_TARGET = 1e-2
HY_FAST_DECAY_PCT = 0.3
HY_SLOW_DECAY_PCT = 1.5
SG_HEADS = 4
SG_WIDTH = 256
CHUNK = 128
DA_HEADS = 4
DA_WIDTH = 512
DA_V_DIM = 128
DA_HEAD_DIM = 64
ROPE_BASE = 10000.0
HY_IN = 3 * HY_WIDTH
SG_IN = 2 * SG_WIDTH
LANES = 128
DFT_N2 = 128
VMEM_LIMIT = 56 * 1024 * 1024
NEG_BIG = -1e30
Q_SCALE = math.log2(math.e) * DA_HEAD_DIM ** -0.5


def _cparams(*sem):
    return pltpu.CompilerParams(dimension_semantics=sem, vmem_limit_bytes=VMEM_LIMIT)


def _dot(a, b):
    return jnp.dot(a, b, preferred_element_type=F32)


def _sigmoid(x):
    return 1.0 / (1.0 + jnp.exp(-x))


def _modulated_norm(x, g, sc, sh):
    y = x * lax.rsqrt(jnp.mean(x * x, axis=-1, keepdims=True) + EPS)
    return (y * g) * (1.0 + sc) + sh


def _ada_kernel(c_ref, w_ref, b_ref, o_ref):
    cv = c_ref[...]
    s = (cv * _sigmoid(cv)).astype(BF16)
    o_ref[...] = _dot(s, w_ref[...].astype(BF16)) + b_ref[...]


def _ada(cvecs, ada_w, ada_b):
    depth, d, n = ada_w.shape
    tn = 1536
    return pl.pallas_call(
        _ada_kernel,
        grid=(depth, n // tn),
        in_specs=[pl.BlockSpec((8, d), lambda i, j: (0, 0)),
                  pl.BlockSpec((None, d, tn), lambda i, j: (i, 0, j)),
                  pl.BlockSpec((None, 1, tn), lambda i, j: (i, 0, j))],
        out_specs=pl.BlockSpec((None, 8, tn), lambda i, j: (i, 0, j)),
        out_shape=jax.ShapeDtypeStruct((depth, 8, n), F32),
        compiler_params=_cparams("parallel", "parallel"),
        name="adaln",
    )(cvecs, ada_w, ada_b)


def _group_mean_sq(t, seg):
    sq = t * t
    hi = sq.astype(BF16)
    lo = (sq - hi.astype(F32)).astype(BF16)
    return _dot(jnp.concatenate([hi, lo], axis=1), seg)


def _norm_rope(t, gain, seg, cos, sin, swap_fwd):
    tn = t * lax.rsqrt(_group_mean_sq(t, seg) + EPS) * gain
    rot = jnp.where(swap_fwd, pltpu.roll(tn, LANES - 16, 1), pltpu.roll(tn, 16, 1))
    return tn * cos + rot * sin


def _inproj_kernel(x_ref, g_ref, sc_ref, sh_ref, w_ref, cos_ref, sin_ref, qg_ref, kg_ref, seg_ref,
                   sgg_ref, sgw_ref, sgb_ref, hy_ref, sg_ref, q_ref, k_ref, v_ref):
    tm = x_ref.shape[0]
    hb = _modulated_norm(x_ref[...], g_ref[...], sc_ref[...], sh_ref[...]).astype(BF16)

    hy_ref[...] = _dot(hb, w_ref[:, :HY_IN])

    psg = _dot(hb, w_ref[:, HY_IN:HY_IN + SG_IN])
    ge = 0.5 * psg * (1.0 + lax.erf(psg * np.float32(math.sqrt(0.5))))
    u = ge[:, :SG_WIDTH]
    vv = ge[:, SG_WIDTH:]
    vn = (vv * lax.rsqrt(jnp.mean(vv * vv, axis=-1, keepdims=True) + EPS) * sgg_ref[...]).astype(BF16)
    head_of_lane = lax.broadcasted_iota(jnp.int32, (CHUNK, SG_WIDTH), 1) // (SG_WIDTH // SG_HEADS)
    for c in range(tm // CHUNK):
        rows = slice(c * CHUNK, (c + 1) * CHUNK)
        r = _dot(sgw_ref[...], vn[rows])
        mixed = sgb_ref[...]
        for h in range(SG_HEADS):
            mixed = mixed + jnp.where(head_of_lane == h, r[h * CHUNK:(h + 1) * CHUNK], 0.0)
        sg_ref[rows, :] = (u[rows] * mixed).astype(BF16)

    off = HY_IN + SG_IN
    lane = lax.broadcasted_iota(jnp.int32, (1, LANES), 1)
    swap_fwd = (lane % 32) < 16
    cos = cos_ref[...]
    sin = sin_ref[...]
    seg = seg_ref[...]
    for h in range(DA_HEADS):
        cols = slice(h * LANES, (h + 1) * LANES)
        pq = _dot(hb, w_ref[:, off + h * LANES: off + (h + 1) * LANES])
        q = _norm_rope(pq, qg_ref[...], seg, cos, sin, swap_fwd)
        q_ref[:, cols] = (q * Q_SCALE).astype(BF16)
        pk = _dot(hb, w_ref[:, off + DA_WIDTH + h * LANES: off + DA_WIDTH + (h + 1) * LANES])
        k_ref[:, cols] = _norm_rope(pk, kg_ref[...], seg, cos, sin, swap_fwd).astype(BF16)
    v_ref[...] = _dot(hb, w_ref[:, off + 2 * DA_WIDTH:]).astype(BF16)


def _inproj_into_kernel(*refs):
    _inproj_kernel(*refs[:13], *refs[15:])


def _inproj(x2d, seq, g, sc, sh, w, cos, sin, qg, kg, seg, sgg, sgw, sgb, kv_rows, kv_off, kv_into=None):
    m, d = x2d.shape
    n = w.shape[1]
    tm = min(512, seq)
    per = seq // tm
    assert kv_off % tm == 0
    row = lambda i: (i, 0)
    fixed = lambda i: (0, 0)
    bat = lambda i: (i // per, 0, 0)
    pos = lambda i: (i % per, 0)
    outs = [(HY_IN, F32), (SG_WIDTH, BF16), (DA_WIDTH, BF16)]
    kv_spec = pl.BlockSpec((None, tm, DA_WIDTH), lambda i: (i // per, kv_off // tm + i % per, 0))
    kv_shape = jax.ShapeDtypeStruct((m // seq, kv_rows, DA_WIDTH), BF16)
    extra_specs = [] if kv_into is None else [pl.BlockSpec(memory_space=pl.ANY)] * 2
    extra_args = () if kv_into is None else tuple(kv_into)
    return pl.pallas_call(
        _inproj_kernel if kv_into is None else _inproj_into_kernel,
        grid=(m // tm,),
        input_output_aliases={} if kv_into is None else {13: 3, 14: 4},
        in_specs=[pl.BlockSpec((tm, d), row),
                  pl.BlockSpec((1, d), fixed),
                  pl.BlockSpec((None, 1, d), bat),
                  pl.BlockSpec((None, 1, d), bat),
                  pl.BlockSpec((d, n), fixed),
                  pl.BlockSpec((tm, LANES), pos),
                  pl.BlockSpec((tm, LANES), pos),
                  pl.BlockSpec((1, LANES), fixed),
                  pl.BlockSpec((1, LANES), fixed),
                  pl.BlockSpec((2 * LANES, LANES), fixed),
                  pl.BlockSpec((1, SG_WIDTH), fixed),
                  pl.BlockSpec((SG_HEADS * CHUNK, CHUNK), fixed),
                  pl.BlockSpec((CHUNK, SG_WIDTH), fixed)] + extra_specs,
        out_specs=[pl.BlockSpec((tm, c), row) for c, _ in outs] + [kv_spec, kv_spec],
        out_shape=[jax.ShapeDtypeStruct((m, c), dt) for c, dt in outs] + [kv_shape, kv_shape],
        compiler_params=_cparams("parallel"),
        name="inproj",
    )(x2d, g, sc, sh, w, cos, sin, qg, kg, seg, sgg, sgw, sgb, *extra_args)


_NT = (((1,), (1,)), ((), ()))


def _lambda(lp_ref, lam_init):
    lp = lp_ref[...]
    return (jnp.exp(jnp.sum(lp[0:1] * lp[1:2], axis=-1, keepdims=True))
            - jnp.exp(jnp.sum(lp[2:3] * lp[3:4], axis=-1, keepdims=True)) + lam_init)


def _stacked_components(q):
    lane = lax.broadcasted_iota(jnp.int32, (1, LANES), 1)
    zero = jnp.zeros_like(q)
    return jnp.concatenate([jnp.where(lane < DA_HEAD_DIM, q, zero), jnp.where(lane >= DA_HEAD_DIM, q, zero)],
                           axis=0)


def _attn_finish(o1, o2, lam, g_ref, o_ref, lam_init):
    o = o1 - lam * o2
    o = o * lax.rsqrt(jnp.mean(o * o, axis=-1, keepdims=True) + EPS) * g_ref[...]
    o_ref[...] = (o * (1.0 - lam_init)).astype(BF16)


def _attn_bounded_kernel(lp_ref, g_ref, q_ref, k_ref, v_ref, o_ref, *, tk, lam_init):
    tq = q_ref.shape[0]
    nk = k_ref.shape[0] // tk
    qs = _stacked_components(q_ref[...])
    ones = jnp.ones((tk, LANES), BF16)

    acc = jnp.zeros((2 * tq, 2 * LANES), F32)
    for i in range(nk):
        kb = k_ref[i * tk:(i + 1) * tk, :]
        vb = jnp.concatenate([v_ref[i * tk:(i + 1) * tk, :], ones], axis=1)
        s = lax.dot_general(qs, kb, _NT, preferred_element_type=F32)
        acc = acc + _dot(jnp.exp2(s).astype(BF16), vb)
    o1 = acc[:tq, :LANES] / acc[:tq, LANES:]
    o2 = acc[tq:, :LANES] / acc[tq:, LANES:]
    _attn_finish(o1, o2, _lambda(lp_ref, lam_init), g_ref, o_ref, lam_init)


def _attn_online_kernel(lp_ref, g_ref, q_ref, k_ref, v_ref, o_ref, *, tk, lam_init):
    tq = q_ref.shape[0]
    nk = k_ref.shape[0] // tk
    qs = _stacked_components(q_ref[...])

    def body(i, carry):
        m, l, acc = carry
        start = pl.multiple_of(i * tk, tk)
        kb = k_ref[pl.ds(start, tk), :]
        vb = v_ref[pl.ds(start, tk), :]
        s = lax.dot_general(qs, kb, _NT, preferred_element_type=F32)
        m_new = jnp.maximum(m, jnp.max(s, axis=-1, keepdims=True))
        alpha = jnp.exp2(m - m_new)
        p = jnp.exp2(s - m_new)
        l_new = alpha * l + jnp.sum(p, axis=-1, keepdims=True)
        return m_new, l_new, alpha * acc + _dot(p.astype(BF16), vb)

    init = (jnp.full((2 * tq, 1), NEG_BIG, F32), jnp.zeros((2 * tq, 1), F32),
            jnp.zeros((2 * tq, DA_V_DIM), F32))
    _, l, acc = lax.fori_loop(0, nk, body, init)
    o = acc / l
    _attn_finish(o[:tq], o[tq:], _lambda(lp_ref, lam_init), g_ref, o_ref, lam_init)


def _pick_tile(n, candidates):
    for c in candidates:
        if n % c == 0:
            return c
    raise ValueError(f"no tile for {n}")


BOUNDED_SCORE_LIMIT = 56.0


def _attention(q, k, v, lam_p, subln_g, lam_init, qn_g, kn_g, key_rows=None):
    bound = (math.sqrt(DA_HEAD_DIM) * math.log2(math.e) * 1.02) * jnp.max(jnp.abs(qn_g)) * jnp.max(jnp.abs(kn_g))
    return lax.cond(bound <= BOUNDED_SCORE_LIMIT,
                    functools.partial(_attention_call, _attn_bounded_kernel, lam_init, key_rows),
                    functools.partial(_attention_call, _attn_online_kernel, lam_init, key_rows),
                    q, k, v, lam_p, subln_g)


def _attention_call(body, lam_init, key_rows, q, k, v, lam_p, subln_g):
    b, lq, _ = q.shape
    first, lk = (0, k.shape[1]) if key_rows is None else key_rows
    assert first % lk == 0
    kblk = first // lk
    tq = _pick_tile(lq, (512, 256, 128))
    tk = _pick_tile(lk, (768, 512, 256, 128))
    kern = functools.partial(body, tk=tk, lam_init=lam_init)
    return pl.pallas_call(
        kern,
        grid=(b, DA_HEADS, lq // tq),
        in_specs=[pl.BlockSpec((4, DA_HEAD_DIM), lambda bi, h, i: (0, 0)),
                  pl.BlockSpec((1, DA_V_DIM), lambda bi, h, i: (0, 0)),
                  pl.BlockSpec((None, tq, LANES), lambda bi, h, i: (bi, i, h)),
                  pl.BlockSpec((None, lk, LANES), lambda bi, h, i: (bi, kblk, h)),
                  pl.BlockSpec((None, lk, LANES), lambda bi, h, i: (bi, kblk, h))],
        out_specs=pl.BlockSpec((None, tq, LANES), lambda bi, h, i: (bi, i, h)),
        out_shape=jax.ShapeDtypeStruct((b, lq, DA_WIDTH), BF16),
        compiler_params=_cparams("parallel", "parallel", "parallel"),
        name="diff_attention",
    )(lam_p, subln_g, q, k, v)


def _shortconv_kernel(*refs, tile):
    rid = lax.broadcasted_iota(jnp.int32, (tile, 1), 0)
    zero_row = jnp.zeros((1, LANES), F32)
    for part in range(3):
        p_ref, w_ref, b_ref = refs[3 * part:3 * part + 3]
        o_ref = refs[9 + part]
        seq = p_ref.shape[0]
        w0 = w_ref[0:1, :]
        w1 = w_ref[1:2, :]
        w2 = w_ref[2:3, :]
        for t in range(seq // tile):
            r0 = t * tile
            cur = p_ref[r0:r0 + tile, :]
            before = zero_row if t == 0 else p_ref[r0 - 1:r0, :]
            after = zero_row if r0 + tile == seq else p_ref[r0 + tile:r0 + tile + 1, :]
            prev = jnp.where(rid == 0, before, pltpu.roll(cur, 1, 0))
            nxt = jnp.where(rid == tile - 1, after, pltpu.roll(cur, tile - 1, 0))
            o_ref[r0:r0 + tile, :] = prev * w0 + cur * w1 + nxt * w2 + b_ref[...]


def _shortconv(p, w, bias):
    b, seq, _ = p.shape
    tile = min(512, seq)
    nblk = HY_WIDTH // LANES
    in_specs = []
    for part in range(3):
        in_specs += [pl.BlockSpec((None, seq, LANES), lambda bi, j, part=part: (bi, 0, part * nblk + j)),
                     pl.BlockSpec((3, LANES), lambda bi, j, part=part: (0, part * nblk + j)),
                     pl.BlockSpec((1, LANES), lambda bi, j, part=part: (0, part * nblk + j))]
    out_spec = pl.BlockSpec((None, seq, LANES), lambda bi, j: (bi, 0, j))
    out_shape = jax.ShapeDtypeStruct((b, seq, HY_WIDTH), F32)
    return pl.pallas_call(
        functools.partial(_shortconv_kernel, tile=tile),
        grid=(b, nblk),
        in_specs=in_specs,
        out_specs=[out_spec] * 3,
        out_shape=[out_shape] * 3,
        compiler_params=_cparams("parallel", "parallel"),
        name="hyena_shortconv",
    )(*([p, w, bias] * 3))


def _filter_kernel(f_ref, w1_ref, b1_ref, w2_ref, b2_ref, w3_ref, fr_ref, dl_ref, o_ref, *, seq):
    tl = f_ref.shape[0]
    half = LANES // 2
    nout = HY_ORDER * HY_WIDTH
    feats = f_ref[...]
    h = jnp.sin(fr_ref[0:1, :] * (_dot(feats.astype(BF16), w1_ref[...]) + b1_ref[...]))
    h = jnp.sin(fr_ref[1:2, :] * (_dot(h.astype(BF16), w2_ref[...]) + b2_ref[...]))
    h = _dot(h.astype(BF16), w3_ref[...])
    row = pl.program_id(0) * tl + lax.broadcasted_iota(jnp.int32, (tl, 1), 0)
    o_ref[0] = h[:, :nout] * jnp.exp(-feats[:, 0:1] * dl_ref[...])
    bwd = h[:, nout:] * jnp.exp(-feats[:, half:half + 1] * dl_ref[...])
    o_ref[1] = jnp.where(row == 0, 0.0, bwd)


def _hyena_features(seq):
    t = jnp.linspace(0.0, 1.0, seq, dtype=F32)[:, None]
    bands = jnp.linspace(1e-4, HY_POS_BANDS - 1, HY_POS_BANDS, dtype=F32)
    ang = (2.0 * math.pi / seq) * jnp.arange(seq, dtype=F32)[:, None] * bands[None, :]
    feats = jnp.concatenate([t, jnp.cos(ang), -jnp.sin(ang)], axis=-1)
    back = jnp.concatenate([feats[:1], feats[:0:-1]], axis=0)
    pad = ((0, 0), (0, LANES // 2 - feats.shape[1]))
    return jnp.concatenate([jnp.pad(feats, pad), jnp.pad(back, pad)], axis=1)


def _hyena_filters(seq, w1, b1, w2, b2, w3, freq):
    feats = _hyena_features(seq)
    nfeat, hid = w1.shape
    half = LANES // 2
    nout = HY_ORDER * HY_WIDTH
    assert nfeat <= half and hid <= half
    w1p = jnp.zeros((LANES, LANES), F32).at[:nfeat, :hid].set(w1).at[half:half + nfeat, half:half + hid].set(w1)
    w2p = jnp.zeros((LANES, LANES), F32).at[:hid, :hid].set(w2).at[half:half + hid, half:half + hid].set(w2)
    w3d = w3.reshape(hid, HY_ORDER, 2, HY_WIDTH)
    w3p = (jnp.zeros((LANES, 2 * nout), F32)
           .at[:hid, :nout].set(w3d[:, :, 0].reshape(hid, nout))
           .at[half:half + hid, nout:].set(w3d[:, :, 1].reshape(hid, nout)))
    both = lambda a: jnp.tile(jnp.pad(a.reshape(-1, hid), ((0, 0), (0, half - hid))), (1, 2))
    min_decay = math.log(HY_DECAY_TARGET) / HY_SLOW_DECAY_PCT
    max_decay = math.log(HY_DECAY_TARGET) / HY_FAST_DECAY_PCT
    deltas = jnp.abs(jnp.linspace(min_decay, max_decay, HY_WIDTH, dtype=F32))[None, :]
    tl = min(512, seq)
    fixed = lambda i: (0, 0)
    kt = pl.pallas_call(
        functools.partial(_filter_kernel, seq=seq),
        grid=(seq // tl,),
        in_specs=[pl.BlockSpec((tl, LANES), lambda i: (i, 0)),
                  pl.BlockSpec((LANES, LANES), fixed), pl.BlockSpec((1, LANES), fixed),
                  pl.BlockSpec((LANES, LANES), fixed), pl.BlockSpec((1, LANES), fixed),
                  pl.BlockSpec((LANES, 2 * nout), fixed), pl.BlockSpec((2, LANES), fixed),
                  pl.BlockSpec((1, nout), fixed)],
        out_specs=pl.BlockSpec((2, tl, nout), lambda i: (0, i, 0)),
        out_shape=jax.ShapeDtypeStruct((2, seq, nout), F32),
        compiler_params=_cparams("parallel"),
        name="hyena_filter_mlp",
    )(feats, w1p.astype(BF16), both(b1), w2p.astype(BF16), both(b2), w3p.astype(BF16), both(freq),
      jnp.tile(deltas, (1, HY_ORDER)))
    return kt.reshape(2 * seq, nout)


def _dft_tables(seq):
    n = 2 * seq
    n1 = n // DFT_N2
    idx1 = np.arange(n1)
    idx2 = np.arange(DFT_N2)
    f1 = np.exp(-2j * np.pi * np.outer(idx1, idx1) / n1)
    k = idx1[:, None, None] + n1 * idx2[None, :, None]
    g = np.exp(-2j * np.pi * (k * idx2[None, None, :] % n) / n)
    return n1, f1, g


def _rows_block(nin):
    s_mm = max(8, LANES // nin)
    return s_mm, max(16, s_mm)


def _kron_rows(mat, s):
    return jnp.asarray(np.kron(mat, np.eye(s)), F32).astype(BF16)


def _rows_real_kernel(u_ref, kr_ref, ki_ref, ar_ref, ai_ref):
    nin, s, c = u_ref.shape
    u = u_ref[...].reshape(nin * s, c).astype(BF16)
    ar_ref[...] = _dot(kr_ref[...], u).reshape(ar_ref.shape)
    ai_ref[...] = _dot(ki_ref[...], u).reshape(ai_ref.shape)


def _rows_real(u, seq):
    rows, c = u.shape
    n1, f1, _ = _dft_tables(seq)
    nin = rows // DFT_N2
    s, _ = _rows_block(nin)
    fixed = lambda j: (0, 0)
    out_spec = pl.BlockSpec((n1, s, c), lambda j: (0, j, 0))
    out_shape = jax.ShapeDtypeStruct((n1, DFT_N2, c), F32)
    return pl.pallas_call(
        _rows_real_kernel,
        grid=(DFT_N2 // s,),
        in_specs=[pl.BlockSpec((nin, s, c), lambda j: (0, j, 0)),
                  pl.BlockSpec((n1 * s, nin * s), fixed),
                  pl.BlockSpec((n1 * s, nin * s), fixed)],
        out_specs=[out_spec, out_spec],
        out_shape=[out_shape, out_shape],
        compiler_params=_cparams("parallel"),
        name="hyena_filter_dft_rows",
    )(u.reshape(nin, DFT_N2, c), _kron_rows(f1.real[:, :nin], s), _kron_rows(f1.imag[:, :nin], s))


def _rows_pair_kernel(u_ref, kc_ref, ar_ref, ai_ref, *, s_mm):
    _, nin, s_blk, c = u_ref.shape
    m = kc_ref.shape[0] // 2
    re, im = [], []
    for h in range(s_blk // s_mm):
        sub = slice(h * s_mm, (h + 1) * s_mm)
        p = _dot(kc_ref[...], u_ref[0, :, sub, :].reshape(nin * s_mm, c).astype(BF16))
        q = _dot(kc_ref[...], u_ref[1, :, sub, :].reshape(nin * s_mm, c).astype(BF16))
        re.append((p[:m] - q[m:]).reshape(m // s_mm, s_mm, c))
        im.append((q[:m] + p[m:]).reshape(m // s_mm, s_mm, c))
    ar_ref[...] = jnp.concatenate(re, axis=1).astype(BF16)
    ai_ref[...] = jnp.concatenate(im, axis=1).astype(BF16)


def _rows_pair(u, seq):
    b, rows, c = u.shape
    n1, f1, _ = _dft_tables(seq)
    nin = rows // DFT_N2
    s_mm, s_blk = _rows_block(nin)
    kcat = jnp.concatenate([_kron_rows(f1.real[:, :nin], s_mm), _kron_rows(f1.imag[:, :nin], s_mm)], axis=0)
    out_spec = pl.BlockSpec((None, n1, s_blk, c), lambda p, j: (p, 0, j, 0))
    out_shape = jax.ShapeDtypeStruct((b // 2, n1, DFT_N2, c), BF16)
    return pl.pallas_call(
        functools.partial(_rows_pair_kernel, s_mm=s_mm),
        grid=(b // 2, DFT_N2 // s_blk),
        in_specs=[pl.BlockSpec((2, nin, s_blk, c), lambda p, j: (p, 0, j, 0)),
                  pl.BlockSpec(kcat.shape, lambda p, j: (0, 0))],
        out_specs=[out_spec, out_spec],
        out_shape=[out_shape, out_shape],
        compiler_params=_cparams("parallel", "parallel"),
        name="hyena_dft_rows",
    )(u.reshape(b, nin, DFT_N2, c), kcat)


def _complex_apply(mcat, xr, xi):
    p = _dot(mcat, xr)
    q = _dot(mcat, xi)
    return p[:DFT_N2] - q[DFT_N2:], q[:DFT_N2] + p[DFT_N2:]


def _spectrum_kernel(ar_ref, ai_ref, g_ref, xr_ref, xi_ref, *, scale):
    for j in range(ar_ref.shape[0]):
        xr, xi = _complex_apply(g_ref[j], ar_ref[j].astype(BF16), ai_ref[j].astype(BF16))
        xr_ref[j] = xr * scale
        xi_ref[j] = xi * scale


def _slab_conv_kernel(ar_ref, ai_ref, kr_ref, ki_ref, g_ref, h_ref, dr_ref, di_ref):
    for j in range(ar_ref.shape[0]):
        xr, xi = _complex_apply(g_ref[j], ar_ref[j], ai_ref[j])
        kr = kr_ref[j]
        ki = ki_ref[j]
        zr = (xr * kr - xi * ki).astype(BF16)
        zi = (xr * ki + xi * kr).astype(BF16)
        dr, di = _complex_apply(h_ref[j], zr, zi)
        dr_ref[j] = dr.astype(BF16)
        di_ref[j] = di.astype(BF16)


def _slab_consts(seq):
    n1, _, g = _dft_tables(seq)
    gcat = jnp.asarray(np.concatenate([g.real, g.imag], axis=1), F32).astype(BF16)
    gt = np.conj(g).transpose(0, 2, 1)
    hcat = jnp.asarray(np.concatenate([gt.real, gt.imag], axis=1), F32).astype(BF16)
    return n1, gcat, hcat


def _spectrum(ar, ai, seq):
    n1, gcat, _ = _slab_consts(seq)
    c = ar.shape[-1]
    g = min(8, n1)
    slab = pl.BlockSpec((g, DFT_N2, c), lambda i: (i, 0, 0))
    return pl.pallas_call(
        functools.partial(_spectrum_kernel, scale=1.0 / (2 * seq)),
        grid=(n1 // g,),
        in_specs=[slab, slab, pl.BlockSpec((g, 2 * DFT_N2, DFT_N2), lambda i: (i, 0, 0))],
        out_specs=[slab, slab],
        out_shape=[jax.ShapeDtypeStruct((n1, DFT_N2, c), F32)] * 2,
        compiler_params=_cparams("parallel"),
        name="hyena_filter_spectrum",
    )(ar, ai, gcat)


def _slab_conv(ar, ai, kr, ki, order, seq):
    n1, gcat, hcat = _slab_consts(seq)
    npair, _, _, c = ar.shape
    g = min(8, n1)
    slab = pl.BlockSpec((None, g, DFT_N2, c), lambda p, i: (p, i, 0, 0))
    filt = pl.BlockSpec((g, DFT_N2, c), lambda p, i: (i, 0, order))
    mats = pl.BlockSpec((g, 2 * DFT_N2, DFT_N2), lambda p, i: (i, 0, 0))
    return pl.pallas_call(
        _slab_conv_kernel,
        grid=(npair, n1 // g),
        in_specs=[slab, slab, filt, filt, mats, mats],
        out_specs=[slab, slab],
        out_shape=[jax.ShapeDtypeStruct(ar.shape, BF16)] * 2,
        compiler_params=_cparams("parallel", "parallel"),
        name="hyena_dft_slabs",
    )(ar, ai, kr, ki, gcat, hcat)


def _rows_inverse_kernel(dr_ref, di_ref, kc_ref, u_ref, gate_ref, bias_ref, o_ref, *, s_mm):
    n1, s_blk, c = dr_ref.shape
    m = kc_ref.shape[0] // 2
    dr = dr_ref[...].astype(F32)
    di = di_ref[...].astype(F32)
    y0, y1 = [], []
    for h in range(s_blk // s_mm):
        sub = slice(h * s_mm, (h + 1) * s_mm)
        p = _dot(kc_ref[...], dr[:, sub, :].reshape(n1 * s_mm, c).astype(BF16))
        q = _dot(kc_ref[...], di[:, sub, :].reshape(n1 * s_mm, c).astype(BF16))
        y0.append((p[:m] + q[m:]).reshape(m // s_mm, s_mm, c))
        y1.append((q[:m] - p[m:]).reshape(m // s_mm, s_mm, c))
    for e, parts in enumerate((y0, y1)):
        y = jnp.concatenate(parts, axis=1)
        o_ref[e] = (gate_ref[e] * (y + u_ref[e] * bias_ref[...])).astype(o_ref.dtype)


def _rows_inverse(dr, di, u, gate, bias, seq, out_dtype):
    npair, n1, _, c = dr.shape
    _, f1, _ = _dft_tables(seq)
    nout = seq // DFT_N2
    s_mm, s_blk = _rows_block(nout)
    kcat = jnp.concatenate([_kron_rows(f1.real[:nout], s_mm), _kron_rows(f1.imag[:nout], s_mm)], axis=0)
    dspec = pl.BlockSpec((None, n1, s_blk, c), lambda p, j: (p, 0, j, 0))
    uspec = pl.BlockSpec((2, nout, s_blk, c), lambda p, j: (p, 0, j, 0))
    y = pl.pallas_call(
        functools.partial(_rows_inverse_kernel, s_mm=s_mm),
        grid=(npair, DFT_N2 // s_blk),
        in_specs=[dspec, dspec, pl.BlockSpec(kcat.shape, lambda p, j: (0, 0)),
                  uspec, uspec, pl.BlockSpec((1, 1, c), lambda p, j: (0, 0, 0))],
        out_specs=uspec,
        out_shape=jax.ShapeDtypeStruct((2 * npair, nout, DFT_N2, c), out_dtype),
        compiler_params=_cparams("parallel", "parallel"),
        name="hyena_idft_rows",
    )(dr, di, kcat, u.reshape(2 * npair, nout, DFT_N2, c), gate.reshape(2 * npair, nout, DFT_N2, c),
      bias.reshape(1, 1, c))
    return y.reshape(2 * npair, seq, c)


def _hyena(p, conv_w, conv_b, w1, b1, w2, b2, w3, freq, bias):
    b, seq, _ = p.shape
    assert b % 2 == 0, "batch elements are transformed in pairs"
    x1, x2, v = _shortconv(p, conv_w, conv_b.reshape(1, -1))
    kt = _hyena_filters(seq, w1, b1, w2, b2, w3, freq)
    kr, ki = _spectrum(*_rows_real(kt, seq), seq)
    dr, di = _slab_conv(*_rows_pair(v, seq), kr, ki, 0, seq)
    z = _rows_inverse(dr, di, v, x1, bias[0], seq, F32)
    dr, di = _slab_conv(*_rows_pair(z, seq), kr, ki, 1, seq)
    return _rows_inverse(dr, di, z, x2, bias[1], seq, BF16)


def _mix_ffn_kernel(x_ref, hy_ref, sg_ref, at_ref, wo_ref, g1_ref, n2_ref, sc_ref, sh_ref, g2_ref, w1_ref,
                    w2_ref, o_ref, *, chunk):
    mix = _dot(hy_ref[...], wo_ref[:HY_WIDTH])
    mix = mix + _dot(sg_ref[...], wo_ref[HY_WIDTH:HY_WIDTH + SG_WIDTH])
    mix = mix + _dot(at_ref[...], wo_ref[HY_WIDTH + SG_WIDTH:])
    x = x_ref[...] + g1_ref[...] * mix
    hb = _modulated_norm(x, n2_ref[...], sc_ref[...], sh_ref[...]).astype(BF16)
    hidden = w2_ref.shape[0]
    acc = jnp.zeros(x.shape, F32)
    for c in range(hidden // chunk):
        a = _dot(hb, w1_ref[:, c * chunk:(c + 1) * chunk])
        up = _dot(hb, w1_ref[:, hidden + c * chunk:hidden + (c + 1) * chunk])
        act = (a * _sigmoid(a) * up).astype(BF16)
        acc = acc + _dot(act, w2_ref[c * chunk:(c + 1) * chunk, :])
    o_ref[...] = x + g2_ref[...] * acc


def _mix_ffn(x2d, seq, hy, sg, att, w_out, g1, n2, sc, sh, g2, w1, w2):
    m, d = x2d.shape
    tm = min(512, seq)
    per = seq // tm
    row = lambda i: (i, 0)
    fixed = lambda i: (0, 0)
    bat = lambda i: (i // per, 0, 0)
    resident = lambda a: pl.BlockSpec(a.shape, fixed, pipeline_mode=pl.Buffered(1))
    return pl.pallas_call(
        functools.partial(_mix_ffn_kernel, chunk=256),
        grid=(m // tm,),
        in_specs=[pl.BlockSpec((tm, d), row),
                  pl.BlockSpec((tm, HY_WIDTH), row),
                  pl.BlockSpec((tm, SG_WIDTH), row),
                  pl.BlockSpec((tm, DA_WIDTH), row),
                  resident(w_out),
                  pl.BlockSpec((None, 1, d), bat),
                  pl.BlockSpec((1, d), fixed),
                  pl.BlockSpec((None, 1, d), bat),
                  pl.BlockSpec((None, 1, d), bat),
                  pl.BlockSpec((None, 1, d), bat),
                  resident(w1),
                  resident(w2)],
        out_specs=pl.BlockSpec((tm, d), row),
        out_shape=jax.ShapeDtypeStruct((m, d), F32),
        compiler_params=_cparams("parallel"),
        name="outproj_ffn",
    )(x2d, hy, sg, att, w_out, g1, n2, sc, sh, g2, w1, w2)


def _rope_tables(seq):
    t = jnp.arange(seq, dtype=jnp.int32)
    half = DA_HEAD_DIM // 4
    inv_freq = ROPE_BASE ** (-jnp.arange(half, dtype=F32) / half)
    ang_row = (t // GRID_W).astype(F32)[:, None] * inv_freq[None, :]
    ang_col = (t % GRID_W).astype(F32)[:, None] * inv_freq[None, :]
    cos = jnp.concatenate([jnp.cos(ang_row)] * 2 + [jnp.cos(ang_col)] * 2, axis=-1)
    sin = jnp.concatenate([-jnp.sin(ang_row), jnp.sin(ang_row), -jnp.sin(ang_col), jnp.sin(ang_col)], axis=-1)
    return jnp.tile(cos, (1, 2)), jnp.tile(sin, (1, 2))


def kernel(x, c, ctx, c_ctx, norm1_g, norm2_g, ada_w, ada_b, w_in, hy_conv_w, hy_conv_b, hy_w1, hy_b1,
           hy_w2, hy_b2, hy_w3, hy_freq, hy_bias, sg_norm_g, sg_w, sg_b, qn_g, kn_g, lam_p, subln_g,
           w_out, ffn_w1, ffn_w2):
    batch, seq, d = x.shape
    ctx_len = ctx.shape[1]
    depth = w_in.shape[0]
    assert batch + 1 <= 8 and seq % CHUNK == 0 and ctx_len % CHUNK == 0

    cvecs = jnp.zeros((8, d), F32).at[:batch].set(c).at[batch].set(c_ctx)
    mods = _ada(cvecs, ada_w, ada_b[:, None, :])

    cos_l, sin_l = _rope_tables(seq)
    cos_c = jnp.ones((ctx_len, LANES), F32)
    sin_c = jnp.zeros((ctx_len, LANES), F32)
    seg_np = np.kron(np.eye(LANES // DA_HEAD_DIM), np.full((DA_HEAD_DIM, DA_HEAD_DIM), 1.0 / DA_HEAD_DIM))
    seg = jnp.asarray(np.concatenate([seg_np, seg_np], axis=0), F32).astype(BF16)

    xs = x.reshape(batch * seq, d)
    cs = ctx.reshape(batch * ctx_len, d)
    for i in range(depth):
        last = i == depth - 1
        lam_init = 0.8 - 0.6 * math.exp(-0.3 * i)
        mod_l = [m[:, None, :] for m in jnp.split(mods[i, :batch], 6, axis=-1)]
        mod_c = [jnp.broadcast_to(m[None], (batch, 1, d)) for m in jnp.split(mods[i, batch:batch + 1], 6, axis=-1)]
        w_in_b = w_in[i].astype(BF16)
        w_out_b = w_out[i].astype(BF16)
        w1_b = ffn_w1[i].astype(BF16)
        w2_b = ffn_w2[i].astype(BF16)
        qg = jnp.tile(qn_g[i], 2)[None, :]
        kg = jnp.tile(kn_g[i], 2)[None, :]
        sgw = sg_w[i].reshape(SG_HEADS * CHUNK, CHUNK).astype(BF16)
        sgb = jnp.repeat(sg_b[i].T, SG_WIDTH // SG_HEADS, axis=1)
        hy_params = (hy_conv_w[i], hy_conv_b[i], hy_w1[i], hy_b1[i], hy_w2[i], hy_b2[i], hy_w3[i],
                     hy_freq[i], hy_bias[i])

        def project(tokens, n_tok, mod, cos, sin, kv_off, kv_into=None):
            return _inproj(tokens, n_tok, norm1_g[i][None, :], mod[1], mod[0], w_in_b, cos, sin, qg, kg, seg,
                           sg_norm_g[i][None, :], sgw, sgb, seq + ctx_len, kv_off, kv_into)

        def finish(tokens, n_tok, mod, p_hy, sg, att):
            hy = _hyena(p_hy.reshape(batch, n_tok, HY_IN), *hy_params).reshape(batch * n_tok, HY_WIDTH)
            return _mix_ffn(tokens, n_tok, hy, sg, att.reshape(batch * n_tok, DA_WIDTH), w_out_b, mod[2],
                            norm2_g[i][None, :], mod[4], mod[3], mod[5], w1_b, w2_b)

        hy_l, sg_l, q_l, k_all, v_all = project(xs, seq, mod_l, cos_l, sin_l, 0)
        hy_c, sg_c, q_c, k_all, v_all = project(cs, ctx_len, mod_c, cos_c, sin_c, seq, (k_all, v_all))
        attend = functools.partial(_attention, lam_p=lam_p[i], subln_g=subln_g[i][None, :], lam_init=lam_init,
                                   qn_g=qn_g[i], kn_g=kn_g[i])
        att_l = attend(q_l.reshape(batch, seq, DA_WIDTH), k_all, v_all)
        xs = finish(xs, seq, mod_l, hy_l, sg_l, att_l)
        if not last:
            att_c = attend(q_c.reshape(batch, ctx_len, DA_WIDTH), k_all, v_all, key_rows=(seq, ctx_len))
            cs = finish(cs, ctx_len, mod_c, hy_c, sg_c, att_c)
    return xs.reshape(batch, seq, d)
```

```python
import functools
import math

import numpy as np
import jax
import jax.numpy as jnp
from jax import lax
from jax.experimental import pallas as pl
from jax.experimental.pallas import tpu as pltpu

F32 = jnp.float32
BF16 = jnp.bfloat16

GRID_W = 64
EPS = 1e-6
HY_WIDTH = 256
HY_ORDER = 2
HY_POS_BANDS = 16
HY_DECAY_TARGET = 1e-2
HY_FAST_DECAY_PCT = 0.3
HY_SLOW_DECAY_PCT = 1.5
SG_HEADS = 4
SG_WIDTH = 256
CHUNK = 128
DA_HEADS = 4
DA_WIDTH = 512
DA_V_DIM = 128
DA_HEAD_DIM = 64
ROPE_BASE = 10000.0
HY_IN = 3 * HY_WIDTH
SG_IN = 2 * SG_WIDTH
LANES = 128
DFT_N2 = 128
VMEM_LIMIT = 56 * 1024 * 1024
NEG_BIG = -1e30
Q_SCALE = math.log2(math.e) * DA_HEAD_DIM ** -0.5
ATTN_TQ = 1024
ATTN_TK = 256


def _cparams(*sem):
    return pltpu.CompilerParams(dimension_semantics=sem, vmem_limit_bytes=VMEM_LIMIT)


def _dot(a, b):
    return jnp.dot(a, b, preferred_element_type=F32)


def _sigmoid(x):
    return 1.0 / (1.0 + jnp.exp(-x))


def _modulated_norm(x, g, sc, sh):
    y = x * lax.rsqrt(jnp.mean(x * x, axis=-1, keepdims=True) + EPS)
    return (y * g) * (1.0 + sc) + sh


def _ada_kernel(c_ref, w_ref, b_ref, o_ref):
    cv = c_ref[...]
    s = (cv * _sigmoid(cv)).astype(BF16)
    o_ref[...] = _dot(s, w_ref[...].astype(BF16)) + b_ref[...]


def _ada(cvecs, ada_w, ada_b):
    depth, d, n = ada_w.shape
    tn = 1536
    return pl.pallas_call(
        _ada_kernel,
        grid=(depth, n // tn),
        in_specs=[pl.BlockSpec((8, d), lambda i, j: (0, 0)),
                  pl.BlockSpec((None, d, tn), lambda i, j: (i, 0, j)),
                  pl.BlockSpec((None, 1, tn), lambda i, j: (i, 0, j))],
        out_specs=pl.BlockSpec((None, 8, tn), lambda i, j: (i, 0, j)),
        out_shape=jax.ShapeDtypeStruct((depth, 8, n), F32),
        compiler_params=_cparams("parallel", "parallel"),
        name="adaln",
    )(cvecs, ada_w, ada_b)


def _group_mean_sq(t, seg):
    sq = t * t
    hi = sq.astype(BF16)
    lo = (sq - hi.astype(F32)).astype(BF16)
    return _dot(jnp.concatenate([hi, lo], axis=1), seg)


def _norm_rope(t, gain, seg, cos, sin, swap_fwd):
    tn = t * lax.rsqrt(_group_mean_sq(t, seg) + EPS) * gain
    rot = jnp.where(swap_fwd, pltpu.roll(tn, LANES - 16, 1), pltpu.roll(tn, 16, 1))
    return tn * cos + rot * sin


def _inproj_kernel(x_ref, g_ref, sc_ref, sh_ref, w_ref, cos_ref, sin_ref, qg_ref, kg_ref, seg_ref,
                   sgg_ref, sgw_ref, sgb_ref, hy_ref, sg_ref, q_ref, k_ref, v_ref):
    tm = x_ref.shape[0]
    hb = _modulated_norm(x_ref[...], g_ref[...], sc_ref[...], sh_ref[...]).astype(BF16)

    hy_ref[...] = _dot(hb, w_ref[:, :HY_IN])

    psg = _dot(hb, w_ref[:, HY_IN:HY_IN + SG_IN])
    ge = 0.5 * psg * (1.0 + lax.erf(psg * np.float32(math.sqrt(0.5))))
    u = ge[:, :SG_WIDTH]
    vv = ge[:, SG_WIDTH:]
    vn = (vv * lax.rsqrt(jnp.mean(vv * vv, axis=-1, keepdims=True) + EPS) * sgg_ref[...]).astype(BF16)
    head_of_lane = lax.broadcasted_iota(jnp.int32, (CHUNK, SG_WIDTH), 1) // (SG_WIDTH // SG_HEADS)
    for c in range(tm // CHUNK):
        rows = slice(c * CHUNK, (c + 1) * CHUNK)
        r = _dot(sgw_ref[...], vn[rows])
        mixed = sgb_ref[...]
        for h in range(SG_HEADS):
            mixed = mixed + jnp.where(head_of_lane == h, r[h * CHUNK:(h + 1) * CHUNK], 0.0)
        sg_ref[rows, :] = (u[rows] * mixed).astype(BF16)

    off = HY_IN + SG_IN
    lane = lax.broadcasted_iota(jnp.int32, (1, LANES), 1)
    swap_fwd = (lane % 32) < 16
    cos = cos_ref[...]
    sin = sin_ref[...]
    seg = seg_ref[...]
    for h in range(DA_HEADS):
        cols = slice(h * LANES, (h + 1) * LANES)
        pq = _dot(hb, w_ref[:, off + h * LANES: off + (h + 1) * LANES])
        q = _norm_rope(pq, qg_ref[...], seg, cos, sin, swap_fwd)
        q_ref[:, cols] = (q * Q_SCALE).astype(BF16)
        pk = _dot(hb, w_ref[:, off + DA_WIDTH + h * LANES: off + DA_WIDTH + (h + 1) * LANES])
        k_ref[:, cols] = _norm_rope(pk, kg_ref[...], seg, cos, sin, swap_fwd).astype(BF16)
    v_ref[...] = _dot(hb, w_ref[:, off + 2 * DA_WIDTH:]).astype(BF16)


def _inproj(x2d, seq, layer, g, sc, sh, w, cos, sin, qg, kg, seg, sgg, sgw, sgb):
    m, d = x2d.shape
    n = w.shape[2]
    tm = min(512, seq)
    per = seq // tm
    row = lambda i: (i, 0)
    fixed = lambda i: (0, 0)
    bat = lambda i: (i // per, 0, 0)
    pos = lambda i: (i % per, 0)
    outs = [(HY_IN, F32), (SG_WIDTH, BF16), (DA_WIDTH, BF16), (DA_WIDTH, BF16), (DA_WIDTH, BF16)]
    return pl.pallas_call(
        _inproj_kernel,
        grid=(m // tm,),
        in_specs=[pl.BlockSpec((tm, d), row),
                  pl.BlockSpec((1, d), fixed),
                  pl.BlockSpec((None, 1, d), bat),
                  pl.BlockSpec((None, 1, d), bat),
                  pl.BlockSpec((None, d, n), lambda i: (layer, 0, 0)),
                  pl.BlockSpec((tm, LANES), pos),
                  pl.BlockSpec((tm, LANES), pos),
                  pl.BlockSpec((1, LANES), fixed),
                  pl.BlockSpec((1, LANES), fixed),
                  pl.BlockSpec((2 * LANES, LANES), fixed),
                  pl.BlockSpec((1, SG_WIDTH), fixed),
                  pl.BlockSpec((SG_HEADS * CHUNK, CHUNK), fixed),
                  pl.BlockSpec((CHUNK, SG_WIDTH), fixed)],
        out_specs=[pl.BlockSpec((tm, c), row) for c, _ in outs],
        out_shape=[jax.ShapeDtypeStruct((m, c), dt) for c, dt in outs],
        compiler_params=_cparams("parallel"),
        name="inproj",
    )(x2d, g, sc, sh, w, cos, sin, qg, kg, seg, sgg, sgw, sgb)


_NT = (((1,), (1,)), ((), ()))


def _lambda(lp_ref, lam_init):
    lp = lp_ref[...]
    return (jnp.exp(jnp.sum(lp[0:1] * lp[1:2], axis=-1, keepdims=True))
            - jnp.exp(jnp.sum(lp[2:3] * lp[3:4], axis=-1, keepdims=True)) + lam_init)


def _stacked_components(q):
    lane = lax.broadcasted_iota(jnp.int32, (1, LANES), 1)
    zero = jnp.zeros_like(q)
    return jnp.concatenate([jnp.where(lane < DA_HEAD_DIM, q, zero), jnp.where(lane >= DA_HEAD_DIM, q, zero)],
                           axis=0)


def _attn_finish(o1, o2, lam, g_ref, o_ref, lam_init):
    o = o1 - lam * o2
    o = o * lax.rsqrt(jnp.mean(o * o, axis=-1, keepdims=True) + EPS) * g_ref[...]
    o_ref[...] = (o * (1.0 - lam_init)).astype(BF16)


def _attn_bounded_kernel(lp_ref, g_ref, q_ref, *refs, tiles, lam_init):
    tq = q_ref.shape[0]
    o_ref = refs[-1]
    qs = _stacked_components(q_ref[...])
    acc = jnp.zeros((2 * tq, 2 * LANES), F32)
    for j, tk in enumerate(tiles):
        k_ref, v_ref = refs[2 * j], refs[2 * j + 1]
        ones = jnp.ones((tk, LANES), BF16)
        for i in range(k_ref.shape[0] // tk):
            kb = k_ref[i * tk:(i + 1) * tk, :]
            vb = jnp.concatenate([v_ref[i * tk:(i + 1) * tk, :], ones], axis=1)
            s = lax.dot_general(qs, kb, _NT, preferred_element_type=F32)
            acc = acc + _dot(jnp.exp2(s).astype(BF16), vb)
    o1 = acc[:tq, :LANES] / acc[:tq, LANES:]
    o2 = acc[tq:, :LANES] / acc[tq:, LANES:]
    _attn_finish(o1, o2, _lambda(lp_ref, lam_init), g_ref, o_ref, lam_init)


def _attn_online_kernel(lp_ref, g_ref, q_ref, *refs, tiles, lam_init):
    tq = q_ref.shape[0]
    o_ref = refs[-1]
    qs = _stacked_components(q_ref[...])
    carry = (jnp.full((2 * tq, 1), NEG_BIG, F32), jnp.zeros((2 * tq, 1), F32),
             jnp.zeros((2 * tq, DA_V_DIM), F32))
    for j, tk in enumerate(tiles):
        k_ref, v_ref = refs[2 * j], refs[2 * j + 1]

        def body(i, carry, k_ref=k_ref, v_ref=v_ref, tk=tk):
            m, l, acc = carry
            start = pl.multiple_of(i * tk, tk)
            kb = k_ref[pl.ds(start, tk), :]
            vb = v_ref[pl.ds(start, tk), :]
            s = lax.dot_general(qs, kb, _NT, preferred_element_type=F32)
            m_new = jnp.maximum(m, jnp.max(s, axis=-1, keepdims=True))
            alpha = jnp.exp2(m - m_new)
            p = jnp.exp2(s - m_new)
            l_new = alpha * l + jnp.sum(p, axis=-1, keepdims=True)
            return m_new, l_new, alpha * acc + _dot(p.astype(BF16), vb)

        carry = lax.fori_loop(0, k_ref.shape[0] // tk, body, carry)
    _, l, acc = carry
    o = acc / l
    _attn_finish(o[:tq], o[tq:], _lambda(lp_ref, lam_init), g_ref, o_ref, lam_init)


def _pick_tile(n, candidates):
    for c in candidates:
        if n % c == 0:
            return c
    raise ValueError(f"no tile for {n}")


BOUNDED_SCORE_LIMIT = 56.0


def _attention(q, kv, lam_p, subln_g, lam_init, qn_g, kn_g):
    bound = (math.sqrt(DA_HEAD_DIM) * math.log2(math.e) * 1.02) * jnp.max(jnp.abs(qn_g)) * jnp.max(jnp.abs(kn_g))
    flat = [a for pair in kv for a in pair]
    return lax.cond(bound <= BOUNDED_SCORE_LIMIT,
                    functools.partial(_attention_call, _attn_bounded_kernel, lam_init),
                    functools.partial(_attention_call, _attn_online_kernel, lam_init),
                    q, lam_p, subln_g, *flat)


def _attention_call(body, lam_init, q, lam_p, subln_g, *kv):
    b, lq, _ = q.shape
    tq = _pick_tile(lq, (ATTN_TQ, 512, 256, 128))
    tiles = tuple(_pick_tile(a.shape[1], (ATTN_TK, 512, 256, 128)) for a in kv[::2])
    whole = lambda a: pl.BlockSpec((None, a.shape[1], LANES), lambda bi, h, i: (bi, 0, h))
    return pl.pallas_call(
        functools.partial(body, tiles=tiles, lam_init=lam_init),
        grid=(b, DA_HEADS, lq // tq),
        in_specs=[pl.BlockSpec((4, DA_HEAD_DIM), lambda bi, h, i: (0, 0)),
                  pl.BlockSpec((1, DA_V_DIM), lambda bi, h, i: (0, 0)),
                  pl.BlockSpec((None, tq, LANES), lambda bi, h, i: (bi, i, h))] + [whole(a) for a in kv],
        out_specs=pl.BlockSpec((None, tq, LANES), lambda bi, h, i: (bi, i, h)),
        out_shape=jax.ShapeDtypeStruct((b, lq, DA_WIDTH), BF16),
        compiler_params=_cparams("parallel", "parallel", "parallel"),
        name="diff_attention",
    )(lam_p, subln_g, q, *kv)


def _shortconv_kernel(*refs, tile):
    rid = lax.broadcasted_iota(jnp.int32, (tile, 1), 0)
    zero_row = jnp.zeros((1, LANES), F32)
    for part in range(3):
        p_ref, w_ref, b_ref = refs[3 * part:3 * part + 3]
        o_ref = refs[9 + part]
        seq = p_ref.shape[0]
        w0 = w_ref[0:1, :]
        w1 = w_ref[1:2, :]
        w2 = w_ref[2:3, :]
        for t in range(seq // tile):
            r0 = t * tile
            cur = p_ref[r0:r0 + tile, :]
            before = zero_row if t == 0 else p_ref[r0 - 1:r0, :]
            after = zero_row if r0 + tile == seq else p_ref[r0 + tile:r0 + tile + 1, :]
            prev = jnp.where(rid == 0, before, pltpu.roll(cur, 1, 0))
            nxt = jnp.where(rid == tile - 1, after, pltpu.roll(cur, tile - 1, 0))
            o_ref[r0:r0 + tile, :] = prev * w0 + cur * w1 + nxt * w2 + b_ref[...]


def _shortconv(p, w, bias):
    b, seq, _ = p.shape
    tile = min(512, seq)
    nblk = HY_WIDTH // LANES
    in_specs = []
    for part in range(3):
        in_specs += [pl.BlockSpec((None, seq, LANES), lambda bi, j, part=part: (bi, 0, part * nblk + j)),
                     pl.BlockSpec((3, LANES), lambda bi, j, part=part: (0, part * nblk + j)),
                     pl.BlockSpec((1, LANES), lambda bi, j, part=part: (0, part * nblk + j))]
    out_spec = pl.BlockSpec((None, seq, LANES), lambda bi, j: (bi, 0, j))
    out_shape = jax.ShapeDtypeStruct((b, seq, HY_WIDTH), F32)
    return pl.pallas_call(
        functools.partial(_shortconv_kernel, tile=tile),
        grid=(b, nblk),
        in_specs=in_specs,
        out_specs=[out_spec] * 3,
        out_shape=[out_shape] * 3,
        compiler_params=_cparams("parallel", "parallel"),
        name="hyena_shortconv",
    )(*([p, w, bias] * 3))


def _filter_kernel(f_ref, w1_ref, b1_ref, w2_ref, b2_ref, w3_ref, fr_ref, dl_ref, o_ref, *, seq):
    tl = f_ref.shape[0]
    half = LANES // 2
    nout = HY_ORDER * HY_WIDTH
    feats = f_ref[...]
    h = jnp.sin(fr_ref[0:1, :] * (_dot(feats.astype(BF16), w1_ref[...]) + b1_ref[...]))
    h = jnp.sin(fr_ref[1:2, :] * (_dot(h.astype(BF16), w2_ref[...]) + b2_ref[...]))
    h = _dot(h.astype(BF16), w3_ref[...])
    row = pl.program_id(0) * tl + lax.broadcasted_iota(jnp.int32, (tl, 1), 0)
    o_ref[0] = h[:, :nout] * jnp.exp(-feats[:, 0:1] * dl_ref[...])
    bwd = h[:, nout:] * jnp.exp(-feats[:, half:half + 1] * dl_ref[...])
    o_ref[1] = jnp.where(row == 0, 0.0, bwd)


def _hyena_features(seq):
    t = jnp.linspace(0.0, 1.0, seq, dtype=F32)[:, None]
    bands = jnp.linspace(1e-4, HY_POS_BANDS - 1, HY_POS_BANDS, dtype=F32)
    ang = (2.0 * math.pi / seq) * jnp.arange(seq, dtype=F32)[:, None] * bands[None, :]
    feats = jnp.concatenate([t, jnp.cos(ang), -jnp.sin(ang)], axis=-1)
    back = jnp.concatenate([feats[:1], feats[:0:-1]], axis=0)
    pad = ((0, 0), (0, LANES // 2 - feats.shape[1]))
    return jnp.concatenate([jnp.pad(feats, pad), jnp.pad(back, pad)], axis=1)


def _hyena_filters(seq, w1, b1, w2, b2, w3, freq):
    feats = _hyena_features(seq)
    nfeat, hid = w1.shape
    half = LANES // 2
    nout = HY_ORDER * HY_WIDTH
    assert nfeat <= half and hid <= half
    w1p = jnp.zeros((LANES, LANES), F32).at[:nfeat, :hid].set(w1).at[half:half + nfeat, half:half + hid].set(w1)
    w2p = jnp.zeros((LANES, LANES), F32).at[:hid, :hid].set(w2).at[half:half + hid, half:half + hid].set(w2)
    w3d = w3.reshape(hid, HY_ORDER, 2, HY_WIDTH)
    w3p = (jnp.zeros((LANES, 2 * nout), F32)
           .at[:hid, :nout].set(w3d[:, :, 0].reshape(hid, nout))
           .at[half:half + hid, nout:].set(w3d[:, :, 1].reshape(hid, nout)))
    both = lambda a: jnp.tile(jnp.pad(a.reshape(-1, hid), ((0, 0), (0, half - hid))), (1, 2))
    min_decay = math.log(HY_DECAY_TARGET) / HY_SLOW_DECAY_PCT
    max_decay = math.log(HY_DECAY_TARGET) / HY_FAST_DECAY_PCT
    deltas = jnp.abs(jnp.linspace(min_decay, max_decay, HY_WIDTH, dtype=F32))[None, :]
    tl = min(512, seq)
    fixed = lambda i: (0, 0)
    kt = pl.pallas_call(
        functools.partial(_filter_kernel, seq=seq),
        grid=(seq // tl,),
        in_specs=[pl.BlockSpec((tl, LANES), lambda i: (i, 0)),
                  pl.BlockSpec((LANES, LANES), fixed), pl.BlockSpec((1, LANES), fixed),
                  pl.BlockSpec((LANES, LANES), fixed), pl.BlockSpec((1, LANES), fixed),
                  pl.BlockSpec((LANES, 2 * nout), fixed), pl.BlockSpec((2, LANES), fixed),
                  pl.BlockSpec((1, nout), fixed)],
        out_specs=pl.BlockSpec((2, tl, nout), lambda i: (0, i, 0)),
        out_shape=jax.ShapeDtypeStruct((2, seq, nout), F32),
        compiler_params=_cparams("parallel"),
        name="hyena_filter_mlp",
    )(feats, w1p.astype(BF16), both(b1), w2p.astype(BF16), both(b2), w3p.astype(BF16), both(freq),
      jnp.tile(deltas, (1, HY_ORDER)))
    return kt.reshape(2 * seq, nout)


def _dft_tables(seq):
    n = 2 * seq
    n1 = n // DFT_N2
    idx1 = np.arange(n1)
    idx2 = np.arange(DFT_N2)
    f1 = np.exp(-2j * np.pi * np.outer(idx1, idx1) / n1)
    k = idx1[:, None, None] + n1 * idx2[None, :, None]
    g = np.exp(-2j * np.pi * (k * idx2[None, None, :] % n) / n)
    return n1, f1, g


def _rows_block(nin):
    s_mm = max(8, LANES // nin)
    return s_mm, max(16, s_mm)


def _kron_rows(mat, s):
    return jnp.asarray(np.kron(mat, np.eye(s)), F32).astype(BF16)


def _rows_real_kernel(u_ref, kr_ref, ki_ref, ar_ref, ai_ref):
    nin, s, c = u_ref.shape
    u = u_ref[...].reshape(nin * s, c).astype(BF16)
    ar_ref[...] = _dot(kr_ref[...], u).reshape(ar_ref.shape)
    ai_ref[...] = _dot(ki_ref[...], u).reshape(ai_ref.shape)


def _rows_real(u, seq):
    rows, c = u.shape
    n1, f1, _ = _dft_tables(seq)
    nin = rows // DFT_N2
    s, _ = _rows_block(nin)
    fixed = lambda j: (0, 0)
    out_spec = pl.BlockSpec((n1, s, c), lambda j: (0, j, 0))
    out_shape = jax.ShapeDtypeStruct((n1, DFT_N2, c), F32)
    return pl.pallas_call(
        _rows_real_kernel,
        grid=(DFT_N2 // s,),
        in_specs=[pl.BlockSpec((nin, s, c), lambda j: (0, j, 0)),
                  pl.BlockSpec((n1 * s, nin * s), fixed),
                  pl.BlockSpec((n1 * s, nin * s), fixed)],
        out_specs=[out_spec, out_spec],
        out_shape=[out_shape, out_shape],
        compiler_params=_cparams("parallel"),
        name="hyena_filter_dft_rows",
    )(u.reshape(nin, DFT_N2, c), _kron_rows(f1.real[:, :nin], s), _kron_rows(f1.imag[:, :nin], s))


def _rows_pair_kernel(u_ref, kc_ref, ar_ref, ai_ref, *, s_mm):
    _, nin, s_blk, c = u_ref.shape
    m = kc_ref.shape[0] // 2
    re, im = [], []
    for h in range(s_blk // s_mm):
        sub = slice(h * s_mm, (h + 1) * s_mm)
        p = _dot(kc_ref[...], u_ref[0, :, sub, :].reshape(nin * s_mm, c).astype(BF16))
        q = _dot(kc_ref[...], u_ref[1, :, sub, :].reshape(nin * s_mm, c).astype(BF16))
        re.append((p[:m] - q[m:]).reshape(m // s_mm, s_mm, c))
        im.append((q[:m] + p[m:]).reshape(m // s_mm, s_mm, c))
    ar_ref[...] = jnp.concatenate(re, axis=1).astype(BF16)
    ai_ref[...] = jnp.concatenate(im, axis=1).astype(BF16)


def _rows_pair(u, seq):
    b, rows, c = u.shape
    n1, f1, _ = _dft_tables(seq)
    nin = rows // DFT_N2
    s_mm, s_blk = _rows_block(nin)
    kcat = jnp.concatenate([_kron_rows(f1.real[:, :nin], s_mm), _kron_rows(f1.imag[:, :nin], s_mm)], axis=0)
    out_spec = pl.BlockSpec((None, n1, s_blk, c), lambda p, j: (p, 0, j, 0))
    out_shape = jax.ShapeDtypeStruct((b // 2, n1, DFT_N2, c), BF16)
    return pl.pallas_call(
        functools.partial(_rows_pair_kernel, s_mm=s_mm),
        grid=(b // 2, DFT_N2 // s_blk),
        in_specs=[pl.BlockSpec((2, nin, s_blk, c), lambda p, j: (p, 0, j, 0)),
                  pl.BlockSpec(kcat.shape, lambda p, j: (0, 0))],
        out_specs=[out_spec, out_spec],
        out_shape=[out_shape, out_shape],
        compiler_params=_cparams("parallel", "parallel"),
        name="hyena_dft_rows",
    )(u.reshape(b, nin, DFT_N2, c), kcat)


def _complex_apply(mcat, xr, xi):
    p = _dot(mcat, xr)
    q = _dot(mcat, xi)
    return p[:DFT_N2] - q[DFT_N2:], q[:DFT_N2] + p[DFT_N2:]


def _spectrum_kernel(ar_ref, ai_ref, g_ref, xr_ref, xi_ref, *, scale):
    for j in range(ar_ref.shape[0]):
        xr, xi = _complex_apply(g_ref[j], ar_ref[j].astype(BF16), ai_ref[j].astype(BF16))
        xr_ref[j] = xr * scale
        xi_ref[j] = xi * scale


def _slab_conv_kernel(ar_ref, ai_ref, kr_ref, ki_ref, g_ref, h_ref, dr_ref, di_ref):
    for j in range(ar_ref.shape[0]):
        xr, xi = _complex_apply(g_ref[j], ar_ref[j], ai_ref[j])
        kr = kr_ref[j]
        ki = ki_ref[j]
        zr = (xr * kr - xi * ki).astype(BF16)
        zi = (xr * ki + xi * kr).astype(BF16)
        dr, di = _complex_apply(h_ref[j], zr, zi)
        dr_ref[j] = dr.astype(BF16)
        di_ref[j] = di.astype(BF16)


def _slab_consts(seq):
    n1, _, g = _dft_tables(seq)
    gcat = jnp.asarray(np.concatenate([g.real, g.imag], axis=1), F32).astype(BF16)
    gt = np.conj(g).transpose(0, 2, 1)
    hcat = jnp.asarray(np.concatenate([gt.real, gt.imag], axis=1), F32).astype(BF16)
    return n1, gcat, hcat


def _spectrum(ar, ai, seq):
    n1, gcat, _ = _slab_consts(seq)
    c = ar.shape[-1]
    g = min(8, n1)
    slab = pl.BlockSpec((g, DFT_N2, c), lambda i: (i, 0, 0))
    return pl.pallas_call(
        functools.partial(_spectrum_kernel, scale=1.0 / (2 * seq)),
        grid=(n1 // g,),
        in_specs=[slab, slab, pl.BlockSpec((g, 2 * DFT_N2, DFT_N2), lambda i: (i, 0, 0))],
        out_specs=[slab, slab],
        out_shape=[jax.ShapeDtypeStruct((n1, DFT_N2, c), F32)] * 2,
        compiler_params=_cparams("parallel"),
        name="hyena_filter_spectrum",
    )(ar, ai, gcat)


def _slab_conv(ar, ai, kr, ki, order, seq):
    n1, gcat, hcat = _slab_consts(seq)
    npair, _, _, c = ar.shape
    g = min(8, n1)
    slab = pl.BlockSpec((None, g, DFT_N2, c), lambda p, i: (p, i, 0, 0))
    filt = pl.BlockSpec((g, DFT_N2, c), lambda p, i: (i, 0, order))
    mats = pl.BlockSpec((g, 2 * DFT_N2, DFT_N2), lambda p, i: (i, 0, 0))
    return pl.pallas_call(
        _slab_conv_kernel,
        grid=(npair, n1 // g),
        in_specs=[slab, slab, filt, filt, mats, mats],
        out_specs=[slab, slab],
        out_shape=[jax.ShapeDtypeStruct(ar.shape, BF16)] * 2,
        compiler_params=_cparams("parallel", "parallel"),
        name="hyena_dft_slabs",
    )(ar, ai, kr, ki, gcat, hcat)


def _rows_inverse_kernel(dr_ref, di_ref, kc_ref, u_ref, gate_ref, bias_ref, o_ref, *, s_mm):
    n1, s_blk, c = dr_ref.shape
    m = kc_ref.shape[0] // 2
    dr = dr_ref[...].astype(F32)
    di = di_ref[...].astype(F32)
    y0, y1 = [], []
    for h in range(s_blk // s_mm):
        sub = slice(h * s_mm, (h + 1) * s_mm)
        p = _dot(kc_ref[...], dr[:, sub, :].reshape(n1 * s_mm, c).astype(BF16))
        q = _dot(kc_ref[...], di[:, sub, :].reshape(n1 * s_mm, c).astype(BF16))
        y0.append((p[:m] + q[m:]).reshape(m // s_mm, s_mm, c))
        y1.append((q[:m] - p[m:]).reshape(m // s_mm, s_mm, c))
    for e, parts in enumerate((y0, y1)):
        y = jnp.concatenate(parts, axis=1)
        o_ref[e] = (gate_ref[e] * (y + u_ref[e] * bias_ref[...])).astype(o_ref.dtype)


def _rows_inverse(dr, di, u, gate, bias, seq, out_dtype):
    npair, n1, _, c = dr.shape
    _, f1, _ = _dft_tables(seq)
    nout = seq // DFT_N2
    s_mm, s_blk = _rows_block(nout)
    kcat = jnp.concatenate([_kron_rows(f1.real[:nout], s_mm), _kron_rows(f1.imag[:nout], s_mm)], axis=0)
    dspec = pl.BlockSpec((None, n1, s_blk, c), lambda p, j: (p, 0, j, 0))
    uspec = pl.BlockSpec((2, nout, s_blk, c), lambda p, j: (p, 0, j, 0))
    y = pl.pallas_call(
        functools.partial(_rows_inverse_kernel, s_mm=s_mm),
        grid=(npair, DFT_N2 // s_blk),
        in_specs=[dspec, dspec, pl.BlockSpec(kcat.shape, lambda p, j: (0, 0)),
                  uspec, uspec, pl.BlockSpec((1, 1, c), lambda p, j: (0, 0, 0))],
        out_specs=uspec,
        out_shape=jax.ShapeDtypeStruct((2 * npair, nout, DFT_N2, c), out_dtype),
        compiler_params=_cparams("parallel", "parallel"),
        name="hyena_idft_rows",
    )(dr, di, kcat, u.reshape(2 * npair, nout, DFT_N2, c), gate.reshape(2 * npair, nout, DFT_N2, c),
      bias.reshape(1, 1, c))
    return y.reshape(2 * npair, seq, c)


def _hyena(p, conv_w, conv_b, w1, b1, w2, b2, w3, freq, bias):
    b, seq, _ = p.shape
    assert b % 2 == 0, "batch elements are transformed in pairs"
    x1, x2, v = _shortconv(p, conv_w, conv_b.reshape(1, -1))
    kt = _hyena_filters(seq, w1, b1, w2, b2, w3, freq)
    kr, ki = _spectrum(*_rows_real(kt, seq), seq)
    dr, di = _slab_conv(*_rows_pair(v, seq), kr, ki, 0, seq)
    z = _rows_inverse(dr, di, v, x1, bias[0], seq, F32)
    dr, di = _slab_conv(*_rows_pair(z, seq), kr, ki, 1, seq)
    return _rows_inverse(dr, di, z, x2, bias[1], seq, BF16)


def _mix_ffn_kernel(x_ref, hy_ref, sg_ref, at_ref, wo_ref, g1_ref, n2_ref, sc_ref, sh_ref, g2_ref, w1_ref,
                    w2_ref, o_ref, *, chunk):
    mix = _dot(hy_ref[...], wo_ref[:HY_WIDTH])
    mix = mix + _dot(sg_ref[...], wo_ref[HY_WIDTH:HY_WIDTH + SG_WIDTH])
    mix = mix + _dot(at_ref[...], wo_ref[HY_WIDTH + SG_WIDTH:])
    x = x_ref[...] + g1_ref[...] * mix
    hb = _modulated_norm(x, n2_ref[...], sc_ref[...], sh_ref[...]).astype(BF16)
    hidden = w2_ref.shape[0]
    acc = jnp.zeros(x.shape, F32)
    for c in range(hidden // chunk):
        a = _dot(hb, w1_ref[:, c * chunk:(c + 1) * chunk])
        up = _dot(hb, w1_ref[:, hidden + c * chunk:hidden + (c + 1) * chunk])
        act = (a * _sigmoid(a) * up).astype(BF16)
        acc = acc + _dot(act, w2_ref[c * chunk:(c + 1) * chunk, :])
    o_ref[...] = x + g2_ref[...] * acc


def _mix_ffn(x2d, seq, layer, hy, sg, att, w_out, g1, n2, sc, sh, g2, w1, w2):
    m, d = x2d.shape
    tm = min(512, seq)
    per = seq // tm
    row = lambda i: (i, 0)
    fixed = lambda i: (0, 0)
    bat = lambda i: (i // per, 0, 0)
    resident = lambda a: pl.BlockSpec((None,) + a.shape[1:], lambda i: (layer, 0, 0), pipeline_mode=pl.Buffered(1))
    return pl.pallas_call(
        functools.partial(_mix_ffn_kernel, chunk=256),
        grid=(m // tm,),
        in_specs=[pl.BlockSpec((tm, d), row),
                  pl.BlockSpec((tm, HY_WIDTH), row),
                  pl.BlockSpec((tm, SG_WIDTH), row),
                  pl.BlockSpec((tm, DA_WIDTH), row),
                  resident(w_out),
                  pl.BlockSpec((None, 1, d), bat),
                  pl.BlockSpec((1, d), fixed),
                  pl.BlockSpec((None, 1, d), bat),
                  pl.BlockSpec((None, 1, d), bat),
                  pl.BlockSpec((None, 1, d), bat),
                  resident(w1),
                  resident(w2)],
        out_specs=pl.BlockSpec((tm, d), row),
        out_shape=jax.ShapeDtypeStruct((m, d), F32),
        compiler_params=_cparams("parallel"),
        name="outproj_ffn",
    )(x2d, hy, sg, att, w_out, g1, n2, sc, sh, g2, w1, w2)


def _rope_tables(seq):
    t = jnp.arange(seq, dtype=jnp.int32)
    half = DA_HEAD_DIM // 4
    inv_freq = ROPE_BASE ** (-jnp.arange(half, dtype=F32) / half)
    ang_row = (t // GRID_W).astype(F32)[:, None] * inv_freq[None, :]
    ang_col = (t % GRID_W).astype(F32)[:, None] * inv_freq[None, :]
    cos = jnp.concatenate([jnp.cos(ang_row)] * 2 + [jnp.cos(ang_col)] * 2, axis=-1)
    sin = jnp.concatenate([-jnp.sin(ang_row), jnp.sin(ang_row), -jnp.sin(ang_col), jnp.sin(ang_col)], axis=-1)
    return jnp.tile(cos, (1, 2)), jnp.tile(sin, (1, 2))


def kernel(x, c, ctx, c_ctx, norm1_g, norm2_g, ada_w, ada_b, w_in, hy_conv_w, hy_conv_b, hy_w1, hy_b1,
           hy_w2, hy_b2, hy_w3, hy_freq, hy_bias, sg_norm_g, sg_w, sg_b, qn_g, kn_g, lam_p, subln_g,
           w_out, ffn_w1, ffn_w2):
    batch, seq, d = x.shape
    ctx_len = ctx.shape[1]
    depth = w_in.shape[0]
    assert batch + 1 <= 8 and seq % CHUNK == 0 and ctx_len % CHUNK == 0

    cvecs = jnp.zeros((8, d), F32).at[:batch].set(c).at[batch].set(c_ctx)
    mods = _ada(cvecs, ada_w, ada_b[:, None, :])

    cos_l, sin_l = _rope_tables(seq)
    cos_c = jnp.ones((ctx_len, LANES), F32)
    sin_c = jnp.zeros((ctx_len, LANES), F32)
    seg_np = np.kron(np.eye(LANES // DA_HEAD_DIM), np.full((DA_HEAD_DIM, DA_HEAD_DIM), 1.0 / DA_HEAD_DIM))
    seg = jnp.asarray(np.concatenate([seg_np, seg_np], axis=0), F32).astype(BF16)

    w_in_b = w_in.astype(BF16)
    w_out_b = w_out.astype(BF16)
    w1_b = ffn_w1.astype(BF16)
    w2_b = ffn_w2.astype(BF16)
    xs = x.reshape(batch * seq, d)
    cs = ctx.reshape(batch * ctx_len, d)
    for i in range(depth):
        last = i == depth - 1
        lam_init = 0.8 - 0.6 * math.exp(-0.3 * i)
        mod_l = [m[:, None, :] for m in jnp.split(mods[i, :batch], 6, axis=-1)]
        mod_c = [jnp.broadcast_to(m[None], (batch, 1, d)) for m in jnp.split(mods[i, batch:batch + 1], 6, axis=-1)]
        qg = jnp.tile(qn_g[i], 2)[None, :]
        kg = jnp.tile(kn_g[i], 2)[None, :]
        sgw = sg_w[i].reshape(SG_HEADS * CHUNK, CHUNK).astype(BF16)
        sgb = jnp.repeat(sg_b[i].T, SG_WIDTH // SG_HEADS, axis=1)
        hy_params = (hy_conv_w[i], hy_conv_b[i], hy_w1[i], hy_b1[i], hy_w2[i], hy_b2[i], hy_w3[i],
                     hy_freq[i], hy_bias[i])

        def project(tokens, n_tok, mod, cos, sin):
            outs = _inproj(tokens, n_tok, i, norm1_g[i][None, :], mod[1], mod[0], w_in_b, cos, sin, qg, kg, seg,
                           sg_norm_g[i][None, :], sgw, sgb)
            return list(outs[:2]) + [a.reshape(batch, n_tok, DA_WIDTH) for a in outs[2:]]

        def finish(tokens, n_tok, mod, p_hy, sg, att):
            hy = _hyena(p_hy.reshape(batch, n_tok, HY_IN), *hy_params).reshape(batch * n_tok, HY_WIDTH)
            return _mix_ffn(tokens, n_tok, i, hy, sg, att.reshape(batch * n_tok, DA_WIDTH), w_out_b, mod[2],
                            norm2_g[i][None, :], mod[4], mod[3], mod[5], w1_b, w2_b)

        hy_l, sg_l, q_l, k_l, v_l = project(xs, seq, mod_l, cos_l, sin_l)
        hy_c, sg_c, q_c, k_c, v_c = project(cs, ctx_len, mod_c, cos_c, sin_c)
        attend = functools.partial(_attention, lam_p=lam_p[i], subln_g=subln_g[i][None, :], lam_init=lam_init,
                                   qn_g=qn_g[i], kn_g=kn_g[i])
        att_l = attend(q_l, [(k_l, v_l), (k_c, v_c)])
        xs = finish(xs, seq, mod_l, hy_l, sg_l, att_l)
        if not last:
            cs = finish(cs, ctx_len, mod_c, hy_c, sg_c, attend(q_c, [(k_c, v_c)]))
    return xs.reshape(batch, seq, d)
```

```python
import functools
import math

import numpy as np
import jax
import jax.numpy as jnp
from jax import lax
from jax.experimental import pallas as pl
from jax.experimental.pallas import tpu as pltpu

F32 = jnp.float32
BF16 = jnp.bfloat16

GRID_W = 64
EPS = 1e-6
HY_WIDTH = 256
HY_ORDER = 2
HY_POS_BANDS = 16
HY_DECAY_TARGET = 1e-2
HY_FAST_DECAY_PCT = 0.3
HY_SLOW_DECAY_PCT = 1.5
SG_HEADS = 4
SG_WIDTH = 256
CHUNK = 128
DA_HEADS = 4
DA_WIDTH = 512
DA_V_DIM = 128
DA_HEAD_DIM = 64
ROPE_BASE = 10000.0
HY_IN = 3 * HY_WIDTH
SG_IN = 2 * SG_WIDTH
LANES = 128
DFT_N2 = 128
SLABS_PER_STEP = 16
VMEM_LIMIT = 56 * 1024 * 1024
NEG_BIG = -1e30
Q_SCALE = math.log2(math.e) * DA_HEAD_DIM ** -0.5
INPROJ_TM = 1024
ATTN_TQ = 1024
ATTN_TK = 256


def _cparams(*sem):
    return pltpu.CompilerParams(dimension_semantics=sem, vmem_limit_bytes=VMEM_LIMIT)


def _dot(a, b):
    return jnp.dot(a, b, preferred_element_type=F32)


def _sigmoid(x):
    return 1.0 / (1.0 + jnp.exp(-x))


def _modulated_norm(x, g, sc, sh):
    y = x * lax.rsqrt(jnp.mean(x * x, axis=-1, keepdims=True) + EPS)
    return (y * g) * (1.0 + sc) + sh


def _ada_kernel(c_ref, w_ref, b_ref, o_ref):
    cv = c_ref[...]
    s = (cv * _sigmoid(cv)).astype(BF16)
    o_ref[...] = _dot(s, w_ref[...].astype(BF16)) + b_ref[...]


def _ada(cvecs, ada_w, ada_b):
    depth, d, n = ada_w.shape
    tn = 1536
    return pl.pallas_call(
        _ada_kernel,
        grid=(depth, n // tn),
        in_specs=[pl.BlockSpec((8, d), lambda i, j: (0, 0)),
                  pl.BlockSpec((None, d, tn), lambda i, j: (i, 0, j)),
                  pl.BlockSpec((None, 1, tn), lambda i, j: (i, 0, j))],
        out_specs=pl.BlockSpec((None, 8, tn), lambda i, j: (i, 0, j)),
        out_shape=jax.ShapeDtypeStruct((depth, 8, n), F32),
        compiler_params=_cparams("parallel", "parallel"),
        name="adaln",
    )(cvecs, ada_w, ada_b)


def _group_mean_sq(t, seg):
    sq = t * t
    hi = sq.astype(BF16)
    lo = (sq - hi.astype(F32)).astype(BF16)
    return _dot(jnp.concatenate([hi, lo], axis=1), seg)


def _norm_rope(t, gain, seg, cos, sin, swap_fwd):
    tn = t * lax.rsqrt(_group_mean_sq(t, seg) + EPS) * gain
    rot = jnp.where(swap_fwd, pltpu.roll(tn, LANES - 16, 1), pltpu.roll(tn, 16, 1))
    return tn * cos + rot * sin


def _inproj_kernel(x_ref, g_ref, sc_ref, sh_ref, w_ref, cos_ref, sin_ref, qg_ref, kg_ref, seg_ref,
                   sgg_ref, sgw_ref, sgb_ref, hy_ref, sg_ref, q_ref, k_ref, v_ref):
    tm = x_ref.shape[0]
    hb = _modulated_norm(x_ref[...], g_ref[...], sc_ref[...], sh_ref[...]).astype(BF16)

    hy_ref[...] = _dot(hb, w_ref[:, :HY_IN])

    psg = _dot(hb, w_ref[:, HY_IN:HY_IN + SG_IN])
    ge = 0.5 * psg * (1.0 + lax.erf(psg * np.float32(math.sqrt(0.5))))
    u = ge[:, :SG_WIDTH]
    vv = ge[:, SG_WIDTH:]
    vn = (vv * lax.rsqrt(jnp.mean(vv * vv, axis=-1, keepdims=True) + EPS) * sgg_ref[...]).astype(BF16)
    head_of_lane = lax.broadcasted_iota(jnp.int32, (CHUNK, SG_WIDTH), 1) // (SG_WIDTH // SG_HEADS)
    for c in range(tm // CHUNK):
        rows = slice(c * CHUNK, (c + 1) * CHUNK)
        r = _dot(sgw_ref[...], vn[rows])
        mixed = sgb_ref[...]
        for h in range(SG_HEADS):
            mixed = mixed + jnp.where(head_of_lane == h, r[h * CHUNK:(h + 1) * CHUNK], 0.0)
        sg_ref[rows, :] = (u[rows] * mixed).astype(BF16)

    off = HY_IN + SG_IN
    lane = lax.broadcasted_iota(jnp.int32, (1, LANES), 1)
    swap_fwd = (lane % 32) < 16
    cos = cos_ref[...]
    sin = sin_ref[...]
    seg = seg_ref[...]
    pq = _dot(hb, w_ref[:, off:off + DA_WIDTH])
    pk = _dot(hb, w_ref[:, off + DA_WIDTH:off + 2 * DA_WIDTH])
    v_ref[...] = _dot(hb, w_ref[:, off + 2 * DA_WIDTH:]).astype(BF16)
    for h in range(DA_HEADS):
        cols = slice(h * LANES, (h + 1) * LANES)
        q = _norm_rope(pq[:, cols], qg_ref[...], seg, cos, sin, swap_fwd)
        q_ref[:, cols] = (q * Q_SCALE).astype(BF16)
        k_ref[:, cols] = _norm_rope(pk[:, cols], kg_ref[...], seg, cos, sin, swap_fwd).astype(BF16)


def _inproj(x2d, seq, layer, g, sc, sh, w, cos, sin, qg, kg, seg, sgg, sgw, sgb):
    m, d = x2d.shape
    n = w.shape[2]
    tm = min(INPROJ_TM, seq)
    per = seq // tm
    row = lambda i: (i, 0)
    fixed = lambda i: (0, 0)
    bat = lambda i: (i // per, 0, 0)
    pos = lambda i: (i % per, 0)
    outs = [(HY_IN, F32), (SG_WIDTH, BF16), (DA_WIDTH, BF16), (DA_WIDTH, BF16), (DA_WIDTH, BF16)]
    return pl.pallas_call(
        _inproj_kernel,
        grid=(m // tm,),
        in_specs=[pl.BlockSpec((tm, d), row),
                  pl.BlockSpec((1, d), fixed),
                  pl.BlockSpec((None, 1, d), bat),
                  pl.BlockSpec((None, 1, d), bat),
                  pl.BlockSpec((None, d, n), lambda i: (layer, 0, 0)),
                  pl.BlockSpec((tm, LANES), pos),
                  pl.BlockSpec((tm, LANES), pos),
                  pl.BlockSpec((1, LANES), fixed),
                  pl.BlockSpec((1, LANES), fixed),
                  pl.BlockSpec((2 * LANES, LANES), fixed),
                  pl.BlockSpec((1, SG_WIDTH), fixed),
                  pl.BlockSpec((SG_HEADS * CHUNK, CHUNK), fixed),
                  pl.BlockSpec((CHUNK, SG_WIDTH), fixed)],
        out_specs=[pl.BlockSpec((tm, c), row) for c, _ in outs],
        out_shape=[jax.ShapeDtypeStruct((m, c), dt) for c, dt in outs],
        compiler_params=_cparams("parallel"),
        name="inproj",
    )(x2d, g, sc, sh, w, cos, sin, qg, kg, seg, sgg, sgw, sgb)


_NT = (((1,), (1,)), ((), ()))


def _lambda(lp_ref, lam_init):
    lp = lp_ref[...]
    return (jnp.exp(jnp.sum(lp[0:1] * lp[1:2], axis=-1, keepdims=True))
            - jnp.exp(jnp.sum(lp[2:3] * lp[3:4], axis=-1, keepdims=True)) + lam_init)


def _stacked_components(q):
    lane = lax.broadcasted_iota(jnp.int32, (1, LANES), 1)
    zero = jnp.zeros_like(q)
    return jnp.concatenate([jnp.where(lane < DA_HEAD_DIM, q, zero), jnp.where(lane >= DA_HEAD_DIM, q, zero)],
                           axis=0)


def _attn_finish(o1, o2, lam, g_ref, o_ref, lam_init):
    o = o1 - lam * o2
    o = o * lax.rsqrt(jnp.mean(o * o, axis=-1, keepdims=True) + EPS) * g_ref[...]
    o_ref[...] = (o * (1.0 - lam_init)).astype(BF16)


def _attn_bounded_kernel(lp_ref, g_ref, q_ref, *refs, tiles, lam_init):
    tq = q_ref.shape[0]
    o_ref = refs[-1]
    qs = _stacked_components(q_ref[...])
    acc = jnp.zeros((2 * tq, 2 * LANES), F32)
    for j, tk in enumerate(tiles):
        k_ref, v_ref = refs[2 * j], refs[2 * j + 1]
        ones = jnp.ones((tk, LANES), BF16)
        for i in range(k_ref.shape[0] // tk):
            kb = k_ref[i * tk:(i + 1) * tk, :]
            vb = jnp.concatenate([v_ref[i * tk:(i + 1) * tk, :], ones], axis=1)
            s = lax.dot_general(qs, kb, _NT, preferred_element_type=F32)
            acc = acc + _dot(jnp.exp2(s).astype(BF16), vb)
    o1 = acc[:tq, :LANES] / acc[:tq, LANES:]
    o2 = acc[tq:, :LANES] / acc[tq:, LANES:]
    _attn_finish(o1, o2, _lambda(lp_ref, lam_init), g_ref, o_ref, lam_init)


def _attn_online_kernel(lp_ref, g_ref, q_ref, *refs, tiles, lam_init):
    tq = q_ref.shape[0]
    o_ref = refs[-1]
    qs = _stacked_components(q_ref[...])
    carry = (jnp.full((2 * tq, 1), NEG_BIG, F32), jnp.zeros((2 * tq, 1), F32),
             jnp.zeros((2 * tq, DA_V_DIM), F32))
    for j, tk in enumerate(tiles):
        k_ref, v_ref = refs[2 * j], refs[2 * j + 1]

        def body(i, carry, k_ref=k_ref, v_ref=v_ref, tk=tk):
            m, l, acc = carry
            start = pl.multiple_of(i * tk, tk)
            kb = k_ref[pl.ds(start, tk), :]
            vb = v_ref[pl.ds(start, tk), :]
            s = lax.dot_general(qs, kb, _NT, preferred_element_type=F32)
            m_new = jnp.maximum(m, jnp.max(s, axis=-1, keepdims=True))
            alpha = jnp.exp2(m - m_new)
            p = jnp.exp2(s - m_new)
            l_new = alpha * l + jnp.sum(p, axis=-1, keepdims=True)
            return m_new, l_new, alpha * acc + _dot(p.astype(BF16), vb)

        carry = lax.fori_loop(0, k_ref.shape[0] // tk, body, carry)
    _, l, acc = carry
    o = acc / l
    _attn_finish(o[:tq], o[tq:], _lambda(lp_ref, lam_init), g_ref, o_ref, lam_init)


def _pick_tile(n, candidates):
    for c in candidates:
        if n % c == 0:
            return c
    raise ValueError(f"no tile for {n}")


BOUNDED_SCORE_LIMIT = 56.0


def _attention(q, kv, lam_p, subln_g, lam_init, qn_g, kn_g):
    bound = (math.sqrt(DA_HEAD_DIM) * math.log2(math.e) * 1.02) * jnp.max(jnp.abs(qn_g)) * jnp.max(jnp.abs(kn_g))
    flat = [a for pair in kv for a in pair]
    return lax.cond(bound <= BOUNDED_SCORE_LIMIT,
                    functools.partial(_attention_call, _attn_bounded_kernel, lam_init),
                    functools.partial(_attention_call, _attn_online_kernel, lam_init),
                    q, lam_p, subln_g, *flat)


def _attention_call(body, lam_init, q, lam_p, subln_g, *kv):
    b, lq, _ = q.shape
    tq = _pick_tile(lq, (ATTN_TQ, 512, 256, 128))
    tiles = tuple(_pick_tile(a.shape[1], (ATTN_TK, 512, 256, 128)) for a in kv[::2])
    whole = lambda a: pl.BlockSpec((None, a.shape[1], LANES), lambda bi, h, i: (bi, 0, h))
    return pl.pallas_call(
        functools.partial(body, tiles=tiles, lam_init=lam_init),
        grid=(b, DA_HEADS, lq // tq),
        in_specs=[pl.BlockSpec((4, DA_HEAD_DIM), lambda bi, h, i: (0, 0)),
                  pl.BlockSpec((1, DA_V_DIM), lambda bi, h, i: (0, 0)),
                  pl.BlockSpec((None, tq, LANES), lambda bi, h, i: (bi, i, h))] + [whole(a) for a in kv],
        out_specs=pl.BlockSpec((None, tq, LANES), lambda bi, h, i: (bi, i, h)),
        out_shape=jax.ShapeDtypeStruct((b, lq, DA_WIDTH), BF16),
        compiler_params=_cparams("parallel", "parallel", "parallel"),
        name="diff_attention",
    )(lam_p, subln_g, q, *kv)


def _shortconv_kernel(*refs, tile):
    rid = lax.broadcasted_iota(jnp.int32, (tile, 1), 0)
    zero_row = jnp.zeros((1, LANES), F32)
    for part in range(3):
        p_ref, w_ref, b_ref = refs[3 * part:3 * part + 3]
        o_ref = refs[9 + part]
        seq = p_ref.shape[0]
        w0 = w_ref[0:1, :]
        w1 = w_ref[1:2, :]
        w2 = w_ref[2:3, :]
        for t in range(seq // tile):
            r0 = t * tile
            cur = p_ref[r0:r0 + tile, :]
            before = zero_row if t == 0 else p_ref[r0 - 1:r0, :]
            after = zero_row if r0 + tile == seq else p_ref[r0 + tile:r0 + tile + 1, :]
            prev = jnp.where(rid == 0, before, pltpu.roll(cur, 1, 0))
            nxt = jnp.where(rid == tile - 1, after, pltpu.roll(cur, tile - 1, 0))
            o_ref[r0:r0 + tile, :] = prev * w0 + cur * w1 + nxt * w2 + b_ref[...]


def _shortconv(p, w, bias):
    b, seq, _ = p.shape
    tile = min(512, seq)
    nblk = HY_WIDTH // LANES
    in_specs = []
    for part in range(3):
        in_specs += [pl.BlockSpec((None, seq, LANES), lambda bi, j, part=part: (bi, 0, part * nblk + j)),
                     pl.BlockSpec((3, LANES), lambda bi, j, part=part: (0, part * nblk + j)),
                     pl.BlockSpec((1, LANES), lambda bi, j, part=part: (0, part * nblk + j))]
    out_spec = pl.BlockSpec((None, seq, LANES), lambda bi, j: (bi, 0, j))
    out_shape = jax.ShapeDtypeStruct((b, seq, HY_WIDTH), F32)
    return pl.pallas_call(
        functools.partial(_shortconv_kernel, tile=tile),
        grid=(b, nblk),
        in_specs=in_specs,
        out_specs=[out_spec] * 3,
        out_shape=[out_shape] * 3,
        compiler_params=_cparams("parallel", "parallel"),
        name="hyena_shortconv",
    )(*([p, w, bias] * 3))


def _filter_kernel(f_ref, w1_ref, b1_ref, w2_ref, b2_ref, w3_ref, fr_ref, dl_ref, o_ref, *, seq):
    tl = f_ref.shape[0]
    half = LANES // 2
    nout = HY_ORDER * HY_WIDTH
    feats = f_ref[...]
    h = jnp.sin(fr_ref[0:1, :] * (_dot(feats.astype(BF16), w1_ref[...]) + b1_ref[...]))
    h = jnp.sin(fr_ref[1:2, :] * (_dot(h.astype(BF16), w2_ref[...]) + b2_ref[...]))
    h = _dot(h.astype(BF16), w3_ref[...])
    row = pl.program_id(0) * tl + lax.broadcasted_iota(jnp.int32, (tl, 1), 0)
    o_ref[0] = h[:, :nout] * jnp.exp(-feats[:, 0:1] * dl_ref[...])
    bwd = h[:, nout:] * jnp.exp(-feats[:, half:half + 1] * dl_ref[...])
    o_ref[1] = jnp.where(row == 0, 0.0, bwd)


def _hyena_features(seq):
    t = jnp.linspace(0.0, 1.0, seq, dtype=F32)[:, None]
    bands = jnp.linspace(1e-4, HY_POS_BANDS - 1, HY_POS_BANDS, dtype=F32)
    ang = (2.0 * math.pi / seq) * jnp.arange(seq, dtype=F32)[:, None] * bands[None, :]
    feats = jnp.concatenate([t, jnp.cos(ang), -jnp.sin(ang)], axis=-1)
    back = jnp.concatenate([feats[:1], feats[:0:-1]], axis=0)
    pad = ((0, 0), (0, LANES // 2 - feats.shape[1]))
    return jnp.concatenate([jnp.pad(feats, pad), jnp.pad(back, pad)], axis=1)


def _hyena_filters(seq, w1, b1, w2, b2, w3, freq):
    feats = _hyena_features(seq)
    nfeat, hid = w1.shape
    half = LANES // 2
    nout = HY_ORDER * HY_WIDTH
    assert nfeat <= half and hid <= half
    w1p = jnp.zeros((LANES, LANES), F32).at[:nfeat, :hid].set(w1).at[half:half + nfeat, half:half + hid].set(w1)
    w2p = jnp.zeros((LANES, LANES), F32).at[:hid, :hid].set(w2).at[half:half + hid, half:half + hid].set(w2)
    w3d = w3.reshape(hid, HY_ORDER, 2, HY_WIDTH)
    w3p = (jnp.zeros((LANES, 2 * nout), F32)
           .at[:hid, :nout].set(w3d[:, :, 0].reshape(hid, nout))
           .at[half:half + hid, nout:].set(w3d[:, :, 1].reshape(hid, nout)))
    both = lambda a: jnp.tile(jnp.pad(a.reshape(-1, hid), ((0, 0), (0, half - hid))), (1, 2))
    min_decay = math.log(HY_DECAY_TARGET) / HY_SLOW_DECAY_PCT
    max_decay = math.log(HY_DECAY_TARGET) / HY_FAST_DECAY_PCT
    deltas = jnp.abs(jnp.linspace(min_decay, max_decay, HY_WIDTH, dtype=F32))[None, :]
    tl = min(512, seq)
    fixed = lambda i: (0, 0)
    kt = pl.pallas_call(
        functools.partial(_filter_kernel, seq=seq),
        grid=(seq // tl,),
        in_specs=[pl.BlockSpec((tl, LANES), lambda i: (i, 0)),
                  pl.BlockSpec((LANES, LANES), fixed), pl.BlockSpec((1, LANES), fixed),
                  pl.BlockSpec((LANES, LANES), fixed), pl.BlockSpec((1, LANES), fixed),
                  pl.BlockSpec((LANES, 2 * nout), fixed), pl.BlockSpec((2, LANES), fixed),
                  pl.BlockSpec((1, nout), fixed)],
        out_specs=pl.BlockSpec((2, tl, nout), lambda i: (0, i, 0)),
        out_shape=jax.ShapeDtypeStruct((2, seq, nout), F32),
        compiler_params=_cparams("parallel"),
        name="hyena_filter_mlp",
    )(feats, w1p.astype(BF16), both(b1), w2p.astype(BF16), both(b2), w3p.astype(BF16), both(freq),
      jnp.tile(deltas, (1, HY_ORDER)))
    return kt.reshape(2 * seq, nout)


def _dft_tables(seq):
    n = 2 * seq
    n1 = n // DFT_N2
    idx1 = np.arange(n1)
    idx2 = np.arange(DFT_N2)
    f1 = np.exp(-2j * np.pi * np.outer(idx1, idx1) / n1)
    k = idx1[:, None, None] + n1 * idx2[None, :, None]
    g = np.exp(-2j * np.pi * (k * idx2[None, None, :] % n) / n)
    return n1, f1, g


def _rows_block(nin):
    s_mm = max(8, LANES // nin)
    return s_mm, max(16, s_mm)


def _kron_rows(mat, s):
    return jnp.asarray(np.kron(mat, np.eye(s)), F32).astype(BF16)


def _rows_real_kernel(u_ref, kr_ref, ki_ref, ar_ref, ai_ref, *, s_mm):
    nin, s_blk, c = u_ref.shape
    n1 = ar_ref.shape[0]
    re, im = [], []
    for h in range(s_blk // s_mm):
        u = u_ref[:, h * s_mm:(h + 1) * s_mm, :].reshape(nin * s_mm, c).astype(BF16)
        re.append(_dot(kr_ref[...], u).reshape(n1, s_mm, c))
        im.append(_dot(ki_ref[...], u).reshape(n1, s_mm, c))
    ar_ref[...] = jnp.concatenate(re, axis=1).astype(BF16)
    ai_ref[...] = jnp.concatenate(im, axis=1).astype(BF16)


def _rows_real(u, seq):
    rows, c = u.shape
    n1, f1, _ = _dft_tables(seq)
    nin = rows // DFT_N2
    s_mm, s_blk = _rows_block(nin)
    fixed = lambda j: (0, 0)
    out_spec = pl.BlockSpec((n1, s_blk, c), lambda j: (0, j, 0))
    out_shape = jax.ShapeDtypeStruct((n1, DFT_N2, c), BF16)
    return pl.pallas_call(
        functools.partial(_rows_real_kernel, s_mm=s_mm),
        grid=(DFT_N2 // s_blk,),
        in_specs=[pl.BlockSpec((nin, s_blk, c), lambda j: (0, j, 0)),
                  pl.BlockSpec((n1 * s_mm, nin * s_mm), fixed),
                  pl.BlockSpec((n1 * s_mm, nin * s_mm), fixed)],
        out_specs=[out_spec, out_spec],
        out_shape=[out_shape, out_shape],
        compiler_params=_cparams("parallel"),
        name="hyena_filter_dft_rows",
    )(u.reshape(nin, DFT_N2, c), _kron_rows(f1.real[:, :nin], s_mm), _kron_rows(f1.imag[:, :nin], s_mm))


def _rows_pair_kernel(u_ref, kc_ref, ar_ref, ai_ref, *, s_mm):
    _, nin, s_blk, c = u_ref.shape
    m = kc_ref.shape[0] // 2
    re, im = [], []
    for h in range(s_blk // s_mm):
        sub = slice(h * s_mm, (h + 1) * s_mm)
        p = _dot(kc_ref[...], u_ref[0, :, sub, :].reshape(nin * s_mm, c).astype(BF16))
        q = _dot(kc_ref[...], u_ref[1, :, sub, :].reshape(nin * s_mm, c).astype(BF16))
        re.append((p[:m] - q[m:]).reshape(m // s_mm, s_mm, c))
        im.append((q[:m] + p[m:]).reshape(m // s_mm, s_mm, c))
    ar_ref[...] = jnp.concatenate(re, axis=1).astype(BF16)
    ai_ref[...] = jnp.concatenate(im, axis=1).astype(BF16)


def _rows_pair(u, seq):
    b, rows, c = u.shape
    n1, f1, _ = _dft_tables(seq)
    nin = rows // DFT_N2
    s_mm, s_blk = _rows_block(nin)
    kcat = jnp.concatenate([_kron_rows(f1.real[:, :nin], s_mm), _kron_rows(f1.imag[:, :nin], s_mm)], axis=0)
    out_spec = pl.BlockSpec((None, n1, s_blk, c), lambda p, j: (p, 0, j, 0))
    out_shape = jax.ShapeDtypeStruct((b // 2, n1, DFT_N2, c), BF16)
    return pl.pallas_call(
        functools.partial(_rows_pair_kernel, s_mm=s_mm),
        grid=(b // 2, DFT_N2 // s_blk),
        in_specs=[pl.BlockSpec((2, nin, s_blk, c), lambda p, j: (p, 0, j, 0)),
                  pl.BlockSpec(kcat.shape, lambda p, j: (0, 0))],
        out_specs=[out_spec, out_spec],
        out_shape=[out_shape, out_shape],
        compiler_params=_cparams("parallel", "parallel"),
        name="hyena_dft_rows",
    )(u.reshape(b, nin, DFT_N2, c), kcat)


def _complex_apply(mcat, xr, xi):
    p = _dot(mcat, xr)
    q = _dot(mcat, xi)
    return p[:DFT_N2] - q[DFT_N2:], q[:DFT_N2] + p[DFT_N2:]


def _spectrum_kernel(ar_ref, ai_ref, g_ref, xr_ref, xi_ref, *, scale):
    for j in range(ar_ref.shape[0]):
        xr, xi = _complex_apply(g_ref[j], ar_ref[j], ai_ref[j])
        xr_ref[j] = (xr * scale).astype(BF16)
        xi_ref[j] = (xi * scale).astype(BF16)


def _slab_conv_kernel(ar_ref, ai_ref, kr_ref, ki_ref, g_ref, h_ref, dr_ref, di_ref):
    for j in range(ar_ref.shape[0]):
        xr, xi = _complex_apply(g_ref[j], ar_ref[j], ai_ref[j])
        kr = kr_ref[j].astype(F32)
        ki = ki_ref[j].astype(F32)
        zr = (xr * kr - xi * ki).astype(BF16)
        zi = (xr * ki + xi * kr).astype(BF16)
        dr, di = _complex_apply(h_ref[j], zr, zi)
        dr_ref[j] = dr.astype(BF16)
        di_ref[j] = di.astype(BF16)


def _slab_consts(seq):
    n1, _, g = _dft_tables(seq)
    gcat = jnp.asarray(np.concatenate([g.real, g.imag], axis=1), F32).astype(BF16)
    gt = np.conj(g).transpose(0, 2, 1)
    hcat = jnp.asarray(np.concatenate([gt.real, gt.imag], axis=1), F32).astype(BF16)
    return n1, gcat, hcat


def _spectrum(ar, ai, seq):
    n1, gcat, _ = _slab_consts(seq)
    c = ar.shape[-1]
    g = min(SLABS_PER_STEP, n1)
    slab = pl.BlockSpec((g, DFT_N2, c), lambda i: (i, 0, 0))
    return pl.pallas_call(
        functools.partial(_spectrum_kernel, scale=1.0 / (2 * seq)),
        grid=(n1 // g,),
        in_specs=[slab, slab, pl.BlockSpec((g, 2 * DFT_N2, DFT_N2), lambda i: (i, 0, 0))],
        out_specs=[slab, slab],
        out_shape=[jax.ShapeDtypeStruct((n1, DFT_N2, c), BF16)] * 2,
        compiler_params=_cparams("parallel"),
        name="hyena_filter_spectrum",
    )(ar, ai, gcat)


def _slab_conv(ar, ai, kr, ki, order, seq):
    n1, gcat, hcat = _slab_consts(seq)
    npair, _, _, c = ar.shape
    g = min(SLABS_PER_STEP, n1)
    slab = pl.BlockSpec((None, g, DFT_N2, c), lambda p, i: (p, i, 0, 0))
    filt = pl.BlockSpec((g, DFT_N2, c), lambda p, i: (i, 0, order))
    mats = pl.BlockSpec((g, 2 * DFT_N2, DFT_N2), lambda p, i: (i, 0, 0))
    return pl.pallas_call(
        _slab_conv_kernel,
        grid=(npair, n1 // g),
        in_specs=[slab, slab, filt, filt, mats, mats],
        out_specs=[slab, slab],
        out_shape=[jax.ShapeDtypeStruct(ar.shape, BF16)] * 2,
        compiler_params=_cparams("parallel", "parallel"),
        name="hyena_dft_slabs",
    )(ar, ai, kr, ki, gcat, hcat)


def _rows_inverse_kernel(dr_ref, di_ref, kc_ref, u_ref, gate_ref, bias_ref, o_ref, *, s_mm):
    n1, s_blk, c = dr_ref.shape
    m = kc_ref.shape[0] // 2
    dr = dr_ref[...].astype(F32)
    di = di_ref[...].astype(F32)
    y0, y1 = [], []
    for h in range(s_blk // s_mm):
        sub = slice(h * s_mm, (h + 1) * s_mm)
        p = _dot(kc_ref[...], dr[:, sub, :].reshape(n1 * s_mm, c).astype(BF16))
        q = _dot(kc_ref[...], di[:, sub, :].reshape(n1 * s_mm, c).astype(BF16))
        y0.append((p[:m] + q[m:]).reshape(m // s_mm, s_mm, c))
        y1.append((q[:m] - p[m:]).reshape(m // s_mm, s_mm, c))
    for e, parts in enumerate((y0, y1)):
        y = jnp.concatenate(parts, axis=1)
        o_ref[e] = (gate_ref[e] * (y + u_ref[e] * bias_ref[...])).astype(o_ref.dtype)


def _rows_inverse(dr, di, u, gate, bias, seq, out_dtype):
    npair, n1, _, c = dr.shape
    _, f1, _ = _dft_tables(seq)
    nout = seq // DFT_N2
    s_mm, s_blk = _rows_block(nout)
    kcat = jnp.concatenate([_kron_rows(f1.real[:nout], s_mm), _kron_rows(f1.imag[:nout], s_mm)], axis=0)
    dspec = pl.BlockSpec((None, n1, s_blk, c), lambda p, j: (p, 0, j, 0))
    uspec = pl.BlockSpec((2, nout, s_blk, c), lambda p, j: (p, 0, j, 0))
    y = pl.pallas_call(
        functools.partial(_rows_inverse_kernel, s_mm=s_mm),
        grid=(npair, DFT_N2 // s_blk),
        in_specs=[dspec, dspec, pl.BlockSpec(kcat.shape, lambda p, j: (0, 0)),
                  uspec, uspec, pl.BlockSpec((1, 1, c), lambda p, j: (0, 0, 0))],
        out_specs=uspec,
        out_shape=jax.ShapeDtypeStruct((2 * npair, nout, DFT_N2, c), out_dtype),
        compiler_params=_cparams("parallel", "parallel"),
        name="hyena_idft_rows",
    )(dr, di, kcat, u.reshape(2 * npair, nout, DFT_N2, c), gate.reshape(2 * npair, nout, DFT_N2, c),
      bias.reshape(1, 1, c))
    return y.reshape(2 * npair, seq, c)


def _hyena(p, conv_w, conv_b, w1, b1, w2, b2, w3, freq, bias):
    b, seq, _ = p.shape
    assert b % 2 == 0, "batch elements are transformed in pairs"
    x1, x2, v = _shortconv(p, conv_w, conv_b.reshape(1, -1))
    kt = _hyena_filters(seq, w1, b1, w2, b2, w3, freq)
    kr, ki = _spectrum(*_rows_real(kt, seq), seq)
    dr, di = _slab_conv(*_rows_pair(v, seq), kr, ki, 0, seq)
    z = _rows_inverse(dr, di, v, x1, bias[0], seq, F32)
    dr, di = _slab_conv(*_rows_pair(z, seq), kr, ki, 1, seq)
    return _rows_inverse(dr, di, z, x2, bias[1], seq, BF16)


def _mix_ffn_kernel(x_ref, hy_ref, sg_ref, at_ref, wo_ref, g1_ref, n2_ref, sc_ref, sh_ref, g2_ref, w1_ref,
                    w2_ref, o_ref, *, chunk):
    mix = _dot(hy_ref[...], wo_ref[:HY_WIDTH])
    mix = mix + _dot(sg_ref[...], wo_ref[HY_WIDTH:HY_WIDTH + SG_WIDTH])
    mix = mix + _dot(at_ref[...], wo_ref[HY_WIDTH + SG_WIDTH:])
    x = x_ref[...] + g1_ref[...] * mix
    hb = _modulated_norm(x, n2_ref[...], sc_ref[...], sh_ref[...]).astype(BF16)
    hidden = w2_ref.shape[0]
    acc = jnp.zeros(x.shape, F32)
    for c in range(hidden // chunk):
        a = _dot(hb, w1_ref[:, c * chunk:(c + 1) * chunk])
        up = _dot(hb, w1_ref[:, hidden + c * chunk:hidden + (c + 1) * chunk])
        act = (a * _sigmoid(a) * up).astype(BF16)
        acc = acc + _dot(act, w2_ref[c * chunk:(c + 1) * chunk, :])
    o_ref[...] = x + g2_ref[...] * acc


def _mix_ffn(x2d, seq, layer, hy, sg, att, w_out, g1, n2, sc, sh, g2, w1, w2):
    m, d = x2d.shape
    tm = min(512, seq)
    per = seq // tm
    row = lambda i: (i, 0)
    fixed = lambda i: (0, 0)
    bat = lambda i: (i // per, 0, 0)
    resident = lambda a: pl.BlockSpec((None,) + a.shape[1:], lambda i: (layer, 0, 0), pipeline_mode=pl.Buffered(1))
    return pl.pallas_call(
        functools.partial(_mix_ffn_kernel, chunk=256),
        grid=(m // tm,),
        in_specs=[pl.BlockSpec((tm, d), row),
                  pl.BlockSpec((tm, HY_WIDTH), row),
                  pl.BlockSpec((tm, SG_WIDTH), row),
                  pl.BlockSpec((tm, DA_WIDTH), row),
                  resident(w_out),
                  pl.BlockSpec((None, 1, d), bat),
                  pl.BlockSpec((1, d), fixed),
                  pl.BlockSpec((None, 1, d), bat),
                  pl.BlockSpec((None, 1, d), bat),
                  pl.BlockSpec((None, 1, d), bat),
                  resident(w1),
                  resident(w2)],
        out_specs=pl.BlockSpec((tm, d), row),
        out_shape=jax.ShapeDtypeStruct((m, d), F32),
        compiler_params=_cparams("parallel"),
        name="outproj_ffn",
    )(x2d, hy, sg, att, w_out, g1, n2, sc, sh, g2, w1, w2)


def _rope_tables(seq):
    t = jnp.arange(seq, dtype=jnp.int32)
    half = DA_HEAD_DIM // 4
    inv_freq = ROPE_BASE ** (-jnp.arange(half, dtype=F32) / half)
    ang_row = (t // GRID_W).astype(F32)[:, None] * inv_freq[None, :]
    ang_col = (t % GRID_W).astype(F32)[:, None] * inv_freq[None, :]
    cos = jnp.concatenate([jnp.cos(ang_row)] * 2 + [jnp.cos(ang_col)] * 2, axis=-1)
    sin = jnp.concatenate([-jnp.sin(ang_row), jnp.sin(ang_row), -jnp.sin(ang_col), jnp.sin(ang_col)], axis=-1)
    return jnp.tile(cos, (1, 2)), jnp.tile(sin, (1, 2))


def kernel(x, c, ctx, c_ctx, norm1_g, norm2_g, ada_w, ada_b, w_in, hy_conv_w, hy_conv_b, hy_w1, hy_b1,
           hy_w2, hy_b2, hy_w3, hy_freq, hy_bias, sg_norm_g, sg_w, sg_b, qn_g, kn_g, lam_p, subln_g,
           w_out, ffn_w1, ffn_w2):
    batch, seq, d = x.shape
    ctx_len = ctx.shape[1]
    depth = w_in.shape[0]
    assert batch + 1 <= 8 and seq % CHUNK == 0 and ctx_len % CHUNK == 0

    cvecs = jnp.zeros((8, d), F32).at[:batch].set(c).at[batch].set(c_ctx)
    mods = _ada(cvecs, ada_w, ada_b[:, None, :])

    cos_l, sin_l = _rope_tables(seq)
    cos_c = jnp.ones((ctx_len, LANES), F32)
    sin_c = jnp.zeros((ctx_len, LANES), F32)
    seg_np = np.kron(np.eye(LANES // DA_HEAD_DIM), np.full((DA_HEAD_DIM, DA_HEAD_DIM), 1.0 / DA_HEAD_DIM))
    seg = jnp.asarray(np.concatenate([seg_np, seg_np], axis=0), F32).astype(BF16)

    w_in_b = w_in.astype(BF16)
    w_out_b = w_out.astype(BF16)
    w1_b = ffn_w1.astype(BF16)
    w2_b = ffn_w2.astype(BF16)
    xs = x.reshape(batch * seq, d)
    cs = ctx.reshape(batch * ctx_len, d)
    for i in range(depth):
        last = i == depth - 1
        lam_init = 0.8 - 0.6 * math.exp(-0.3 * i)
        mod_l = [m[:, None, :] for m in jnp.split(mods[i, :batch], 6, axis=-1)]
        mod_c = [jnp.broadcast_to(m[None], (batch, 1, d)) for m in jnp.split(mods[i, batch:batch + 1], 6, axis=-1)]
        qg = jnp.tile(qn_g[i], 2)[None, :]
        kg = jnp.tile(kn_g[i], 2)[None, :]
        sgw = sg_w[i].reshape(SG_HEADS * CHUNK, CHUNK).astype(BF16)
        sgb = jnp.repeat(sg_b[i].T, SG_WIDTH // SG_HEADS, axis=1)
        hy_params = (hy_conv_w[i], hy_conv_b[i], hy_w1[i], hy_b1[i], hy_w2[i], hy_b2[i], hy_w3[i],
                     hy_freq[i], hy_bias[i])

        def project(tokens, n_tok, mod, cos, sin):
            outs = _inproj(tokens, n_tok, i, norm1_g[i][None, :], mod[1], mod[0], w_in_b, cos, sin, qg, kg, seg,
                           sg_norm_g[i][None, :], sgw, sgb)
            return list(outs[:2]) + [a.reshape(batch, n_tok, DA_WIDTH) for a in outs[2:]]

        def finish(tokens, n_tok, mod, p_hy, sg, att):
            hy = _hyena(p_hy.reshape(batch, n_tok, HY_IN), *hy_params).reshape(batch * n_tok, HY_WIDTH)
            return _mix_ffn(tokens, n_tok, i, hy, sg, att.reshape(batch * n_tok, DA_WIDTH), w_out_b, mod[2],
                            norm2_g[i][None, :], mod[4], mod[3], mod[5], w1_b, w2_b)

        hy_l, sg_l, q_l, k_l, v_l = project(xs, seq, mod_l, cos_l, sin_l)
        hy_c, sg_c, q_c, k_c, v_c = project(cs, ctx_len, mod_c, cos_c, sin_c)
        attend = functools.partial(_attention, lam_p=lam_p[i], subln_g=subln_g[i][None, :], lam_init=lam_init,
                                   qn_g=qn_g[i], kn_g=kn_g[i])
        att_l = attend(q_l, [(k_l, v_l), (k_c, v_c)])
        xs = finish(xs, seq, mod_l, hy_l, sg_l, att_l)
        if not last:
            cs = finish(cs, ctx_len, mod_c, hy_c, sg_c, attend(q_c, [(k_c, v_c)]))
    return xs.reshape(batch, seq, d)
```

```python
import functools
import math

import numpy as np
import jax
import jax.numpy as jnp
from jax import lax
from jax.experimental import pallas as pl
from jax.experimental.pallas import tpu as pltpu

F32 = jnp.float32
BF16 = jnp.bfloat16

GRID_W = 64
EPS = 1e-6
HY_WIDTH = 256
HY_ORDER = 2
HY_POS_BANDS = 16
HY_DECAY_TARGET = 1e-2
HY_FAST_DECAY_PCT = 0.3
HY_SLOW_DECAY_PCT = 1.5
SG_HEADS = 4
SG_WIDTH = 256
CHUNK = 128
DA_HEADS = 4
DA_WIDTH = 512
DA_V_DIM = 128
DA_HEAD_DIM = 64
ROPE_BASE = 10000.0
HY_IN = 3 * HY_WIDTH
SG_IN = 2 * SG_WIDTH
LANES = 128
DFT_N2 = 128
SLABS_PER_STEP = 16
VMEM_LIMIT = 56 * 1024 * 1024
NEG_BIG = -1e30
Q_SCALE = math.log2(math.e) * DA_HEAD_DIM ** -0.5
INPROJ_TM = 1024
FFN_TM = 512
FFN_CHUNK = 256
ATTN_TQ = 512
ATTN_TK = 512


def _cparams(*sem):
    return pltpu.CompilerParams(dimension_semantics=sem, vmem_limit_bytes=VMEM_LIMIT)


def _dot(a, b):
    return jnp.dot(a, b, preferred_element_type=F32)


def _sigmoid(x):
    return 1.0 / (1.0 + jnp.exp(-x))


def _modulated_norm(x, g, sc, sh):
    y = x * lax.rsqrt(jnp.mean(x * x, axis=-1, keepdims=True) + EPS)
    return (y * g) * (1.0 + sc) + sh


def _ada_kernel(c_ref, w_ref, b_ref, o_ref):
    cv = c_ref[...]
    s = (cv * _sigmoid(cv)).astype(BF16)
    o_ref[...] = _dot(s, w_ref[...].astype(BF16)) + b_ref[...]


def _ada(cvecs, ada_w, ada_b):
    depth, d, n = ada_w.shape
    tn = 1536
    return pl.pallas_call(
        _ada_kernel,
        grid=(depth, n // tn),
        in_specs=[pl.BlockSpec((8, d), lambda i, j: (0, 0)),
                  pl.BlockSpec((None, d, tn), lambda i, j: (i, 0, j)),
                  pl.BlockSpec((None, 1, tn), lambda i, j: (i, 0, j))],
        out_specs=pl.BlockSpec((None, 8, tn), lambda i, j: (i, 0, j)),
        out_shape=jax.ShapeDtypeStruct((depth, 8, n), F32),
        compiler_params=_cparams("parallel", "parallel"),
        name="adaln",
    )(cvecs, ada_w, ada_b)


def _group_mean_sq(t, seg):
    sq = t * t
    hi = sq.astype(BF16)
    lo = (sq - hi.astype(F32)).astype(BF16)
    return _dot(jnp.concatenate([hi, lo], axis=1), seg)


def _norm_rope(t, gain, seg, cos, sin, swap_fwd):
    tn = t * lax.rsqrt(_group_mean_sq(t, seg) + EPS) * gain
    rot = jnp.where(swap_fwd, pltpu.roll(tn, LANES - 16, 1), pltpu.roll(tn, 16, 1))
    return tn * cos + rot * sin


def _inproj_kernel(x_ref, g_ref, sc_ref, sh_ref, w_ref, cos_ref, sin_ref, qg_ref, kg_ref, seg_ref,
                   sgg_ref, sgw_ref, sgb_ref, hy_ref, sg_ref, q_ref, k_ref, v_ref):
    tm = x_ref.shape[0]
    hb = _modulated_norm(x_ref[...], g_ref[...], sc_ref[...], sh_ref[...]).astype(BF16)

    hy_ref[...] = _dot(hb, w_ref[:, :HY_IN])

    psg = _dot(hb, w_ref[:, HY_IN:HY_IN + SG_IN])
    ge = 0.5 * psg * (1.0 + lax.erf(psg * np.float32(math.sqrt(0.5))))
    u = ge[:, :SG_WIDTH]
    vv = ge[:, SG_WIDTH:]
    vn = (vv * lax.rsqrt(jnp.mean(vv * vv, axis=-1, keepdims=True) + EPS) * sgg_ref[...]).astype(BF16)
    head_of_lane = lax.broadcasted_iota(jnp.int32, (CHUNK, SG_WIDTH), 1) // (SG_WIDTH // SG_HEADS)
    for c in range(tm // CHUNK):
        rows = slice(c * CHUNK, (c + 1) * CHUNK)
        r = _dot(sgw_ref[...], vn[rows])
        mixed = sgb_ref[...]
        for h in range(SG_HEADS):
            mixed = mixed + jnp.where(head_of_lane == h, r[h * CHUNK:(h + 1) * CHUNK], 0.0)
        sg_ref[rows, :] = (u[rows] * mixed).astype(BF16)

    off = HY_IN + SG_IN
    lane = lax.broadcasted_iota(jnp.int32, (1, LANES), 1)
    swap_fwd = (lane % 32) < 16
    cos = cos_ref[...]
    sin = sin_ref[...]
    seg = seg_ref[...]
    pq = _dot(hb, w_ref[:, off:off + DA_WIDTH])
    pk = _dot(hb, w_ref[:, off + DA_WIDTH:off + 2 * DA_WIDTH])
    v_ref[...] = _dot(hb, w_ref[:, off + 2 * DA_WIDTH:]).astype(BF16)
    for h in range(DA_HEADS):
        cols = slice(h * LANES, (h + 1) * LANES)
        q = _norm_rope(pq[:, cols], qg_ref[...], seg, cos, sin, swap_fwd)
        q_ref[:, cols] = (q * Q_SCALE).astype(BF16)
        k_ref[:, cols] = _norm_rope(pk[:, cols], kg_ref[...], seg, cos, sin, swap_fwd).astype(BF16)


def _inproj(x2d, seq, layer, g, sc, sh, w, cos, sin, qg, kg, seg, sgg, sgw, sgb):
    m, d = x2d.shape
    n = w.shape[2]
    tm = min(INPROJ_TM, seq)
    per = seq // tm
    row = lambda i: (i, 0)
    fixed = lambda i: (0, 0)
    bat = lambda i: (i // per, 0, 0)
    pos = lambda i: (i % per, 0)
    outs = [(HY_IN, F32), (SG_WIDTH, BF16), (DA_WIDTH, BF16), (DA_WIDTH, BF16), (DA_WIDTH, BF16)]
    return pl.pallas_call(
        _inproj_kernel,
        grid=(m // tm,),
        in_specs=[pl.BlockSpec((tm, d), row),
                  pl.BlockSpec((1, d), fixed),
                  pl.BlockSpec((None, 1, d), bat),
                  pl.BlockSpec((None, 1, d), bat),
                  pl.BlockSpec((None, d, n), lambda i: (layer, 0, 0)),
                  pl.BlockSpec((tm, LANES), pos),
                  pl.BlockSpec((tm, LANES), pos),
                  pl.BlockSpec((1, LANES), fixed),
                  pl.BlockSpec((1, LANES), fixed),
                  pl.BlockSpec((2 * LANES, LANES), fixed),
                  pl.BlockSpec((1, SG_WIDTH), fixed),
                  pl.BlockSpec((SG_HEADS * CHUNK, CHUNK), fixed),
                  pl.BlockSpec((CHUNK, SG_WIDTH), fixed)],
        out_specs=[pl.BlockSpec((tm, c), row) for c, _ in outs],
        out_shape=[jax.ShapeDtypeStruct((m, c), dt) for c, dt in outs],
        compiler_params=_cparams("parallel"),
        name="inproj",
    )(x2d, g, sc, sh, w, cos, sin, qg, kg, seg, sgg, sgw, sgb)


_NT = (((1,), (1,)), ((), ()))


def _lambda(lp_ref, lam_init):
    lp = lp_ref[...]
    return (jnp.exp(jnp.sum(lp[0:1] * lp[1:2], axis=-1, keepdims=True))
            - jnp.exp(jnp.sum(lp[2:3] * lp[3:4], axis=-1, keepdims=True)) + lam_init)


def _stacked_components(q):
    lane = lax.broadcasted_iota(jnp.int32, (1, LANES), 1)
    zero = jnp.zeros_like(q)
    return jnp.concatenate([jnp.where(lane < DA_HEAD_DIM, q, zero), jnp.where(lane >= DA_HEAD_DIM, q, zero)],
                           axis=0)


def _attn_finish(o, g_ref, o_ref, lam_init):
    o = o * lax.rsqrt(jnp.mean(o * o, axis=-1, keepdims=True) + EPS) * g_ref[...]
    o_ref[...] = (o * (1.0 - lam_init)).astype(BF16)


def _attn_bounded_kernel(lp_ref, g_ref, q_ref, *refs, tiles, lam_init):
    tq = q_ref.shape[0]
    o_ref, p_ref = refs[-2], refs[-1]
    qs = _stacked_components(q_ref[...])
    lsum = jnp.zeros((2 * tq, LANES), F32)
    col = 0
    for j, tk in enumerate(tiles):
        k_ref = refs[2 * j]
        for i in range(k_ref.shape[0] // tk):
            s = lax.dot_general(qs, k_ref[i * tk:(i + 1) * tk, :], _NT, preferred_element_type=F32)
            p = jnp.exp2(s)
            for c in range(tk // LANES):
                lsum = lsum + p[:, c * LANES:(c + 1) * LANES]
            p_ref[:, col:col + tk] = p.astype(BF16)
            col += tk
    l = jnp.sum(lsum, axis=-1, keepdims=True)
    ratio = _lambda(lp_ref, lam_init) * l[:tq] / l[tq:]
    half = tq // 2
    acc = [jnp.zeros((half, DA_V_DIM), F32), jnp.zeros((half, DA_V_DIM), F32)]
    col = 0
    for j, tk in enumerate(tiles):
        v_ref = refs[2 * j + 1]
        for i in range(v_ref.shape[0] // tk):
            vb = v_ref[i * tk:(i + 1) * tk, :]
            for r in range(2):
                rows = slice(r * half, (r + 1) * half)
                p1 = p_ref[r * half:(r + 1) * half, col:col + tk]
                p2 = p_ref[tq + r * half:tq + (r + 1) * half, col:col + tk]
                w = p1 - (ratio[rows] * p2.astype(F32)).astype(BF16)
                acc[r] = acc[r] + _dot(w, vb)
            col += tk
    _attn_finish(jnp.concatenate(acc, axis=0) / l[:tq], g_ref, o_ref, lam_init)


def _attn_online_kernel(lp_ref, g_ref, q_ref, *refs, tiles, lam_init):
    tq = q_ref.shape[0]
    o_ref = refs[-1]
    qs = _stacked_components(q_ref[...])
    carry = (jnp.full((2 * tq, 1), NEG_BIG, F32), jnp.zeros((2 * tq, 1), F32),
             jnp.zeros((2 * tq, DA_V_DIM), F32))
    for j, tk in enumerate(tiles):
        k_ref, v_ref = refs[2 * j], refs[2 * j + 1]

        def body(i, carry, k_ref=k_ref, v_ref=v_ref, tk=tk):
            m, l, acc = carry
            start = pl.multiple_of(i * tk, tk)
            kb = k_ref[pl.ds(start, tk), :]
            vb = v_ref[pl.ds(start, tk), :]
            s = lax.dot_general(qs, kb, _NT, preferred_element_type=F32)
            m_new = jnp.maximum(m, jnp.max(s, axis=-1, keepdims=True))
            alpha = jnp.exp2(m - m_new)
            p = jnp.exp2(s - m_new)
            l_new = alpha * l + jnp.sum(p, axis=-1, keepdims=True)
            return m_new, l_new, alpha * acc + _dot(p.astype(BF16), vb)

        carry = lax.fori_loop(0, k_ref.shape[0] // tk, body, carry)
    _, l, acc = carry
    o = acc / l
    _attn_finish(o[:tq] - _lambda(lp_ref, lam_init) * o[tq:], g_ref, o_ref, lam_init)


def _pick_tile(n, candidates):
    for c in candidates:
        if n % c == 0:
            return c
    raise ValueError(f"no tile for {n}")


BOUNDED_SCORE_LIMIT = 56.0


def _attention(q, kv, lam_p, subln_g, lam_init, qn_g, kn_g):
    bound = (math.sqrt(DA_HEAD_DIM) * math.log2(math.e) * 1.02) * jnp.max(jnp.abs(qn_g)) * jnp.max(jnp.abs(kn_g))
    flat = [a for pair in kv for a in pair]
    return lax.cond(bound <= BOUNDED_SCORE_LIMIT,
                    functools.partial(_attention_call, _attn_bounded_kernel, lam_init),
                    functools.partial(_attention_call, _attn_online_kernel, lam_init),
                    q, lam_p, subln_g, *flat)


def _attention_call(body, lam_init, q, lam_p, subln_g, *kv):
    b, lq, _ = q.shape
    tq = _pick_tile(lq, (ATTN_TQ, 512, 256, 128))
    tiles = tuple(_pick_tile(a.shape[1], (ATTN_TK, 512, 256, 128)) for a in kv[::2])
    whole = lambda a: pl.BlockSpec((None, a.shape[1], LANES), lambda bi, h, i: (bi, 0, h))
    keys = sum(a.shape[1] for a in kv[::2])
    scratch = [pltpu.VMEM((2 * tq, keys), BF16)] if body is _attn_bounded_kernel else []
    return pl.pallas_call(
        functools.partial(body, tiles=tiles, lam_init=lam_init),
        grid=(b, DA_HEADS, lq // tq),
        scratch_shapes=scratch,
        in_specs=[pl.BlockSpec((4, DA_HEAD_DIM), lambda bi, h, i: (0, 0)),
                  pl.BlockSpec((1, DA_V_DIM), lambda bi, h, i: (0, 0)),
                  pl.BlockSpec((None, tq, LANES), lambda bi, h, i: (bi, i, h))] + [whole(a) for a in kv],
        out_specs=pl.BlockSpec((None, tq, LANES), lambda bi, h, i: (bi, i, h)),
        out_shape=jax.ShapeDtypeStruct((b, lq, DA_WIDTH), BF16),
        compiler_params=_cparams("parallel", "parallel", "parallel"),
        name="diff_attention",
    )(lam_p, subln_g, q, *kv)


def _shortconv_kernel(*refs, tile):
    rid = lax.broadcasted_iota(jnp.int32, (tile, 1), 0)
    zero_row = jnp.zeros((1, LANES), F32)
    for part in range(3):
        p_ref, w_ref, b_ref = refs[3 * part:3 * part + 3]
        o_ref = refs[9 + part]
        seq = p_ref.shape[0]
        w0 = w_ref[0:1, :]
        w1 = w_ref[1:2, :]
        w2 = w_ref[2:3, :]
        for t in range(seq // tile):
            r0 = t * tile
            cur = p_ref[r0:r0 + tile, :]
            before = zero_row if t == 0 else p_ref[r0 - 1:r0, :]
            after = zero_row if r0 + tile == seq else p_ref[r0 + tile:r0 + tile + 1, :]
            prev = jnp.where(rid == 0, before, pltpu.roll(cur, 1, 0))
            nxt = jnp.where(rid == tile - 1, after, pltpu.roll(cur, tile - 1, 0))
            o_ref[r0:r0 + tile, :] = prev * w0 + cur * w1 + nxt * w2 + b_ref[...]


def _shortconv(p, w, bias):
    b, seq, _ = p.shape
    tile = min(512, seq)
    nblk = HY_WIDTH // LANES
    in_specs = []
    for part in range(3):
        in_specs += [pl.BlockSpec((None, seq, LANES), lambda bi, j, part=part: (bi, 0, part * nblk + j)),
                     pl.BlockSpec((3, LANES), lambda bi, j, part=part: (0, part * nblk + j)),
                     pl.BlockSpec((1, LANES), lambda bi, j, part=part: (0, part * nblk + j))]
    out_spec = pl.BlockSpec((None, seq, LANES), lambda bi, j: (bi, 0, j))
    out_shape = jax.ShapeDtypeStruct((b, seq, HY_WIDTH), F32)
    return pl.pallas_call(
        functools.partial(_shortconv_kernel, tile=tile),
        grid=(b, nblk),
        in_specs=in_specs,
        out_specs=[out_spec] * 3,
        out_shape=[out_shape] * 3,
        compiler_params=_cparams("parallel", "parallel"),
        name="hyena_shortconv",
    )(*([p, w, bias] * 3))


def _filter_kernel(f_ref, w1_ref, b1_ref, w2_ref, b2_ref, w3_ref, fr_ref, dl_ref, o_ref, *, seq):
    tl = f_ref.shape[0]
    half = LANES // 2
    nout = HY_ORDER * HY_WIDTH
    feats = f_ref[...]
    h = jnp.sin(fr_ref[0:1, :] * (_dot(feats.astype(BF16), w1_ref[...]) + b1_ref[...]))
    h = jnp.sin(fr_ref[1:2, :] * (_dot(h.astype(BF16), w2_ref[...]) + b2_ref[...]))
    h = _dot(h.astype(BF16), w3_ref[...])
    row = pl.program_id(0) * tl + lax.broadcasted_iota(jnp.int32, (tl, 1), 0)
    o_ref[0] = h[:, :nout] * jnp.exp(-feats[:, 0:1] * dl_ref[...])
    bwd = h[:, nout:] * jnp.exp(-feats[:, half:half + 1] * dl_ref[...])
    o_ref[1] = jnp.where(row == 0, 0.0, bwd)


def _hyena_features(seq):
    t = jnp.linspace(0.0, 1.0, seq, dtype=F32)[:, None]
    bands = jnp.linspace(1e-4, HY_POS_BANDS - 1, HY_POS_BANDS, dtype=F32)
    ang = (2.0 * math.pi / seq) * jnp.arange(seq, dtype=F32)[:, None] * bands[None, :]
    feats = jnp.concatenate([t, jnp.cos(ang), -jnp.sin(ang)], axis=-1)
    back = jnp.concatenate([feats[:1], feats[:0:-1]], axis=0)
    pad = ((0, 0), (0, LANES // 2 - feats.shape[1]))
    return jnp.concatenate([jnp.pad(feats, pad), jnp.pad(back, pad)], axis=1)


def _hyena_filters(seq, w1, b1, w2, b2, w3, freq):
    feats = _hyena_features(seq)
    nfeat, hid = w1.shape
    half = LANES // 2
    nout = HY_ORDER * HY_WIDTH
    assert nfeat <= half and hid <= half
    w1p = jnp.zeros((LANES, LANES), F32).at[:nfeat, :hid].set(w1).at[half:half + nfeat, half:half + hid].set(w1)
    w2p = jnp.zeros((LANES, LANES), F32).at[:hid, :hid].set(w2).at[half:half + hid, half:half + hid].set(w2)
    w3d = w3.reshape(hid, HY_ORDER, 2, HY_WIDTH)
    w3p = (jnp.zeros((LANES, 2 * nout), F32)
           .at[:hid, :nout].set(w3d[:, :, 0].reshape(hid, nout))
           .at[half:half + hid, nout:].set(w3d[:, :, 1].reshape(hid, nout)))
    both = lambda a: jnp.tile(jnp.pad(a.reshape(-1, hid), ((0, 0), (0, half - hid))), (1, 2))
    min_decay = math.log(HY_DECAY_TARGET) / HY_SLOW_DECAY_PCT
    max_decay = math.log(HY_DECAY_TARGET) / HY_FAST_DECAY_PCT
    deltas = jnp.abs(jnp.linspace(min_decay, max_decay, HY_WIDTH, dtype=F32))[None, :]
    tl = min(512, seq)
    fixed = lambda i: (0, 0)
    kt = pl.pallas_call(
        functools.partial(_filter_kernel, seq=seq),
        grid=(seq // tl,),
        in_specs=[pl.BlockSpec((tl, LANES), lambda i: (i, 0)),
                  pl.BlockSpec((LANES, LANES), fixed), pl.BlockSpec((1, LANES), fixed),
                  pl.BlockSpec((LANES, LANES), fixed), pl.BlockSpec((1, LANES), fixed),
                  pl.BlockSpec((LANES, 2 * nout), fixed), pl.BlockSpec((2, LANES), fixed),
                  pl.BlockSpec((1, nout), fixed)],
        out_specs=pl.BlockSpec((2, tl, nout), lambda i: (0, i, 0)),
        out_shape=jax.ShapeDtypeStruct((2, seq, nout), F32),
        compiler_params=_cparams("parallel"),
        name="hyena_filter_mlp",
    )(feats, w1p.astype(BF16), both(b1), w2p.astype(BF16), both(b2), w3p.astype(BF16), both(freq),
      jnp.tile(deltas, (1, HY_ORDER)))
    return kt.reshape(2 * seq, nout)


def _dft_tables(seq):
    n = 2 * seq
    n1 = n // DFT_N2
    idx1 = np.arange(n1)
    idx2 = np.arange(DFT_N2)
    f1 = np.exp(-2j * np.pi * np.outer(idx1, idx1) / n1)
    k = idx1[:, None, None] + n1 * idx2[None, :, None]
    g = np.exp(-2j * np.pi * (k * idx2[None, None, :] % n) / n)
    return n1, f1, g


def _rows_block(nin):
    s_mm = max(8, LANES // nin)
    return s_mm, max(16, s_mm)


def _kron_rows(mat, s):
    return jnp.asarray(np.kron(mat, np.eye(s)), F32).astype(BF16)


def _rows_real_kernel(u_ref, kr_ref, ki_ref, ar_ref, ai_ref, *, s_mm):
    nin, s_blk, c = u_ref.shape
    n1 = ar_ref.shape[0]
    re, im = [], []
    for h in range(s_blk // s_mm):
        u = u_ref[:, h * s_mm:(h + 1) * s_mm, :].reshape(nin * s_mm, c).astype(BF16)
        re.append(_dot(kr_ref[...], u).reshape(n1, s_mm, c))
        im.append(_dot(ki_ref[...], u).reshape(n1, s_mm, c))
    ar_ref[...] = jnp.concatenate(re, axis=1).astype(BF16)
    ai_ref[...] = jnp.concatenate(im, axis=1).astype(BF16)


def _rows_real(u, seq):
    rows, c = u.shape
    n1, f1, _ = _dft_tables(seq)
    nin = rows // DFT_N2
    s_mm, s_blk = _rows_block(nin)
    fixed = lambda j: (0, 0)
    out_spec = pl.BlockSpec((n1, s_blk, c), lambda j: (0, j, 0))
    out_shape = jax.ShapeDtypeStruct((n1, DFT_N2, c), BF16)
    return pl.pallas_call(
        functools.partial(_rows_real_kernel, s_mm=s_mm),
        grid=(DFT_N2 // s_blk,),
        in_specs=[pl.BlockSpec((nin, s_blk, c), lambda j: (0, j, 0)),
                  pl.BlockSpec((n1 * s_mm, nin * s_mm), fixed),
                  pl.BlockSpec((n1 * s_mm, nin * s_mm), fixed)],
        out_specs=[out_spec, out_spec],
        out_shape=[out_shape, out_shape],
        compiler_params=_cparams("parallel"),
        name="hyena_filter_dft_rows",
    )(u.reshape(nin, DFT_N2, c), _kron_rows(f1.real[:, :nin], s_mm), _kron_rows(f1.imag[:, :nin], s_mm))


def _rows_pair_kernel(u_ref, kc_ref, ar_ref, ai_ref, *, s_mm):
    _, nin, s_blk, c = u_ref.shape
    m = kc_ref.shape[0] // 2
    re, im = [], []
    for h in range(s_blk // s_mm):
        sub = slice(h * s_mm, (h + 1) * s_mm)
        p = _dot(kc_ref[...], u_ref[0, :, sub, :].reshape(nin * s_mm, c).astype(BF16))
        q = _dot(kc_ref[...], u_ref[1, :, sub, :].reshape(nin * s_mm, c).astype(BF16))
        re.append((p[:m] - q[m:]).reshape(m // s_mm, s_mm, c))
        im.append((q[:m] + p[m:]).reshape(m // s_mm, s_mm, c))
    ar_ref[...] = jnp.concatenate(re, axis=1).astype(BF16)
    ai_ref[...] = jnp.concatenate(im, axis=1).astype(BF16)


def _rows_pair(u, seq):
    b, rows, c = u.shape
    n1, f1, _ = _dft_tables(seq)
    nin = rows // DFT_N2
    s_mm, s_blk = _rows_block(nin)
    kcat = jnp.concatenate([_kron_rows(f1.real[:, :nin], s_mm), _kron_rows(f1.imag[:, :nin], s_mm)], axis=0)
    out_spec = pl.BlockSpec((None, n1, s_blk, c), lambda p, j: (p, 0, j, 0))
    out_shape = jax.ShapeDtypeStruct((b // 2, n1, DFT_N2, c), BF16)
    return pl.pallas_call(
        functools.partial(_rows_pair_kernel, s_mm=s_mm),
        grid=(b // 2, DFT_N2 // s_blk),
        in_specs=[pl.BlockSpec((2, nin, s_blk, c), lambda p, j: (p, 0, j, 0)),
                  pl.BlockSpec(kcat.shape, lambda p, j: (0, 0))],
        out_specs=[out_spec, out_spec],
        out_shape=[out_shape, out_shape],
        compiler_params=_cparams("parallel", "parallel"),
        name="hyena_dft_rows",
    )(u.reshape(b, nin, DFT_N2, c), kcat)


def _complex_apply(mcat, xr, xi):
    p = _dot(mcat, xr)
    q = _dot(mcat, xi)
    return p[:DFT_N2] - q[DFT_N2:], q[:DFT_N2] + p[DFT_N2:]


def _spectrum_kernel(ar_ref, ai_ref, g_ref, xr_ref, xi_ref, *, scale):
    for j in range(ar_ref.shape[0]):
        xr, xi = _complex_apply(g_ref[j], ar_ref[j], ai_ref[j])
        xr_ref[j] = (xr * scale).astype(BF16)
        xi_ref[j] = (xi * scale).astype(BF16)


def _slab_conv_kernel(ar_ref, ai_ref, kr_ref, ki_ref, g_ref, h_ref, dr_ref, di_ref):
    for j in range(ar_ref.shape[0]):
        xr, xi = _complex_apply(g_ref[j], ar_ref[j], ai_ref[j])
        kr = kr_ref[j].astype(F32)
        ki = ki_ref[j].astype(F32)
        zr = (xr * kr - xi * ki).astype(BF16)
        zi = (xr * ki + xi * kr).astype(BF16)
        dr, di = _complex_apply(h_ref[j], zr, zi)
        dr_ref[j] = dr.astype(BF16)
        di_ref[j] = di.astype(BF16)


def _slab_consts(seq):
    n1, _, g = _dft_tables(seq)
    gcat = jnp.asarray(np.concatenate([g.real, g.imag], axis=1), F32).astype(BF16)
    gt = np.conj(g).transpose(0, 2, 1)
    hcat = jnp.asarray(np.concatenate([gt.real, gt.imag], axis=1), F32).astype(BF16)
    return n1, gcat, hcat


def _spectrum(ar, ai, seq):
    n1, gcat, _ = _slab_consts(seq)
    c = ar.shape[-1]
    g = min(SLABS_PER_STEP, n1)
    slab = pl.BlockSpec((g, DFT_N2, c), lambda i: (i, 0, 0))
    return pl.pallas_call(
        functools.partial(_spectrum_kernel, scale=1.0 / (2 * seq)),
        grid=(n1 // g,),
        in_specs=[slab, slab, pl.BlockSpec((g, 2 * DFT_N2, DFT_N2), lambda i: (i, 0, 0))],
        out_specs=[slab, slab],
        out_shape=[jax.ShapeDtypeStruct((n1, DFT_N2, c), BF16)] * 2,
        compiler_params=_cparams("parallel"),
        name="hyena_filter_spectrum",
    )(ar, ai, gcat)


def _slab_conv(ar, ai, kr, ki, order, seq):
    n1, gcat, hcat = _slab_consts(seq)
    npair, _, _, c = ar.shape
    g = min(SLABS_PER_STEP, n1)
    slab = pl.BlockSpec((None, g, DFT_N2, c), lambda p, i: (p, i, 0, 0))
    filt = pl.BlockSpec((g, DFT_N2, c), lambda p, i: (i, 0, order))
    mats = pl.BlockSpec((g, 2 * DFT_N2, DFT_N2), lambda p, i: (i, 0, 0))
    return pl.pallas_call(
        _slab_conv_kernel,
        grid=(npair, n1 // g),
        in_specs=[slab, slab, filt, filt, mats, mats],
        out_specs=[slab, slab],
        out_shape=[jax.ShapeDtypeStruct(ar.shape, BF16)] * 2,
        compiler_params=_cparams("parallel", "parallel"),
        name="hyena_dft_slabs",
    )(ar, ai, kr, ki, gcat, hcat)


def _rows_inverse_kernel(dr_ref, di_ref, kc_ref, u_ref, gate_ref, bias_ref, o_ref, *, s_mm):
    n1, s_blk, c = dr_ref.shape
    m = kc_ref.shape[0] // 2
    dr = dr_ref[...].astype(F32)
    di = di_ref[...].astype(F32)
    y0, y1 = [], []
    for h in range(s_blk // s_mm):
        sub = slice(h * s_mm, (h + 1) * s_mm)
        p = _dot(kc_ref[...], dr[:, sub, :].reshape(n1 * s_mm, c).astype(BF16))
        q = _dot(kc_ref[...], di[:, sub, :].reshape(n1 * s_mm, c).astype(BF16))
        y0.append((p[:m] + q[m:]).reshape(m // s_mm, s_mm, c))
        y1.append((q[:m] - p[m:]).reshape(m // s_mm, s_mm, c))
    for e, parts in enumerate((y0, y1)):
        y = jnp.concatenate(parts, axis=1)
        o_ref[e] = (gate_ref[e] * (y + u_ref[e] * bias_ref[...])).astype(o_ref.dtype)


def _rows_inverse(dr, di, u, gate, bias, seq, out_dtype):
    npair, n1, _, c = dr.shape
    _, f1, _ = _dft_tables(seq)
    nout = seq // DFT_N2
    s_mm, s_blk = _rows_block(nout)
    kcat = jnp.concatenate([_kron_rows(f1.real[:nout], s_mm), _kron_rows(f1.imag[:nout], s_mm)], axis=0)
    dspec = pl.BlockSpec((None, n1, s_blk, c), lambda p, j: (p, 0, j, 0))
    uspec = pl.BlockSpec((2, nout, s_blk, c), lambda p, j: (p, 0, j, 0))
    y = pl.pallas_call(
        functools.partial(_rows_inverse_kernel, s_mm=s_mm),
        grid=(npair, DFT_N2 // s_blk),
        in_specs=[dspec, dspec, pl.BlockSpec(kcat.shape, lambda p, j: (0, 0)),
                  uspec, uspec, pl.BlockSpec((1, 1, c), lambda p, j: (0, 0, 0))],
        out_specs=uspec,
        out_shape=jax.ShapeDtypeStruct((2 * npair, nout, DFT_N2, c), out_dtype),
        compiler_params=_cparams("parallel", "parallel"),
        name="hyena_idft_rows",
    )(dr, di, kcat, u.reshape(2 * npair, nout, DFT_N2, c), gate.reshape(2 * npair, nout, DFT_N2, c),
      bias.reshape(1, 1, c))
    return y.reshape(2 * npair, seq, c)


def _hyena(p, conv_w, conv_b, w1, b1, w2, b2, w3, freq, bias):
    b, seq, _ = p.shape
    assert b % 2 == 0, "batch elements are transformed in pairs"
    x1, x2, v = _shortconv(p, conv_w, conv_b.reshape(1, -1))
    kt = _hyena_filters(seq, w1, b1, w2, b2, w3, freq)
    kr, ki = _spectrum(*_rows_real(kt, seq), seq)
    dr, di = _slab_conv(*_rows_pair(v, seq), kr, ki, 0, seq)
    z = _rows_inverse(dr, di, v, x1, bias[0], seq, F32)
    dr, di = _slab_conv(*_rows_pair(z, seq), kr, ki, 1, seq)
    return _rows_inverse(dr, di, z, x2, bias[1], seq, BF16)


def _mix_ffn_kernel(x_ref, hy_ref, sg_ref, at_ref, wo_ref, g1_ref, n2_ref, sc_ref, sh_ref, g2_ref, w1_ref,
                    w2_ref, o_ref, *, chunk):
    mix = _dot(hy_ref[...], wo_ref[:HY_WIDTH])
    mix = mix + _dot(sg_ref[...], wo_ref[HY_WIDTH:HY_WIDTH + SG_WIDTH])
    mix = mix + _dot(at_ref[...], wo_ref[HY_WIDTH + SG_WIDTH:])
    x = x_ref[...] + g1_ref[...] * mix
    hb = _modulated_norm(x, n2_ref[...], sc_ref[...], sh_ref[...]).astype(BF16)
    hidden = w2_ref.shape[0]
    acc = jnp.zeros(x.shape, F32)
    for c in range(hidden // chunk):
        a = _dot(hb, w1_ref[:, c * chunk:(c + 1) * chunk])
        up = _dot(hb, w1_ref[:, hidden + c * chunk:hidden + (c + 1) * chunk])
        act = (a * _sigmoid(a) * up).astype(BF16)
        acc = acc + _dot(act, w2_ref[c * chunk:(c + 1) * chunk, :])
    o_ref[...] = x + g2_ref[...] * acc


def _mix_ffn(x2d, seq, layer, hy, sg, att, w_out, g1, n2, sc, sh, g2, w1, w2):
    m, d = x2d.shape
    tm = min(FFN_TM, seq)
    per = seq // tm
    row = lambda i: (i, 0)
    fixed = lambda i: (0, 0)
    bat = lambda i: (i // per, 0, 0)
    resident = lambda a: pl.BlockSpec((None,) + a.shape[1:], lambda i: (layer, 0, 0), pipeline_mode=pl.Buffered(1))
    return pl.pallas_call(
        functools.partial(_mix_ffn_kernel, chunk=FFN_CHUNK),
        grid=(m // tm,),
        in_specs=[pl.BlockSpec((tm, d), row),
                  pl.BlockSpec((tm, HY_WIDTH), row),
                  pl.BlockSpec((tm, SG_WIDTH), row),
                  pl.BlockSpec((tm, DA_WIDTH), row),
                  resident(w_out),
                  pl.BlockSpec((None, 1, d), bat),
                  pl.BlockSpec((1, d), fixed),
                  pl.BlockSpec((None, 1, d), bat),
                  pl.BlockSpec((None, 1, d), bat),
                  pl.BlockSpec((None, 1, d), bat),
                  resident(w1),
                  resident(w2)],
        out_specs=pl.BlockSpec((tm, d), row),
        out_shape=jax.ShapeDtypeStruct((m, d), F32),
        compiler_params=_cparams("parallel"),
        name="outproj_ffn",
    )(x2d, hy, sg, att, w_out, g1, n2, sc, sh, g2, w1, w2)


def _rope_tables(seq):
    t = jnp.arange(seq, dtype=jnp.int32)
    half = DA_HEAD_DIM // 4
    inv_freq = ROPE_BASE ** (-jnp.arange(half, dtype=F32) / half)
    ang_row = (t // GRID_W).astype(F32)[:, None] * inv_freq[None, :]
    ang_col = (t % GRID_W).astype(F32)[:, None] * inv_freq[None, :]
    cos = jnp.concatenate([jnp.cos(ang_row)] * 2 + [jnp.cos(ang_col)] * 2, axis=-1)
    sin = jnp.concatenate([-jnp.sin(ang_row), jnp.sin(ang_row), -jnp.sin(ang_col), jnp.sin(ang_col)], axis=-1)
    return jnp.tile(cos, (1, 2)), jnp.tile(sin, (1, 2))


def kernel(x, c, ctx, c_ctx, norm1_g, norm2_g, ada_w, ada_b, w_in, hy_conv_w, hy_conv_b, hy_w1, hy_b1,
           hy_w2, hy_b2, hy_w3, hy_freq, hy_bias, sg_norm_g, sg_w, sg_b, qn_g, kn_g, lam_p, subln_g,
           w_out, ffn_w1, ffn_w2):
    batch, seq, d = x.shape
    ctx_len = ctx.shape[1]
    depth = w_in.shape[0]
    assert batch + 1 <= 8 and seq % CHUNK == 0 and ctx_len % CHUNK == 0

    cvecs = jnp.zeros((8, d), F32).at[:batch].set(c).at[batch].set(c_ctx)
    mods = _ada(cvecs, ada_w, ada_b[:, None, :])

    cos_l, sin_l = _rope_tables(seq)
    cos_c = jnp.ones((ctx_len, LANES), F32)
    sin_c = jnp.zeros((ctx_len, LANES), F32)
    seg_np = np.kron(np.eye(LANES // DA_HEAD_DIM), np.full((DA_HEAD_DIM, DA_HEAD_DIM), 1.0 / DA_HEAD_DIM))
    seg = jnp.asarray(np.concatenate([seg_np, seg_np], axis=0), F32).astype(BF16)

    w_in_b = w_in.astype(BF16)
    w_out_b = w_out.astype(BF16)
    w1_b = ffn_w1.astype(BF16)
    w2_b = ffn_w2.astype(BF16)
    xs = x.reshape(batch * seq, d)
    cs = ctx.reshape(batch * ctx_len, d)
    for i in range(depth):
        last = i == depth - 1
        lam_init = 0.8 - 0.6 * math.exp(-0.3 * i)
        mod_l = [m[:, None, :] for m in jnp.split(mods[i, :batch], 6, axis=-1)]
        mod_c = [jnp.broadcast_to(m[None], (batch, 1, d)) for m in jnp.split(mods[i, batch:batch + 1], 6, axis=-1)]
        qg = jnp.tile(qn_g[i], 2)[None, :]
        kg = jnp.tile(kn_g[i], 2)[None, :]
        sgw = sg_w[i].reshape(SG_HEADS * CHUNK, CHUNK).astype(BF16)
        sgb = jnp.repeat(sg_b[i].T, SG_WIDTH // SG_HEADS, axis=1)
        hy_params = (hy_conv_w[i], hy_conv_b[i], hy_w1[i], hy_b1[i], hy_w2[i], hy_b2[i], hy_w3[i],
                     hy_freq[i], hy_bias[i])

        def project(tokens, n_tok, mod, cos, sin):
            outs = _inproj(tokens, n_tok, i, norm1_g[i][None, :], mod[1], mod[0], w_in_b, cos, sin, qg, kg, seg,
                           sg_norm_g[i][None, :], sgw, sgb)
            return list(outs[:2]) + [a.reshape(batch, n_tok, DA_WIDTH) for a in outs[2:]]

        def finish(tokens, n_tok, mod, p_hy, sg, att):
            hy = _hyena(p_hy.reshape(batch, n_tok, HY_IN), *hy_params).reshape(batch * n_tok, HY_WIDTH)
            return _mix_ffn(tokens, n_tok, i, hy, sg, att.reshape(batch * n_tok, DA_WIDTH), w_out_b, mod[2],
                            norm2_g[i][None, :], mod[4], mod[3], mod[5], w1_b, w2_b)

        hy_l, sg_l, q_l, k_l, v_l = project(xs, seq, mod_l, cos_l, sin_l)
        hy_c, sg_c, q_c, k_c, v_c = project(cs, ctx_len, mod_c, cos_c, sin_c)
        attend = functools.partial(_attention, lam_p=lam_p[i], subln_g=subln_g[i][None, :], lam_init=lam_init,
                                   qn_g=qn_g[i], kn_g=kn_g[i])
        att_l = attend(q_l, [(k_l, v_l), (k_c, v_c)])
        xs = finish(xs, seq, mod_l, hy_l, sg_l, att_l)
        if not last:
            cs = finish(cs, ctx_len, mod_c, hy_c, sg_c, attend(q_c, [(k_c, v_c)]))
    return xs.reshape(batch, seq, d)
```

```python
import functools
import math

import numpy as np
import jax
import jax.numpy as jnp
from jax import lax
from jax.experimental import pallas as pl
from jax.experimental.pallas import tpu as pltpu

F32 = jnp.float32
BF16 = jnp.bfloat16

GRID_W = 64
EPS = 1e-6
HY_WIDTH = 256
HY_ORDER = 2
HY_POS_BANDS = 16
HY_DECAY_TARGET = 1e-2
HY_FAST_DECAY_PCT = 0.3
HY_SLOW_DECAY_PCT = 1.5
SG_HEADS = 4
SG_WIDTH = 256
CHUNK = 128
DA_HEADS = 4
DA_WIDTH = 512
DA_V_DIM = 128
DA_HEAD_DIM = 64
ROPE_BASE = 10000.0
HY_IN = 3 * HY_WIDTH
SG_IN = 2 * SG_WIDTH
LANES = 128
DFT_N2 = 128
SLABS_PER_STEP = 16
VMEM_LIMIT = 56 * 1024 * 1024
NEG_BIG = -1e30
Q_SCALE = math.log2(math.e) * DA_HEAD_DIM ** -0.5
INPROJ_TM = 1024
FFN_TM = 512
FFN_CHUNK = 256
ATTN_TQ = 1024
ATTN_TK = 256


def _cparams(*sem):
    return pltpu.CompilerParams(dimension_semantics=sem, vmem_limit_bytes=VMEM_LIMIT)


def _dot(a, b):
    return jnp.dot(a, b, preferred_element_type=F32)


def _sigmoid(x):
    return 1.0 / (1.0 + jnp.exp(-x))


def _modulated_norm(x, g, sc, sh):
    y = x * lax.rsqrt(jnp.mean(x * x, axis=-1, keepdims=True) + EPS)
    return (y * g) * (1.0 + sc) + sh


def _ada_kernel(c_ref, w_ref, b_ref, o_ref):
    cv = c_ref[...]
    s = (cv * _sigmoid(cv)).astype(BF16)
    o_ref[...] = _dot(s, w_ref[...].astype(BF16)) + b_ref[...]


def _ada(cvecs, ada_w, ada_b):
    depth, d, n = ada_w.shape
    tn = 1536
    return pl.pallas_call(
        _ada_kernel,
        grid=(depth, n // tn),
        in_specs=[pl.BlockSpec((8, d), lambda i, j: (0, 0)),
                  pl.BlockSpec((None, d, tn), lambda i, j: (i, 0, j)),
                  pl.BlockSpec((None, 1, tn), lambda i, j: (i, 0, j))],
        out_specs=pl.BlockSpec((None, 8, tn), lambda i, j: (i, 0, j)),
        out_shape=jax.ShapeDtypeStruct((depth, 8, n), F32),
        compiler_params=_cparams("parallel", "parallel"),
        name="adaln",
    )(cvecs, ada_w, ada_b)


def _group_mean_sq(t, seg):
    sq = t * t
    hi = sq.astype(BF16)
    lo = (sq - hi.astype(F32)).astype(BF16)
    return _dot(jnp.concatenate([hi, lo], axis=1), seg)


def _norm_rope(t, gain, seg, cos, sin, swap_fwd):
    tn = t * lax.rsqrt(_group_mean_sq(t, seg) + EPS) * gain
    rot = jnp.where(swap_fwd, pltpu.roll(tn, LANES - 16, 1), pltpu.roll(tn, 16, 1))
    return tn * cos + rot * sin


def _inproj_kernel(x_ref, xp_ref, xn_ref, g_ref, sc_ref, sh_ref, w_ref, cw_ref, cb_ref, cos_ref, sin_ref, qg_ref,
                   kg_ref, seg_ref, sgg_ref, sgw_ref, sgb_ref, hx1_ref, hx2_ref, hxv_ref, sg_ref, q_ref, k_ref, v_ref,
                   *, per):
    tm = x_ref.shape[0]
    hb = _modulated_norm(x_ref[...], g_ref[...], sc_ref[...], sh_ref[...]).astype(BF16)

    halo = jnp.concatenate([xp_ref[...], xn_ref[...]], axis=0)
    hh = _modulated_norm(halo, g_ref[...], sc_ref[...], sh_ref[...]).astype(BF16)
    p_ext = _dot(jnp.concatenate([hb, hh], axis=0), w_ref[:, :HY_IN])
    pos = pl.program_id(0) % per
    before = jnp.where(pos == 0, 0.0, p_ext[tm + 7:tm + 8])
    after = jnp.where(pos == per - 1, 0.0, p_ext[tm + 8:tm + 9])
    rid = lax.broadcasted_iota(jnp.int32, (tm, 1), 0)
    for part, hx_ref in enumerate((hx1_ref, hx2_ref, hxv_ref)):
        cols = slice(part * HY_WIDTH, (part + 1) * HY_WIDTH)
        cur = p_ext[:tm, cols]
        prev = jnp.where(rid == 0, before[:, cols], pltpu.roll(cur, 1, 0))
        nxt = jnp.where(rid == tm - 1, after[:, cols], pltpu.roll(cur, tm - 1, 0))
        hx_ref[...] = prev * cw_ref[0:1, cols] + cur * cw_ref[1:2, cols] + nxt * cw_ref[2:3, cols] + cb_ref[:, cols]

    psg = _dot(hb, w_ref[:, HY_IN:HY_IN + SG_IN])
    ge = 0.5 * psg * (1.0 + lax.erf(psg * np.float32(math.sqrt(0.5))))
    u = ge[:, :SG_WIDTH]
    vv = ge[:, SG_WIDTH:]
    vn = (vv * lax.rsqrt(jnp.mean(vv * vv, axis=-1, keepdims=True) + EPS) * sgg_ref[...]).astype(BF16)
    head_of_lane = lax.broadcasted_iota(jnp.int32, (CHUNK, SG_WIDTH), 1) // (SG_WIDTH // SG_HEADS)
    for c in range(tm // CHUNK):
        rows = slice(c * CHUNK, (c + 1) * CHUNK)
        r = _dot(sgw_ref[...], vn[rows])
        mixed = sgb_ref[...]
        for h in range(SG_HEADS):
            mixed = mixed + jnp.where(head_of_lane == h, r[h * CHUNK:(h + 1) * CHUNK], 0.0)
        sg_ref[rows, :] = (u[rows] * mixed).astype(BF16)

    off = HY_IN + SG_IN
    lane = lax.broadcasted_iota(jnp.int32, (1, LANES), 1)
    swap_fwd = (lane % 32) < 16
    cos = cos_ref[...]
    sin = sin_ref[...]
    seg = seg_ref[...]
    pq = _dot(hb, w_ref[:, off:off + DA_WIDTH])
    pk = _dot(hb, w_ref[:, off + DA_WIDTH:off + 2 * DA_WIDTH])
    v_ref[...] = _dot(hb, w_ref[:, off + 2 * DA_WIDTH:]).astype(BF16)
    for h in range(DA_HEADS):
        cols = slice(h * LANES, (h + 1) * LANES)
        q = _norm_rope(pq[:, cols], qg_ref[...], seg, cos, sin, swap_fwd)
        q_ref[:, cols] = (q * Q_SCALE).astype(BF16)
        k_ref[:, cols] = _norm_rope(pk[:, cols], kg_ref[...], seg, cos, sin, swap_fwd).astype(BF16)


def _inproj(x2d, seq, layer, g, sc, sh, w, conv_w, conv_b, cos, sin, qg, kg, seg, sgg, sgw, sgb):
    m, d = x2d.shape
    n = w.shape[2]
    tm = min(INPROJ_TM, seq)
    per = seq // tm
    halo = 8
    row = lambda i: (i, 0)
    fixed = lambda i: (0, 0)
    bat = lambda i: (i // per, 0, 0)
    pos = lambda i: (i % per, 0)
    outs = [(HY_WIDTH, F32)] * 3 + [(SG_WIDTH, BF16)] + [(DA_WIDTH, BF16)] * 3
    return pl.pallas_call(
        functools.partial(_inproj_kernel, per=per),
        grid=(m // tm,),
        in_specs=[pl.BlockSpec((tm, d), row),
                  pl.BlockSpec((halo, d), lambda i: (jnp.maximum(i * (tm // halo) - 1, 0), 0)),
                  pl.BlockSpec((halo, d), lambda i: (jnp.minimum((i + 1) * (tm // halo), m // halo - 1), 0)),
                  pl.BlockSpec((1, d), fixed),
                  pl.BlockSpec((None, 1, d), bat),
                  pl.BlockSpec((None, 1, d), bat),
                  pl.BlockSpec((None, d, n), lambda i: (layer, 0, 0)),
                  pl.BlockSpec((3, HY_IN), fixed),
                  pl.BlockSpec((1, HY_IN), fixed),
                  pl.BlockSpec((tm, LANES), pos),
                  pl.BlockSpec((tm, LANES), pos),
                  pl.BlockSpec((1, LANES), fixed),
                  pl.BlockSpec((1, LANES), fixed),
                  pl.BlockSpec((2 * LANES, LANES), fixed),
                  pl.BlockSpec((1, SG_WIDTH), fixed),
                  pl.BlockSpec((SG_HEADS * CHUNK, CHUNK), fixed),
                  pl.BlockSpec((CHUNK, SG_WIDTH), fixed)],
        out_specs=[pl.BlockSpec((tm, c), row) for c, _ in outs],
        out_shape=[jax.ShapeDtypeStruct((m, c), dt) for c, dt in outs],
        compiler_params=_cparams("parallel"),
        name="inproj",
    )(x2d, x2d, x2d, g, sc, sh, w, conv_w, conv_b, cos, sin, qg, kg, seg, sgg, sgw, sgb)


_NT = (((1,), (1,)), ((), ()))


def _lambda(lp_ref, lam_init):
    lp = lp_ref[...]
    return (jnp.exp(jnp.sum(lp[0:1] * lp[1:2], axis=-1, keepdims=True))
            - jnp.exp(jnp.sum(lp[2:3] * lp[3:4], axis=-1, keepdims=True)) + lam_init)


def _stacked_components(q):
    lane = lax.broadcasted_iota(jnp.int32, (1, LANES), 1)
    zero = jnp.zeros_like(q)
    return jnp.concatenate([jnp.where(lane < DA_HEAD_DIM, q, zero), jnp.where(lane >= DA_HEAD_DIM, q, zero)],
                           axis=0)


def _attn_finish(o1, o2, lam, g_ref, o_ref, lam_init):
    o = o1 - lam * o2
    o = o * lax.rsqrt(jnp.mean(o * o, axis=-1, keepdims=True) + EPS) * g_ref[...]
    o_ref[...] = (o * (1.0 - lam_init)).astype(BF16)


def _attn_bounded_kernel(lp_ref, g_ref, q_ref, *refs, tiles, lam_init):
    tq = q_ref.shape[0]
    o_ref = refs[-1]
    qs = _stacked_components(q_ref[...])
    acc = jnp.zeros((2 * tq, 2 * LANES), F32)
    for j, tk in enumerate(tiles):
        k_ref, v_ref = refs[2 * j], refs[2 * j + 1]
        ones = jnp.ones((tk, LANES), BF16)
        for i in range(k_ref.shape[0] // tk):
            kb = k_ref[i * tk:(i + 1) * tk, :]
            vb = jnp.concatenate([v_ref[i * tk:(i + 1) * tk, :], ones], axis=1)
            s = lax.dot_general(qs, kb, _NT, preferred_element_type=F32)
            acc = acc + _dot(jnp.exp2(s).astype(BF16), vb)
    o1 = acc[:tq, :LANES] / acc[:tq, LANES:]
    o2 = acc[tq:, :LANES] / acc[tq:, LANES:]
    _attn_finish(o1, o2, _lambda(lp_ref, lam_init), g_ref, o_ref, lam_init)


def _attn_online_kernel(lp_ref, g_ref, q_ref, *refs, tiles, lam_init):
    tq = q_ref.shape[0]
    o_ref = refs[-1]
    qs = _stacked_components(q_ref[...])
    carry = (jnp.full((2 * tq, 1), NEG_BIG, F32), jnp.zeros((2 * tq, 1), F32),
             jnp.zeros((2 * tq, DA_V_DIM), F32))
    for j, tk in enumerate(tiles):
        k_ref, v_ref = refs[2 * j], refs[2 * j + 1]

        def body(i, carry, k_ref=k_ref, v_ref=v_ref, tk=tk):
            m, l, acc = carry
            start = pl.multiple_of(i * tk, tk)
            kb = k_ref[pl.ds(start, tk), :]
            vb = v_ref[pl.ds(start, tk), :]
            s = lax.dot_general(qs, kb, _NT, preferred_element_type=F32)
            m_new = jnp.maximum(m, jnp.max(s, axis=-1, keepdims=True))
            alpha = jnp.exp2(m - m_new)
            p = jnp.exp2(s - m_new)
            l_new = alpha * l + jnp.sum(p, axis=-1, keepdims=True)
            return m_new, l_new, alpha * acc + _dot(p.astype(BF16), vb)

        carry = lax.fori_loop(0, k_ref.shape[0] // tk, body, carry)
    _, l, acc = carry
    o = acc / l
    _attn_finish(o[:tq], o[tq:], _lambda(lp_ref, lam_init), g_ref, o_ref, lam_init)


def _pick_tile(n, candidates):
    for c in candidates:
        if n % c == 0:
            return c
    raise ValueError(f"no tile for {n}")


BOUNDED_SCORE_LIMIT = 56.0


def _attention(q, kv, lam_p, subln_g, lam_init, qn_g, kn_g):
    bound = (math.sqrt(DA_HEAD_DIM) * math.log2(math.e) * 1.02) * jnp.max(jnp.abs(qn_g)) * jnp.max(jnp.abs(kn_g))
    flat = [a for pair in kv for a in pair]
    return lax.cond(bound <= BOUNDED_SCORE_LIMIT,
                    functools.partial(_attention_call, _attn_bounded_kernel, lam_init),
                    functools.partial(_attention_call, _attn_online_kernel, lam_init),
                    q, lam_p, subln_g, *flat)


def _attention_call(body, lam_init, q, lam_p, subln_g, *kv):
    b, lq, _ = q.shape
    tq = _pick_tile(lq, (ATTN_TQ, 512, 256, 128))
    tiles = tuple(_pick_tile(a.shape[1], (ATTN_TK, 512, 256, 128)) for a in kv[::2])
    whole = lambda a: pl.BlockSpec((None, a.shape[1], LANES), lambda bi, h, i: (bi, 0, h))
    return pl.pallas_call(
        functools.partial(body, tiles=tiles, lam_init=lam_init),
        grid=(b, DA_HEADS, lq // tq),
        in_specs=[pl.BlockSpec((4, DA_HEAD_DIM), lambda bi, h, i: (0, 0)),
                  pl.BlockSpec((1, DA_V_DIM), lambda bi, h, i: (0, 0)),
                  pl.BlockSpec((None, tq, LANES), lambda bi, h, i: (bi, i, h))] + [whole(a) for a in kv],
        out_specs=pl.BlockSpec((None, tq, LANES), lambda bi, h, i: (bi, i, h)),
        out_shape=jax.ShapeDtypeStruct((b, lq, DA_WIDTH), BF16),
        compiler_params=_cparams("parallel", "parallel", "parallel"),
        name="diff_attention",
    )(lam_p, subln_g, q, *kv)


def _filter_kernel(f_ref, w1_ref, b1_ref, w2_ref, b2_ref, w3_ref, fr_ref, dl_ref, o_ref, *, seq):
    tl = f_ref.shape[0]
    half = LANES // 2
    nout = HY_ORDER * HY_WIDTH
    feats = f_ref[...]
    h = jnp.sin(fr_ref[0:1, :] * (_dot(feats.astype(BF16), w1_ref[...]) + b1_ref[...]))
    h = jnp.sin(fr_ref[1:2, :] * (_dot(h.astype(BF16), w2_ref[...]) + b2_ref[...]))
    h = _dot(h.astype(BF16), w3_ref[...])
    row = pl.program_id(0) * tl + lax.broadcasted_iota(jnp.int32, (tl, 1), 0)
    o_ref[0] = h[:, :nout] * jnp.exp(-feats[:, 0:1] * dl_ref[...])
    bwd = h[:, nout:] * jnp.exp(-feats[:, half:half + 1] * dl_ref[...])
    o_ref[1] = jnp.where(row == 0, 0.0, bwd)


def _hyena_features(seq):
    t = jnp.linspace(0.0, 1.0, seq, dtype=F32)[:, None]
    bands = jnp.linspace(1e-4, HY_POS_BANDS - 1, HY_POS_BANDS, dtype=F32)
    ang = (2.0 * math.pi / seq) * jnp.arange(seq, dtype=F32)[:, None] * bands[None, :]
    feats = jnp.concatenate([t, jnp.cos(ang), -jnp.sin(ang)], axis=-1)
    back = jnp.concatenate([feats[:1], feats[:0:-1]], axis=0)
    pad = ((0, 0), (0, LANES // 2 - feats.shape[1]))
    return jnp.concatenate([jnp.pad(feats, pad), jnp.pad(back, pad)], axis=1)


def _hyena_filters(seq, w1, b1, w2, b2, w3, freq):
    feats = _hyena_features(seq)
    nfeat, hid = w1.shape
    half = LANES // 2
    nout = HY_ORDER * HY_WIDTH
    assert nfeat <= half and hid <= half
    w1p = jnp.zeros((LANES, LANES), F32).at[:nfeat, :hid].set(w1).at[half:half + nfeat, half:half + hid].set(w1)
    w2p = jnp.zeros((LANES, LANES), F32).at[:hid, :hid].set(w2).at[half:half + hid, half:half + hid].set(w2)
    w3d = w3.reshape(hid, HY_ORDER, 2, HY_WIDTH)
    w3p = (jnp.zeros((LANES, 2 * nout), F32)
           .at[:hid, :nout].set(w3d[:, :, 0].reshape(hid, nout))
           .at[half:half + hid, nout:].set(w3d[:, :, 1].reshape(hid, nout)))
    both = lambda a: jnp.tile(jnp.pad(a.reshape(-1, hid), ((0, 0), (0, half - hid))), (1, 2))
    min_decay = math.log(HY_DECAY_TARGET) / HY_SLOW_DECAY_PCT
    max_decay = math.log(HY_DECAY_TARGET) / HY_FAST_DECAY_PCT
    deltas = jnp.abs(jnp.linspace(min_decay, max_decay, HY_WIDTH, dtype=F32))[None, :]
    tl = min(512, seq)
    fixed = lambda i: (0, 0)
    kt = pl.pallas_call(
        functools.partial(_filter_kernel, seq=seq),
        grid=(seq // tl,),
        in_specs=[pl.BlockSpec((tl, LANES), lambda i: (i, 0)),
                  pl.BlockSpec((LANES, LANES), fixed), pl.BlockSpec((1, LANES), fixed),
                  pl.BlockSpec((LANES, LANES), fixed), pl.BlockSpec((1, LANES), fixed),
                  pl.BlockSpec((LANES, 2 * nout), fixed), pl.BlockSpec((2, LANES), fixed),
                  pl.BlockSpec((1, nout), fixed)],
        out_specs=pl.BlockSpec((2, tl, nout), lambda i: (0, i, 0)),
        out_shape=jax.ShapeDtypeStruct((2, seq, nout), F32),
        compiler_params=_cparams("parallel"),
        name="hyena_filter_mlp",
    )(feats, w1p.astype(BF16), both(b1), w2p.astype(BF16), both(b2), w3p.astype(BF16), both(freq),
      jnp.tile(deltas, (1, HY_ORDER)))
    return kt.reshape(2 * seq, nout)


def _dft_tables(seq):
    n = 2 * seq
    n1 = n // DFT_N2
    idx1 = np.arange(n1)
    idx2 = np.arange(DFT_N2)
    f1 = np.exp(-2j * np.pi * np.outer(idx1, idx1) / n1)
    k = idx1[:, None, None] + n1 * idx2[None, :, None]
    g = np.exp(-2j * np.pi * (k * idx2[None, None, :] % n) / n)
    return n1, f1, g


def _rows_block(nin):
    s_mm = max(8, LANES // nin)
    return s_mm, max(16, s_mm)


def _kron_rows(mat, s):
    return jnp.asarray(np.kron(mat, np.eye(s)), F32).astype(BF16)


def _rows_real_kernel(u_ref, kr_ref, ki_ref, ar_ref, ai_ref, *, s_mm):
    nin, s_blk, c = u_ref.shape
    n1 = ar_ref.shape[0]
    re, im = [], []
    for h in range(s_blk // s_mm):
        u = u_ref[:, h * s_mm:(h + 1) * s_mm, :].reshape(nin * s_mm, c).astype(BF16)
        re.append(_dot(kr_ref[...], u).reshape(n1, s_mm, c))
        im.append(_dot(ki_ref[...], u).reshape(n1, s_mm, c))
    ar_ref[...] = jnp.concatenate(re, axis=1).astype(BF16)
    ai_ref[...] = jnp.concatenate(im, axis=1).astype(BF16)


def _rows_real(u, seq):
    rows, c = u.shape
    n1, f1, _ = _dft_tables(seq)
    nin = rows // DFT_N2
    s_mm, s_blk = _rows_block(nin)
    fixed = lambda j: (0, 0)
    out_spec = pl.BlockSpec((n1, s_blk, c), lambda j: (0, j, 0))
    out_shape = jax.ShapeDtypeStruct((n1, DFT_N2, c), BF16)
    return pl.pallas_call(
        functools.partial(_rows_real_kernel, s_mm=s_mm),
        grid=(DFT_N2 // s_blk,),
        in_specs=[pl.BlockSpec((nin, s_blk, c), lambda j: (0, j, 0)),
                  pl.BlockSpec((n1 * s_mm, nin * s_mm), fixed),
                  pl.BlockSpec((n1 * s_mm, nin * s_mm), fixed)],
        out_specs=[out_spec, out_spec],
        out_shape=[out_shape, out_shape],
        compiler_params=_cparams("parallel"),
        name="hyena_filter_dft_rows",
    )(u.reshape(nin, DFT_N2, c), _kron_rows(f1.real[:, :nin], s_mm), _kron_rows(f1.imag[:, :nin], s_mm))


def _rows_pair_kernel(u_ref, kc_ref, ar_ref, ai_ref, *, s_mm):
    _, nin, s_blk, c = u_ref.shape
    m = kc_ref.shape[0] // 2
    re, im = [], []
    for h in range(s_blk // s_mm):
        sub = slice(h * s_mm, (h + 1) * s_mm)
        p = _dot(kc_ref[...], u_ref[0, :, sub, :].reshape(nin * s_mm, c).astype(BF16))
        q = _dot(kc_ref[...], u_ref[1, :, sub, :].reshape(nin * s_mm, c).astype(BF16))
        re.append((p[:m] - q[m:]).reshape(m // s_mm, s_mm, c))
        im.append((q[:m] + p[m:]).reshape(m // s_mm, s_mm, c))
    ar_ref[...] = jnp.concatenate(re, axis=1).astype(BF16)
    ai_ref[...] = jnp.concatenate(im, axis=1).astype(BF16)


def _rows_pair(u, seq):
    b, rows, c = u.shape
    n1, f1, _ = _dft_tables(seq)
    nin = rows // DFT_N2
    s_mm, s_blk = _rows_block(nin)
    kcat = jnp.concatenate([_kron_rows(f1.real[:, :nin], s_mm), _kron_rows(f1.imag[:, :nin], s_mm)], axis=0)
    out_spec = pl.BlockSpec((None, n1, s_blk, c), lambda p, j: (p, 0, j, 0))
    out_shape = jax.ShapeDtypeStruct((b // 2, n1, DFT_N2, c), BF16)
    return pl.pallas_call(
        functools.partial(_rows_pair_kernel, s_mm=s_mm),
        grid=(b // 2, DFT_N2 // s_blk),
        in_specs=[pl.BlockSpec((2, nin, s_blk, c), lambda p, j: (p, 0, j, 0)),
                  pl.BlockSpec(kcat.shape, lambda p, j: (0, 0))],
        out_specs=[out_spec, out_spec],
        out_shape=[out_shape, out_shape],
        compiler_params=_cparams("parallel", "parallel"),
        name="hyena_dft_rows",
    )(u.reshape(b, nin, DFT_N2, c), kcat)


def _complex_apply(mcat, xr, xi):
    p = _dot(mcat, xr)
    q = _dot(mcat, xi)
    return p[:DFT_N2] - q[DFT_N2:], q[:DFT_N2] + p[DFT_N2:]


def _spectrum_kernel(ar_ref, ai_ref, g_ref, xr_ref, xi_ref, *, scale):
    for j in range(ar_ref.shape[0]):
        xr, xi = _complex_apply(g_ref[j], ar_ref[j], ai_ref[j])
        xr_ref[j] = (xr * scale).astype(BF16)
        xi_ref[j] = (xi * scale).astype(BF16)


def _slab_conv_kernel(ar_ref, ai_ref, kr_ref, ki_ref, g_ref, h_ref, dr_ref, di_ref):
    for j in range(ar_ref.shape[0]):
        xr, xi = _complex_apply(g_ref[j], ar_ref[j], ai_ref[j])
        kr = kr_ref[j].astype(F32)
        ki = ki_ref[j].astype(F32)
        zr = (xr * kr - xi * ki).astype(BF16)
        zi = (xr * ki + xi * kr).astype(BF16)
        dr, di = _complex_apply(h_ref[j], zr, zi)
        dr_ref[j] = dr.astype(BF16)
        di_ref[j] = di.astype(BF16)


def _slab_consts(seq):
    n1, _, g = _dft_tables(seq)
    gcat = jnp.asarray(np.concatenate([g.real, g.imag], axis=1), F32).astype(BF16)
    gt = np.conj(g).transpose(0, 2, 1)
    hcat = jnp.asarray(np.concatenate([gt.real, gt.imag], axis=1), F32).astype(BF16)
    return n1, gcat, hcat


def _spectrum(ar, ai, seq):
    n1, gcat, _ = _slab_consts(seq)
    c = ar.shape[-1]
    g = min(SLABS_PER_STEP, n1)
    slab = pl.BlockSpec((g, DFT_N2, c), lambda i: (i, 0, 0))
    return pl.pallas_call(
        functools.partial(_spectrum_kernel, scale=1.0 / (2 * seq)),
        grid=(n1 // g,),
        in_specs=[slab, slab, pl.BlockSpec((g, 2 * DFT_N2, DFT_N2), lambda i: (i, 0, 0))],
        out_specs=[slab, slab],
        out_shape=[jax.ShapeDtypeStruct((n1, DFT_N2, c), BF16)] * 2,
        compiler_params=_cparams("parallel"),
        name="hyena_filter_spectrum",
    )(ar, ai, gcat)


def _slab_conv(ar, ai, kr, ki, order, seq):
    n1, gcat, hcat = _slab_consts(seq)
    npair, _, _, c = ar.shape
    g = min(SLABS_PER_STEP, n1)
    slab = pl.BlockSpec((None, g, DFT_N2, c), lambda p, i: (p, i, 0, 0))
    filt = pl.BlockSpec((g, DFT_N2, c), lambda p, i: (i, 0, order))
    mats = pl.BlockSpec((g, 2 * DFT_N2, DFT_N2), lambda p, i: (i, 0, 0))
    return pl.pallas_call(
        _slab_conv_kernel,
        grid=(npair, n1 // g),
        in_specs=[slab, slab, filt, filt, mats, mats],
        out_specs=[slab, slab],
        out_shape=[jax.ShapeDtypeStruct(ar.shape, BF16)] * 2,
        compiler_params=_cparams("parallel", "parallel"),
        name="hyena_dft_slabs",
    )(ar, ai, kr, ki, gcat, hcat)


def _rows_inverse_kernel(dr_ref, di_ref, kc_ref, u_ref, gate_ref, bias_ref, o_ref, *, s_mm):
    n1, s_blk, c = dr_ref.shape
    m = kc_ref.shape[0] // 2
    dr = dr_ref[...].astype(F32)
    di = di_ref[...].astype(F32)
    y0, y1 = [], []
    for h in range(s_blk // s_mm):
        sub = slice(h * s_mm, (h + 1) * s_mm)
        p = _dot(kc_ref[...], dr[:, sub, :].reshape(n1 * s_mm, c).astype(BF16))
        q = _dot(kc_ref[...], di[:, sub, :].reshape(n1 * s_mm, c).astype(BF16))
        y0.append((p[:m] + q[m:]).reshape(m // s_mm, s_mm, c))
        y1.append((q[:m] - p[m:]).reshape(m // s_mm, s_mm, c))
    for e, parts in enumerate((y0, y1)):
        y = jnp.concatenate(parts, axis=1)
        o_ref[e] = (gate_ref[e] * (y + u_ref[e] * bias_ref[...])).astype(o_ref.dtype)


def _rows_inverse(dr, di, u, gate, bias, seq, out_dtype):
    npair, n1, _, c = dr.shape
    _, f1, _ = _dft_tables(seq)
    nout = seq // DFT_N2
    s_mm, s_blk = _rows_block(nout)
    kcat = jnp.concatenate([_kron_rows(f1.real[:nout], s_mm), _kron_rows(f1.imag[:nout], s_mm)], axis=0)
    dspec = pl.BlockSpec((None, n1, s_blk, c), lambda p, j: (p, 0, j, 0))
    uspec = pl.BlockSpec((2, nout, s_blk, c), lambda p, j: (p, 0, j, 0))
    y = pl.pallas_call(
        functools.partial(_rows_inverse_kernel, s_mm=s_mm),
        grid=(npair, DFT_N2 // s_blk),
        in_specs=[dspec, dspec, pl.BlockSpec(kcat.shape, lambda p, j: (0, 0)),
                  uspec, uspec, pl.BlockSpec((1, 1, c), lambda p, j: (0, 0, 0))],
        out_specs=uspec,
        out_shape=jax.ShapeDtypeStruct((2 * npair, nout, DFT_N2, c), out_dtype),
        compiler_params=_cparams("parallel", "parallel"),
        name="hyena_idft_rows",
    )(dr, di, kcat, u.reshape(2 * npair, nout, DFT_N2, c), gate.reshape(2 * npair, nout, DFT_N2, c),
      bias.reshape(1, 1, c))
    return y.reshape(2 * npair, seq, c)


def _hyena(x1, x2, v, w1, b1, w2, b2, w3, freq, bias):
    b, seq, _ = v.shape
    assert b % 2 == 0, "batch elements are transformed in pairs"
    kt = _hyena_filters(seq, w1, b1, w2, b2, w3, freq)
    kr, ki = _spectrum(*_rows_real(kt, seq), seq)
    dr, di = _slab_conv(*_rows_pair(v, seq), kr, ki, 0, seq)
    z = _rows_inverse(dr, di, v, x1, bias[0], seq, F32)
    dr, di = _slab_conv(*_rows_pair(z, seq), kr, ki, 1, seq)
    return _rows_inverse(dr, di, z, x2, bias[1], seq, BF16)


def _mix_ffn_kernel(x_ref, hy_ref, sg_ref, at_ref, wo_ref, g1_ref, n2_ref, sc_ref, sh_ref, g2_ref, w1_ref,
                    w2_ref, o_ref, *, chunk):
    mix = _dot(hy_ref[...], wo_ref[:HY_WIDTH])
    mix = mix + _dot(sg_ref[...], wo_ref[HY_WIDTH:HY_WIDTH + SG_WIDTH])
    mix = mix + _dot(at_ref[...], wo_ref[HY_WIDTH + SG_WIDTH:])
    x = x_ref[...] + g1_ref[...] * mix
    hb = _modulated_norm(x, n2_ref[...], sc_ref[...], sh_ref[...]).astype(BF16)
    hidden = w2_ref.shape[0]
    acc = jnp.zeros(x.shape, F32)
    for c in range(hidden // chunk):
        a = _dot(hb, w1_ref[:, c * chunk:(c + 1) * chunk])
        up = _dot(hb, w1_ref[:, hidden + c * chunk:hidden + (c + 1) * chunk])
        act = (a * _sigmoid(a) * up).astype(BF16)
        acc = acc + _dot(act, w2_ref[c * chunk:(c + 1) * chunk, :])
    o_ref[...] = x + g2_ref[...] * acc


def _mix_ffn(x2d, seq, layer, hy, sg, att, w_out, g1, n2, sc, sh, g2, w1, w2):
    m, d = x2d.shape
    tm = min(FFN_TM, seq)
    per = seq // tm
    row = lambda i: (i, 0)
    fixed = lambda i: (0, 0)
    bat = lambda i: (i // per, 0, 0)
    resident = lambda a: pl.BlockSpec((None,) + a.shape[1:], lambda i: (layer, 0, 0), pipeline_mode=pl.Buffered(1))
    return pl.pallas_call(
        functools.partial(_mix_ffn_kernel, chunk=FFN_CHUNK),
        grid=(m // tm,),
        in_specs=[pl.BlockSpec((tm, d), row),
                  pl.BlockSpec((tm, HY_WIDTH), row),
                  pl.BlockSpec((tm, SG_WIDTH), row),
                  pl.BlockSpec((tm, DA_WIDTH), row),
                  resident(w_out),
                  pl.BlockSpec((None, 1, d), bat),
                  pl.BlockSpec((1, d), fixed),
                  pl.BlockSpec((None, 1, d), bat),
                  pl.BlockSpec((None, 1, d), bat),
                  pl.BlockSpec((None, 1, d), bat),
                  resident(w1),
                  resident(w2)],
        out_specs=pl.BlockSpec((tm, d), row),
        out_shape=jax.ShapeDtypeStruct((m, d), F32),
        compiler_params=_cparams("parallel"),
        name="outproj_ffn",
    )(x2d, hy, sg, att, w_out, g1, n2, sc, sh, g2, w1, w2)


def _rope_tables(seq):
    t = jnp.arange(seq, dtype=jnp.int32)
    half = DA_HEAD_DIM // 4
    inv_freq = ROPE_BASE ** (-jnp.arange(half, dtype=F32) / half)
    ang_row = (t // GRID_W).astype(F32)[:, None] * inv_freq[None, :]
    ang_col = (t % GRID_W).astype(F32)[:, None] * inv_freq[None, :]
    cos = jnp.concatenate([jnp.cos(ang_row)] * 2 + [jnp.cos(ang_col)] * 2, axis=-1)
    sin = jnp.concatenate([-jnp.sin(ang_row), jnp.sin(ang_row), -jnp.sin(ang_col), jnp.sin(ang_col)], axis=-1)
    return jnp.tile(cos, (1, 2)), jnp.tile(sin, (1, 2))


def kernel(x, c, ctx, c_ctx, norm1_g, norm2_g, ada_w, ada_b, w_in, hy_conv_w, hy_conv_b, hy_w1, hy_b1,
           hy_w2, hy_b2, hy_w3, hy_freq, hy_bias, sg_norm_g, sg_w, sg_b, qn_g, kn_g, lam_p, subln_g,
           w_out, ffn_w1, ffn_w2):
    batch, seq, d = x.shape
    ctx_len = ctx.shape[1]
    depth = w_in.shape[0]
    assert batch + 1 <= 8 and seq % CHUNK == 0 and ctx_len % CHUNK == 0

    cvecs = jnp.zeros((8, d), F32).at[:batch].set(c).at[batch].set(c_ctx)
    mods = _ada(cvecs, ada_w, ada_b[:, None, :])

    cos_l, sin_l = _rope_tables(seq)
    cos_c = jnp.ones((ctx_len, LANES), F32)
    sin_c = jnp.zeros((ctx_len, LANES), F32)
    seg_np = np.kron(np.eye(LANES // DA_HEAD_DIM), np.full((DA_HEAD_DIM, DA_HEAD_DIM), 1.0 / DA_HEAD_DIM))
    seg = jnp.asarray(np.concatenate([seg_np, seg_np], axis=0), F32).astype(BF16)

    w_in_b = w_in.astype(BF16)
    w_out_b = w_out.astype(BF16)
    w1_b = ffn_w1.astype(BF16)
    w2_b = ffn_w2.astype(BF16)
    xs = x.reshape(batch * seq, d)
    cs = ctx.reshape(batch * ctx_len, d)
    for i in range(depth):
        last = i == depth - 1
        lam_init = 0.8 - 0.6 * math.exp(-0.3 * i)
        mod_l = [m[:, None, :] for m in jnp.split(mods[i, :batch], 6, axis=-1)]
        mod_c = [jnp.broadcast_to(m[None], (batch, 1, d)) for m in jnp.split(mods[i, batch:batch + 1], 6, axis=-1)]
        qg = jnp.tile(qn_g[i], 2)[None, :]
        kg = jnp.tile(kn_g[i], 2)[None, :]
        sgw = sg_w[i].reshape(SG_HEADS * CHUNK, CHUNK).astype(BF16)
        sgb = jnp.repeat(sg_b[i].T, SG_WIDTH // SG_HEADS, axis=1)
        hy_params = (hy_w1[i], hy_b1[i], hy_w2[i], hy_b2[i], hy_w3[i], hy_freq[i], hy_bias[i])

        def project(tokens, n_tok, mod, cos, sin):
            outs = _inproj(tokens, n_tok, i, norm1_g[i][None, :], mod[1], mod[0], w_in_b, hy_conv_w[i],
                           hy_conv_b[i][None, :], cos, sin, qg, kg, seg, sg_norm_g[i][None, :], sgw, sgb)
            hy = [a.reshape(batch, n_tok, HY_WIDTH) for a in outs[:3]]
            return [hy, outs[3]] + [a.reshape(batch, n_tok, DA_WIDTH) for a in outs[4:]]

        def finish(tokens, n_tok, mod, hy_parts, sg, att):
            hy = _hyena(*hy_parts, *hy_params).reshape(batch * n_tok, HY_WIDTH)
            return _mix_ffn(tokens, n_tok, i, hy, sg, att.reshape(batch * n_tok, DA_WIDTH), w_out_b, mod[2],
                            norm2_g[i][None, :], mod[4], mod[3], mod[5], w1_b, w2_b)

        hy_l, sg_l, q_l, k_l, v_l = project(xs, seq, mod_l, cos_l, sin_l)
        hy_c, sg_c, q_c, k_c, v_c = project(cs, ctx_len, mod_c, cos_c, sin_c)
        attend = functools.partial(_attention, lam_p=lam_p[i], subln_g=subln_g[i][None, :], lam_init=lam_init,
                                   qn_g=qn_g[i], kn_g=kn_g[i])
        att_l = attend(q_l, [(k_l, v_l), (k_c, v_c)])
        xs = finish(xs, seq, mod_l, hy_l, sg_l, att_l)
        if not last:
            cs = finish(cs, ctx_len, mod_c, hy_c, sg_c, attend(q_c, [(k_c, v_c)]))
    return xs.reshape(batch, seq, d)
```

```python
import functools
import math

import numpy as np
import jax
import jax.numpy as jnp
from jax import lax
from jax.experimental import pallas as pl
from jax.experimental.pallas import tpu as pltpu

F32 = jnp.float32
BF16 = jnp.bfloat16

GRID_W = 64
EPS = 1e-6
HY_WIDTH = 256
HY_ORDER = 2
HY_POS_BANDS = 16
HY_DECAY_TARGET = 1e-2
HY_FAST_DECAY_PCT = 0.3
HY_SLOW_DECAY_PCT = 1.5
SG_HEADS = 4
SG_WIDTH = 256
CHUNK = 128
DA_HEADS = 4
DA_WIDTH = 512
DA_V_DIM = 128
DA_HEAD_DIM = 64
ROPE_BASE = 10000.0
HY_IN = 3 * HY_WIDTH
SG_IN = 2 * SG_WIDTH
LANES = 128
DFT_N2 = 128
SLABS_PER_STEP = 16
VMEM_LIMIT = 56 * 1024 * 1024
NEG_BIG = -1e30
Q_SCALE = math.log2(math.e) * DA_HEAD_DIM ** -0.5
INPROJ_TM = 1024
FFN_TM = 512
FFN_CHUNK = 256
ATTN_TQ = 1024
ATTN_TK = 256


def _cparams(*sem):
    return pltpu.CompilerParams(dimension_semantics=sem, vmem_limit_bytes=VMEM_LIMIT)


def _dot(a, b):
    return jnp.dot(a, b, preferred_element_type=F32)


def _sigmoid(x):
    return 1.0 / (1.0 + jnp.exp(-x))


def _modulated_norm(x, g, sc, sh):
    y = x * lax.rsqrt(jnp.mean(x * x, axis=-1, keepdims=True) + EPS)
    return (y * g) * (1.0 + sc) + sh


def _ada_kernel(c_ref, w_ref, b_ref, o_ref):
    cv = c_ref[...]
    s = (cv * _sigmoid(cv)).astype(BF16)
    o_ref[...] = _dot(s, w_ref[...].astype(BF16)) + b_ref[...]


def _ada(cvecs, ada_w, ada_b):
    depth, d, n = ada_w.shape
    tn = 1536
    return pl.pallas_call(
        _ada_kernel,
        grid=(depth, n // tn),
        in_specs=[pl.BlockSpec((8, d), lambda i, j: (0, 0)),
                  pl.BlockSpec((None, d, tn), lambda i, j: (i, 0, j)),
                  pl.BlockSpec((None, 1, tn), lambda i, j: (i, 0, j))],
        out_specs=pl.BlockSpec((None, 8, tn), lambda i, j: (i, 0, j)),
        out_shape=jax.ShapeDtypeStruct((depth, 8, n), F32),
        compiler_params=_cparams("parallel", "parallel"),
        name="adaln",
    )(cvecs, ada_w, ada_b)


def _group_mean_sq(t, seg):
    sq = t * t
    hi = sq.astype(BF16)
    lo = (sq - hi.astype(F32)).astype(BF16)
    return _dot(jnp.concatenate([hi, lo], axis=1), seg)


def _norm_rope(t, gain, seg, cos, sin, swap_fwd):
    tn = t * lax.rsqrt(_group_mean_sq(t, seg) + EPS) * gain
    rot = jnp.where(swap_fwd, pltpu.roll(tn, LANES - 16, 1), pltpu.roll(tn, 16, 1))
    return tn * cos + rot * sin


def _inproj_kernel(x_ref, xp_ref, xn_ref, g_ref, sc_ref, sh_ref, w_ref, cw_ref, cb_ref, cos_ref, sin_ref, qg_ref,
                   kg_ref, seg_ref, sgg_ref, sgw_ref, sgb_ref, hx1_ref, hx2_ref, hxv_ref, sg_ref, q_ref, k_ref, v_ref,
                   *, per):
    tm = x_ref.shape[0]
    hb = _modulated_norm(x_ref[...], g_ref[...], sc_ref[...], sh_ref[...]).astype(BF16)

    halo = jnp.concatenate([xp_ref[...], xn_ref[...]], axis=0)
    hh = _modulated_norm(halo, g_ref[...], sc_ref[...], sh_ref[...]).astype(BF16)
    p_ext = _dot(jnp.concatenate([hb, hh], axis=0), w_ref[:, :HY_IN])
    pos = pl.program_id(0) % per
    before = jnp.where(pos == 0, 0.0, p_ext[tm + 7:tm + 8])
    after = jnp.where(pos == per - 1, 0.0, p_ext[tm + 8:tm + 9])
    rid = lax.broadcasted_iota(jnp.int32, (tm, 1), 0)
    for part, hx_ref in enumerate((hx1_ref, hx2_ref, hxv_ref)):
        cols = slice(part * HY_WIDTH, (part + 1) * HY_WIDTH)
        cur = p_ext[:tm, cols]
        prev = jnp.where(rid == 0, before[:, cols], pltpu.roll(cur, 1, 0))
        nxt = jnp.where(rid == tm - 1, after[:, cols], pltpu.roll(cur, tm - 1, 0))
        hx_ref[...] = prev * cw_ref[0:1, cols] + cur * cw_ref[1:2, cols] + nxt * cw_ref[2:3, cols] + cb_ref[:, cols]

    psg = _dot(hb, w_ref[:, HY_IN:HY_IN + SG_IN])
    ge = 0.5 * psg * (1.0 + lax.erf(psg * np.float32(math.sqrt(0.5))))
    u = ge[:, :SG_WIDTH]
    vv = ge[:, SG_WIDTH:]
    vn = (vv * lax.rsqrt(jnp.mean(vv * vv, axis=-1, keepdims=True) + EPS) * sgg_ref[...]).astype(BF16)
    head_of_lane = lax.broadcasted_iota(jnp.int32, (CHUNK, SG_WIDTH), 1) // (SG_WIDTH // SG_HEADS)
    for c in range(tm // CHUNK):
        rows = slice(c * CHUNK, (c + 1) * CHUNK)
        r = _dot(sgw_ref[...], vn[rows])
        mixed = sgb_ref[...]
        for h in range(SG_HEADS):
            mixed = mixed + jnp.where(head_of_lane == h, r[h * CHUNK:(h + 1) * CHUNK], 0.0)
        sg_ref[rows, :] = (u[rows] * mixed).astype(BF16)

    off = HY_IN + SG_IN
    lane = lax.broadcasted_iota(jnp.int32, (1, LANES), 1)
    swap_fwd = (lane % 32) < 16
    cos = cos_ref[...]
    sin = sin_ref[...]
    seg = seg_ref[...]
    pq = _dot(hb, w_ref[:, off:off + DA_WIDTH])
    pk = _dot(hb, w_ref[:, off + DA_WIDTH:off + 2 * DA_WIDTH])
    v_ref[...] = _dot(hb, w_ref[:, off + 2 * DA_WIDTH:]).astype(BF16)
    for h in range(DA_HEADS):
        cols = slice(h * LANES, (h + 1) * LANES)
        q = _norm_rope(pq[:, cols], qg_ref[...], seg, cos, sin, swap_fwd)
        q_ref[:, cols] = (q * Q_SCALE).astype(BF16)
        k_ref[:, cols] = _norm_rope(pk[:, cols], kg_ref[...], seg, cos, sin, swap_fwd).astype(BF16)


def _inproj(x2d, seq, layer, g, sc, sh, w, conv_w, conv_b, cos, sin, qg, kg, seg, sgg, sgw, sgb):
    m, d = x2d.shape
    n = w.shape[2]
    tm = min(INPROJ_TM, seq)
    per = seq // tm
    halo = 8
    row = lambda i: (i, 0)
    fixed = lambda i: (0, 0)
    bat = lambda i: (i // per, 0, 0)
    pos = lambda i: (i % per, 0)
    outs = [(HY_WIDTH, F32)] * 3 + [(SG_WIDTH, BF16)] + [(DA_WIDTH, BF16)] * 3
    return pl.pallas_call(
        functools.partial(_inproj_kernel, per=per),
        grid=(m // tm,),
        in_specs=[pl.BlockSpec((tm, d), row),
                  pl.BlockSpec((halo, d), lambda i: (jnp.maximum(i * (tm // halo) - 1, 0), 0)),
                  pl.BlockSpec((halo, d), lambda i: (jnp.minimum((i + 1) * (tm // halo), m // halo - 1), 0)),
                  pl.BlockSpec((1, d), fixed),
                  pl.BlockSpec((None, 1, d), bat),
                  pl.BlockSpec((None, 1, d), bat),
                  pl.BlockSpec((None, d, n), lambda i: (layer, 0, 0)),
                  pl.BlockSpec((3, HY_IN), fixed),
                  pl.BlockSpec((1, HY_IN), fixed),
                  pl.BlockSpec((tm, LANES), pos),
                  pl.BlockSpec((tm, LANES), pos),
                  pl.BlockSpec((1, LANES), fixed),
                  pl.BlockSpec((1, LANES), fixed),
                  pl.BlockSpec((2 * LANES, LANES), fixed),
                  pl.BlockSpec((1, SG_WIDTH), fixed),
                  pl.BlockSpec((SG_HEADS * CHUNK, CHUNK), fixed),
                  pl.BlockSpec((CHUNK, SG_WIDTH), fixed)],
        out_specs=[pl.BlockSpec((tm, c), row) for c, _ in outs],
        out_shape=[jax.ShapeDtypeStruct((m, c), dt) for c, dt in outs],
        compiler_params=_cparams("parallel"),
        name="inproj",
    )(x2d, x2d, x2d, g, sc, sh, w, conv_w, conv_b, cos, sin, qg, kg, seg, sgg, sgw, sgb)


_NT = (((1,), (1,)), ((), ()))


def _lambda(lp_ref, lam_init):
    lp = lp_ref[...]
    return (jnp.exp(jnp.sum(lp[0:1] * lp[1:2], axis=-1, keepdims=True))
            - jnp.exp(jnp.sum(lp[2:3] * lp[3:4], axis=-1, keepdims=True)) + lam_init)


def _stacked_components(q):
    lane = lax.broadcasted_iota(jnp.int32, (1, LANES), 1)
    zero = jnp.zeros_like(q)
    return jnp.concatenate([jnp.where(lane < DA_HEAD_DIM, q, zero), jnp.where(lane >= DA_HEAD_DIM, q, zero)],
                           axis=0)


def _attn_finish(o1, o2, lam, g_ref, o_ref, lam_init):
    o = o1 - lam * o2
    o = o * lax.rsqrt(jnp.mean(o * o, axis=-1, keepdims=True) + EPS) * g_ref[...]
    o_ref[...] = (o * (1.0 - lam_init)).astype(BF16)


def _attn_bounded_kernel(lp_ref, g_ref, q_ref, *refs, tiles, lam_init):
    tq = q_ref.shape[0]
    o_ref = refs[-1]
    qt = q_ref[...].T
    dim = lax.broadcasted_iota(jnp.int32, (LANES, 1), 0)
    zero = jnp.zeros_like(qt)
    qst = jnp.concatenate([jnp.where(dim < DA_HEAD_DIM, qt, zero), jnp.where(dim >= DA_HEAD_DIM, qt, zero)],
                          axis=1)
    acc = jnp.zeros((DA_V_DIM + 16, 2 * tq), F32)
    for j, tk in enumerate(tiles):
        k_ref, v_ref = refs[2 * j], refs[2 * j + 1]
        ones = jnp.ones((16, tk), BF16)
        for i in range(k_ref.shape[0] // tk):
            st = _dot(k_ref[i * tk:(i + 1) * tk, :], qst)
            vt = jnp.concatenate([v_ref[i * tk:(i + 1) * tk, :].T, ones], axis=0)
            acc = acc + _dot(vt, jnp.exp2(st).astype(BF16))
    ot = acc[:DA_V_DIM] / acc[DA_V_DIM:DA_V_DIM + 1]
    dt = ot[:, :tq] - _lambda(lp_ref, lam_init) * ot[:, tq:]
    dt = dt * lax.rsqrt(jnp.mean(dt * dt, axis=0, keepdims=True) + EPS)
    o_ref[...] = (dt.T * g_ref[...] * (1.0 - lam_init)).astype(BF16)


def _attn_online_kernel(lp_ref, g_ref, q_ref, *refs, tiles, lam_init):
    tq = q_ref.shape[0]
    o_ref = refs[-1]
    qs = _stacked_components(q_ref[...])
    carry = (jnp.full((2 * tq, 1), NEG_BIG, F32), jnp.zeros((2 * tq, 1), F32),
             jnp.zeros((2 * tq, DA_V_DIM), F32))
    for j, tk in enumerate(tiles):
        k_ref, v_ref = refs[2 * j], refs[2 * j + 1]

        def body(i, carry, k_ref=k_ref, v_ref=v_ref, tk=tk):
            m, l, acc = carry
            start = pl.multiple_of(i * tk, tk)
            kb = k_ref[pl.ds(start, tk), :]
            vb = v_ref[pl.ds(start, tk), :]
            s = lax.dot_general(qs, kb, _NT, preferred_element_type=F32)
            m_new = jnp.maximum(m, jnp.max(s, axis=-1, keepdims=True))
            alpha = jnp.exp2(m - m_new)
            p = jnp.exp2(s - m_new)
            l_new = alpha * l + jnp.sum(p, axis=-1, keepdims=True)
            return m_new, l_new, alpha * acc + _dot(p.astype(BF16), vb)

        carry = lax.fori_loop(0, k_ref.shape[0] // tk, body, carry)
    _, l, acc = carry
    o = acc / l
    _attn_finish(o[:tq], o[tq:], _lambda(lp_ref, lam_init), g_ref, o_ref, lam_init)


def _pick_tile(n, candidates):
    for c in candidates:
        if n % c == 0:
            return c
    raise ValueError(f"no tile for {n}")


BOUNDED_SCORE_LIMIT = 56.0


def _attention(q, kv, lam_p, subln_g, lam_init, qn_g, kn_g):
    bound = (math.sqrt(DA_HEAD_DIM) * math.log2(math.e) * 1.02) * jnp.max(jnp.abs(qn_g)) * jnp.max(jnp.abs(kn_g))
    flat = [a for pair in kv for a in pair]
    return lax.cond(bound <= BOUNDED_SCORE_LIMIT,
                    functools.partial(_attention_call, _attn_bounded_kernel, lam_init),
                    functools.partial(_attention_call, _attn_online_kernel, lam_init),
                    q, lam_p, subln_g, *flat)


def _attention_call(body, lam_init, q, lam_p, subln_g, *kv):
    b, lq, _ = q.shape
    tq = _pick_tile(lq, (ATTN_TQ, 512, 256, 128))
    tiles = tuple(_pick_tile(a.shape[1], (ATTN_TK, 512, 256, 128)) for a in kv[::2])
    whole = lambda a: pl.BlockSpec((None, a.shape[1], LANES), lambda bi, h, i: (bi, 0, h))
    return pl.pallas_call(
        functools.partial(body, tiles=tiles, lam_init=lam_init),
        grid=(b, DA_HEADS, lq // tq),
        in_specs=[pl.BlockSpec((4, DA_HEAD_DIM), lambda bi, h, i: (0, 0)),
                  pl.BlockSpec((1, DA_V_DIM), lambda bi, h, i: (0, 0)),
                  pl.BlockSpec((None, tq, LANES), lambda bi, h, i: (bi, i, h))] + [whole(a) for a in kv],
        out_specs=pl.BlockSpec((None, tq, LANES), lambda bi, h, i: (bi, i, h)),
        out_shape=jax.ShapeDtypeStruct((b, lq, DA_WIDTH), BF16),
        compiler_params=_cparams("parallel", "parallel", "parallel"),
        name="diff_attention",
    )(lam_p, subln_g, q, *kv)


def _filter_kernel(f_ref, w1_ref, b1_ref, w2_ref, b2_ref, w3_ref, fr_ref, dl_ref, o_ref, *, seq):
    tl = f_ref.shape[0]
    half = LANES // 2
    nout = HY_ORDER * HY_WIDTH
    feats = f_ref[...]
    h = jnp.sin(fr_ref[0:1, :] * (_dot(feats.astype(BF16), w1_ref[...]) + b1_ref[...]))
    h = jnp.sin(fr_ref[1:2, :] * (_dot(h.astype(BF16), w2_ref[...]) + b2_ref[...]))
    h = _dot(h.astype(BF16), w3_ref[...])
    row = pl.program_id(0) * tl + lax.broadcasted_iota(jnp.int32, (tl, 1), 0)
    o_ref[0] = h[:, :nout] * jnp.exp(-feats[:, 0:1] * dl_ref[...])
    bwd = h[:, nout:] * jnp.exp(-feats[:, half:half + 1] * dl_ref[...])
    o_ref[1] = jnp.where(row == 0, 0.0, bwd)


def _hyena_features(seq):
    t = jnp.linspace(0.0, 1.0, seq, dtype=F32)[:, None]
    bands = jnp.linspace(1e-4, HY_POS_BANDS - 1, HY_POS_BANDS, dtype=F32)
    ang = (2.0 * math.pi / seq) * jnp.arange(seq, dtype=F32)[:, None] * bands[None, :]
    feats = jnp.concatenate([t, jnp.cos(ang), -jnp.sin(ang)], axis=-1)
    back = jnp.concatenate([feats[:1], feats[:0:-1]], axis=0)
    pad = ((0, 0), (0, LANES // 2 - feats.shape[1]))
    return jnp.concatenate([jnp.pad(feats, pad), jnp.pad(back, pad)], axis=1)


def _hyena_filters(seq, w1, b1, w2, b2, w3, freq):
    feats = _hyena_features(seq)
    nfeat, hid = w1.shape
    half = LANES // 2
    nout = HY_ORDER * HY_WIDTH
    assert nfeat <= half and hid <= half
    w1p = jnp.zeros((LANES, LANES), F32).at[:nfeat, :hid].set(w1).at[half:half + nfeat, half:half + hid].set(w1)
    w2p = jnp.zeros((LANES, LANES), F32).at[:hid, :hid].set(w2).at[half:half + hid, half:half + hid].set(w2)
    w3d = w3.reshape(hid, HY_ORDER, 2, HY_WIDTH)
    w3p = (jnp.zeros((LANES, 2 * nout), F32)
           .at[:hid, :nout].set(w3d[:, :, 0].reshape(hid, nout))
           .at[half:half + hid, nout:].set(w3d[:, :, 1].reshape(hid, nout)))
    both = lambda a: jnp.tile(jnp.pad(a.reshape(-1, hid), ((0, 0), (0, half - hid))), (1, 2))
    min_decay = math.log(HY_DECAY_TARGET) / HY_SLOW_DECAY_PCT
    max_decay = math.log(HY_DECAY_TARGET) / HY_FAST_DECAY_PCT
    deltas = jnp.abs(jnp.linspace(min_decay, max_decay, HY_WIDTH, dtype=F32))[None, :]
    tl = min(512, seq)
    fixed = lambda i: (0, 0)
    kt = pl.pallas_call(
        functools.partial(_filter_kernel, seq=seq),
        grid=(seq // tl,),
        in_specs=[pl.BlockSpec((tl, LANES), lambda i: (i, 0)),
                  pl.BlockSpec((LANES, LANES), fixed), pl.BlockSpec((1, LANES), fixed),
                  pl.BlockSpec((LANES, LANES), fixed), pl.BlockSpec((1, LANES), fixed),
                  pl.BlockSpec((LANES, 2 * nout), fixed), pl.BlockSpec((2, LANES), fixed),
                  pl.BlockSpec((1, nout), fixed)],
        out_specs=pl.BlockSpec((2, tl, nout), lambda i: (0, i, 0)),
        out_shape=jax.ShapeDtypeStruct((2, seq, nout), F32),
        compiler_params=_cparams("parallel"),
        name="hyena_filter_mlp",
    )(feats, w1p.astype(BF16), both(b1), w2p.astype(BF16), both(b2), w3p.astype(BF16), both(freq),
      jnp.tile(deltas, (1, HY_ORDER)))
    return kt.reshape(2 * seq, nout)


def _dft_tables(seq):
    n = 2 * seq
    n1 = n // DFT_N2
    idx1 = np.arange(n1)
    idx2 = np.arange(DFT_N2)
    f1 = np.exp(-2j * np.pi * np.outer(idx1, idx1) / n1)
    k = idx1[:, None, None] + n1 * idx2[None, :, None]
    g = np.exp(-2j * np.pi * (k * idx2[None, None, :] % n) / n)
    return n1, f1, g


def _rows_block(nin):
    s_mm = max(8, LANES // nin)
    return s_mm, max(16, s_mm)


def _kron_rows(mat, s):
    return jnp.asarray(np.kron(mat, np.eye(s)), F32).astype(BF16)


def _rows_real_kernel(u_ref, kr_ref, ki_ref, ar_ref, ai_ref, *, s_mm):
    nin, s_blk, c = u_ref.shape
    n1 = ar_ref.shape[0]
    re, im = [], []
    for h in range(s_blk // s_mm):
        u = u_ref[:, h * s_mm:(h + 1) * s_mm, :].reshape(nin * s_mm, c).astype(BF16)
        re.append(_dot(kr_ref[...], u).reshape(n1, s_mm, c))
        im.append(_dot(ki_ref[...], u).reshape(n1, s_mm, c))
    ar_ref[...] = jnp.concatenate(re, axis=1).astype(BF16)
    ai_ref[...] = jnp.concatenate(im, axis=1).astype(BF16)


def _rows_real(u, seq):
    rows, c = u.shape
    n1, f1, _ = _dft_tables(seq)
    nin = rows // DFT_N2
    s_mm, s_blk = _rows_block(nin)
    fixed = lambda j: (0, 0)
    out_spec = pl.BlockSpec((n1, s_blk, c), lambda j: (0, j, 0))
    out_shape = jax.ShapeDtypeStruct((n1, DFT_N2, c), BF16)
    return pl.pallas_call(
        functools.partial(_rows_real_kernel, s_mm=s_mm),
        grid=(DFT_N2 // s_blk,),
        in_specs=[pl.BlockSpec((nin, s_blk, c), lambda j: (0, j, 0)),
                  pl.BlockSpec((n1 * s_mm, nin * s_mm), fixed),
                  pl.BlockSpec((n1 * s_mm, nin * s_mm), fixed)],
        out_specs=[out_spec, out_spec],
        out_shape=[out_shape, out_shape],
        compiler_params=_cparams("parallel"),
        name="hyena_filter_dft_rows",
    )(u.reshape(nin, DFT_N2, c), _kron_rows(f1.real[:, :nin], s_mm), _kron_rows(f1.imag[:, :nin], s_mm))


def _rows_pair_kernel(u_ref, kc_ref, ar_ref, ai_ref, *, s_mm):
    _, nin, s_blk, c = u_ref.shape
    m = kc_ref.shape[0] // 2
    re, im = [], []
    for h in range(s_blk // s_mm):
        sub = slice(h * s_mm, (h + 1) * s_mm)
        p = _dot(kc_ref[...], u_ref[0, :, sub, :].reshape(nin * s_mm, c).astype(BF16))
        q = _dot(kc_ref[...], u_ref[1, :, sub, :].reshape(nin * s_mm, c).astype(BF16))
        re.append((p[:m] - q[m:]).reshape(m // s_mm, s_mm, c))
        im.append((q[:m] + p[m:]).reshape(m // s_mm, s_mm, c))
    ar_ref[...] = jnp.concatenate(re, axis=1).astype(BF16)
    ai_ref[...] = jnp.concatenate(im, axis=1).astype(BF16)


def _rows_pair(u, seq):
    b, rows, c = u.shape
    n1, f1, _ = _dft_tables(seq)
    nin = rows // DFT_N2
    s_mm, s_blk = _rows_block(nin)
    kcat = jnp.concatenate([_kron_rows(f1.real[:, :nin], s_mm), _kron_rows(f1.imag[:, :nin], s_mm)], axis=0)
    out_spec = pl.BlockSpec((None, n1, s_blk, c), lambda p, j: (p, 0, j, 0))
    out_shape = jax.ShapeDtypeStruct((b // 2, n1, DFT_N2, c), BF16)
    return pl.pallas_call(
        functools.partial(_rows_pair_kernel, s_mm=s_mm),
        grid=(b // 2, DFT_N2 // s_blk),
        in_specs=[pl.BlockSpec((2, nin, s_blk, c), lambda p, j: (p, 0, j, 0)),
                  pl.BlockSpec(kcat.shape, lambda p, j: (0, 0))],
        out_specs=[out_spec, out_spec],
        out_shape=[out_shape, out_shape],
        compiler_params=_cparams("parallel", "parallel"),
        name="hyena_dft_rows",
    )(u.reshape(b, nin, DFT_N2, c), kcat)


def _complex_apply(mcat, xr, xi):
    p = _dot(mcat, xr)
    q = _dot(mcat, xi)
    return p[:DFT_N2] - q[DFT_N2:], q[:DFT_N2] + p[DFT_N2:]


def _spectrum_kernel(ar_ref, ai_ref, g_ref, xr_ref, xi_ref, *, scale):
    for j in range(ar_ref.shape[0]):
        xr, xi = _complex_apply(g_ref[j], ar_ref[j], ai_ref[j])
        xr_ref[j] = (xr * scale).astype(BF16)
        xi_ref[j] = (xi * scale).astype(BF16)


def _slab_conv_kernel(ar_ref, ai_ref, kr_ref, ki_ref, g_ref, h_ref, dr_ref, di_ref):
    for j in range(ar_ref.shape[0]):
        xr, xi = _complex_apply(g_ref[j], ar_ref[j], ai_ref[j])
        kr = kr_ref[j].astype(F32)
        ki = ki_ref[j].astype(F32)
        zr = (xr * kr - xi * ki).astype(BF16)
        zi = (xr * ki + xi * kr).astype(BF16)
        dr, di = _complex_apply(h_ref[j], zr, zi)
        dr_ref[j] = dr.astype(BF16)
        di_ref[j] = di.astype(BF16)


def _slab_consts(seq):
    n1, _, g = _dft_tables(seq)
    gcat = jnp.asarray(np.concatenate([g.real, g.imag], axis=1), F32).astype(BF16)
    gt = np.conj(g).transpose(0, 2, 1)
    hcat = jnp.asarray(np.concatenate([gt.real, gt.imag], axis=1), F32).astype(BF16)
    return n1, gcat, hcat


def _spectrum(ar, ai, seq):
    n1, gcat, _ = _slab_consts(seq)
    c = ar.shape[-1]
    g = min(SLABS_PER_STEP, n1)
    slab = pl.BlockSpec((g, DFT_N2, c), lambda i: (i, 0, 0))
    return pl.pallas_call(
        functools.partial(_spectrum_kernel, scale=1.0 / (2 * seq)),
        grid=(n1 // g,),
        in_specs=[slab, slab, pl.BlockSpec((g, 2 * DFT_N2, DFT_N2), lambda i: (i, 0, 0))],
        out_specs=[slab, slab],
        out_shape=[jax.ShapeDtypeStruct((n1, DFT_N2, c), BF16)] * 2,
        compiler_params=_cparams("parallel"),
        name="hyena_filter_spectrum",
    )(ar, ai, gcat)


def _slab_conv(ar, ai, kr, ki, order, seq):
    n1, gcat, hcat = _slab_consts(seq)
    npair, _, _, c = ar.shape
    g = min(SLABS_PER_STEP, n1)
    slab = pl.BlockSpec((None, g, DFT_N2, c), lambda p, i: (p, i, 0, 0))
    filt = pl.BlockSpec((g, DFT_N2, c), lambda p, i: (i, 0, order))
    mats = pl.BlockSpec((g, 2 * DFT_N2, DFT_N2), lambda p, i: (i, 0, 0))
    return pl.pallas_call(
        _slab_conv_kernel,
        grid=(npair, n1 // g),
        in_specs=[slab, slab, filt, filt, mats, mats],
        out_specs=[slab, slab],
        out_shape=[jax.ShapeDtypeStruct(ar.shape, BF16)] * 2,
        compiler_params=_cparams("parallel", "parallel"),
        name="hyena_dft_slabs",
    )(ar, ai, kr, ki, gcat, hcat)


def _rows_inverse_kernel(dr_ref, di_ref, kc_ref, u_ref, gate_ref, bias_ref, o_ref, *, s_mm):
    n1, s_blk, c = dr_ref.shape
    m = kc_ref.shape[0] // 2
    dr = dr_ref[...].astype(F32)
    di = di_ref[...].astype(F32)
    y0, y1 = [], []
    for h in range(s_blk // s_mm):
        sub = slice(h * s_mm, (h + 1) * s_mm)
        p = _dot(kc_ref[...], dr[:, sub, :].reshape(n1 * s_mm, c).astype(BF16))
        q = _dot(kc_ref[...], di[:, sub, :].reshape(n1 * s_mm, c).astype(BF16))
        y0.append((p[:m] + q[m:]).reshape(m // s_mm, s_mm, c))
        y1.append((q[:m] - p[m:]).reshape(m // s_mm, s_mm, c))
    for e, parts in enumerate((y0, y1)):
        y = jnp.concatenate(parts, axis=1)
        o_ref[e] = (gate_ref[e] * (y + u_ref[e] * bias_ref[...])).astype(o_ref.dtype)


def _rows_inverse(dr, di, u, gate, bias, seq, out_dtype):
    npair, n1, _, c = dr.shape
    _, f1, _ = _dft_tables(seq)
    nout = seq // DFT_N2
    s_mm, s_blk = _rows_block(nout)
    kcat = jnp.concatenate([_kron_rows(f1.real[:nout], s_mm), _kron_rows(f1.imag[:nout], s_mm)], axis=0)
    dspec = pl.BlockSpec((None, n1, s_blk, c), lambda p, j: (p, 0, j, 0))
    uspec = pl.BlockSpec((2, nout, s_blk, c), lambda p, j: (p, 0, j, 0))
    y = pl.pallas_call(
        functools.partial(_rows_inverse_kernel, s_mm=s_mm),
        grid=(npair, DFT_N2 // s_blk),
        in_specs=[dspec, dspec, pl.BlockSpec(kcat.shape, lambda p, j: (0, 0)),
                  uspec, uspec, pl.BlockSpec((1, 1, c), lambda p, j: (0, 0, 0))],
        out_specs=uspec,
        out_shape=jax.ShapeDtypeStruct((2 * npair, nout, DFT_N2, c), out_dtype),
        compiler_params=_cparams("parallel", "parallel"),
        name="hyena_idft_rows",
    )(dr, di, kcat, u.reshape(2 * npair, nout, DFT_N2, c), gate.reshape(2 * npair, nout, DFT_N2, c),
      bias.reshape(1, 1, c))
    return y.reshape(2 * npair, seq, c)


def _hyena(x1, x2, v, w1, b1, w2, b2, w3, freq, bias):
    b, seq, _ = v.shape
    assert b % 2 == 0, "batch elements are transformed in pairs"
    kt = _hyena_filters(seq, w1, b1, w2, b2, w3, freq)
    kr, ki = _spectrum(*_rows_real(kt, seq), seq)
    dr, di = _slab_conv(*_rows_pair(v, seq), kr, ki, 0, seq)
    z = _rows_inverse(dr, di, v, x1, bias[0], seq, F32)
    dr, di = _slab_conv(*_rows_pair(z, seq), kr, ki, 1, seq)
    return _rows_inverse(dr, di, z, x2, bias[1], seq, BF16)


def _mix_ffn_kernel(x_ref, hy_ref, sg_ref, at_ref, wo_ref, g1_ref, n2_ref, sc_ref, sh_ref, g2_ref, w1_ref,
                    w2_ref, o_ref, *, chunk):
    mix = _dot(hy_ref[...], wo_ref[:HY_WIDTH])
    mix = mix + _dot(sg_ref[...], wo_ref[HY_WIDTH:HY_WIDTH + SG_WIDTH])
    mix = mix + _dot(at_ref[...], wo_ref[HY_WIDTH + SG_WIDTH:])
    x = x_ref[...] + g1_ref[...] * mix
    hb = _modulated_norm(x, n2_ref[...], sc_ref[...], sh_ref[...]).astype(BF16)
    hidden = w2_ref.shape[0]
    acc = jnp.zeros(x.shape, F32)
    for c in range(hidden // chunk):
        a = _dot(hb, w1_ref[:, c * chunk:(c + 1) * chunk])
        up = _dot(hb, w1_ref[:, hidden + c * chunk:hidden + (c + 1) * chunk])
        act = (a * _sigmoid(a) * up).astype(BF16)
        acc = acc + _dot(act, w2_ref[c * chunk:(c + 1) * chunk, :])
    o_ref[...] = x + g2_ref[...] * acc


def _mix_ffn(x2d, seq, layer, hy, sg, att, w_out, g1, n2, sc, sh, g2, w1, w2):
    m, d = x2d.shape
    tm = min(FFN_TM, seq)
    per = seq // tm
    row = lambda i: (i, 0)
    fixed = lambda i: (0, 0)
    bat = lambda i: (i // per, 0, 0)
    resident = lambda a: pl.BlockSpec((None,) + a.shape[1:], lambda i: (layer, 0, 0), pipeline_mode=pl.Buffered(1))
    return pl.pallas_call(
        functools.partial(_mix_ffn_kernel, chunk=FFN_CHUNK),
        grid=(m // tm,),
        in_specs=[pl.BlockSpec((tm, d), row),
                  pl.BlockSpec((tm, HY_WIDTH), row),
                  pl.BlockSpec((tm, SG_WIDTH), row),
                  pl.BlockSpec((tm, DA_WIDTH), row),
                  resident(w_out),
                  pl.BlockSpec((None, 1, d), bat),
                  pl.BlockSpec((1, d), fixed),
                  pl.BlockSpec((None, 1, d), bat),
                  pl.BlockSpec((None, 1, d), bat),
                  pl.BlockSpec((None, 1, d), bat),
                  resident(w1),
                  resident(w2)],
        out_specs=pl.BlockSpec((tm, d), row),
        out_shape=jax.ShapeDtypeStruct((m, d), F32),
        compiler_params=_cparams("parallel"),
        name="outproj_ffn",
    )(x2d, hy, sg, att, w_out, g1, n2, sc, sh, g2, w1, w2)


def _rope_tables(seq):
    t = jnp.arange(seq, dtype=jnp.int32)
    half = DA_HEAD_DIM // 4
    inv_freq = ROPE_BASE ** (-jnp.arange(half, dtype=F32) / half)
    ang_row = (t // GRID_W).astype(F32)[:, None] * inv_freq[None, :]
    ang_col = (t % GRID_W).astype(F32)[:, None] * inv_freq[None, :]
    cos = jnp.concatenate([jnp.cos(ang_row)] * 2 + [jnp.cos(ang_col)] * 2, axis=-1)
    sin = jnp.concatenate([-jnp.sin(ang_row), jnp.sin(ang_row), -jnp.sin(ang_col), jnp.sin(ang_col)], axis=-1)
    return jnp.tile(cos, (1, 2)), jnp.tile(sin, (1, 2))


def kernel(x, c, ctx, c_ctx, norm1_g, norm2_g, ada_w, ada_b, w_in, hy_conv_w, hy_conv_b, hy_w1, hy_b1,
           hy_w2, hy_b2, hy_w3, hy_freq, hy_bias, sg_norm_g, sg_w, sg_b, qn_g, kn_g, lam_p, subln_g,
           w_out, ffn_w1, ffn_w2):
    batch, seq, d = x.shape
    ctx_len = ctx.shape[1]
    depth = w_in.shape[0]
    assert batch + 1 <= 8 and seq % CHUNK == 0 and ctx_len % CHUNK == 0

    cvecs = jnp.zeros((8, d), F32).at[:batch].set(c).at[batch].set(c_ctx)
    mods = _ada(cvecs, ada_w, ada_b[:, None, :])

    cos_l, sin_l = _rope_tables(seq)
    cos_c = jnp.ones((ctx_len, LANES), F32)
    sin_c = jnp.zeros((ctx_len, LANES), F32)
    seg_np = np.kron(np.eye(LANES // DA_HEAD_DIM), np.full((DA_HEAD_DIM, DA_HEAD_DIM), 1.0 / DA_HEAD_DIM))
    seg = jnp.asarray(np.concatenate([seg_np, seg_np], axis=0), F32).astype(BF16)

    w_in_b = w_in.astype(BF16)
    w_out_b = w_out.astype(BF16)
    w1_b = ffn_w1.astype(BF16)
    w2_b = ffn_w2.astype(BF16)
    xs = x.reshape(batch * seq, d)
    cs = ctx.reshape(batch * ctx_len, d)
    for i in range(depth):
        last = i == depth - 1
        lam_init = 0.8 - 0.6 * math.exp(-0.3 * i)
        mod_l = [m[:, None, :] for m in jnp.split(mods[i, :batch], 6, axis=-1)]
        mod_c = [jnp.broadcast_to(m[None], (batch, 1, d)) for m in jnp.split(mods[i, batch:batch + 1], 6, axis=-1)]
        qg = jnp.tile(qn_g[i], 2)[None, :]
        kg = jnp.tile(kn_g[i], 2)[None, :]
        sgw = sg_w[i].reshape(SG_HEADS * CHUNK, CHUNK).astype(BF16)
        sgb = jnp.repeat(sg_b[i].T, SG_WIDTH // SG_HEADS, axis=1)
        hy_params = (hy_w1[i], hy_b1[i], hy_w2[i], hy_b2[i], hy_w3[i], hy_freq[i], hy_bias[i])

        def project(tokens, n_tok, mod, cos, sin):
            outs = _inproj(tokens, n_tok, i, norm1_g[i][None, :], mod[1], mod[0], w_in_b, hy_conv_w[i],
                           hy_conv_b[i][None, :], cos, sin, qg, kg, seg, sg_norm_g[i][None, :], sgw, sgb)
            hy = [a.reshape(batch, n_tok, HY_WIDTH) for a in outs[:3]]
            return [hy, outs[3]] + [a.reshape(batch, n_tok, DA_WIDTH) for a in outs[4:]]

        def finish(tokens, n_tok, mod, hy_parts, sg, att):
            hy = _hyena(*hy_parts, *hy_params).reshape(batch * n_tok, HY_WIDTH)
            return _mix_ffn(tokens, n_tok, i, hy, sg, att.reshape(batch * n_tok, DA_WIDTH), w_out_b, mod[2],
                            norm2_g[i][None, :], mod[4], mod[3], mod[5], w1_b, w2_b)

        hy_l, sg_l, q_l, k_l, v_l = project(xs, seq, mod_l, cos_l, sin_l)
        hy_c, sg_c, q_c, k_c, v_c = project(cs, ctx_len, mod_c, cos_c, sin_c)
        attend = functools.partial(_attention, lam_p=lam_p[i], subln_g=subln_g[i][None, :], lam_init=lam_init,
                                   qn_g=qn_g[i], kn_g=kn_g[i])
        att_l = attend(q_l, [(k_l, v_l), (k_c, v_c)])
        xs = finish(xs, seq, mod_l, hy_l, sg_l, att_l)
        if not last:
            cs = finish(cs, ctx_len, mod_c, hy_c, sg_c, attend(q_c, [(k_c, v_c)]))
    return xs.reshape(batch, seq, d)
```

```python
import functools
import math

import numpy as np
import jax
import jax.numpy as jnp
from jax import lax
from jax.experimental import pallas as pl
from jax.experimental.pallas import tpu as pltpu

F32 = jnp.float32
BF16 = jnp.bfloat16

GRID_W = 64
EPS = 1e-6
HY_WIDTH = 256
HY_ORDER = 2
HY_POS_BANDS = 16
HY_DECAY_TARGET = 1e-2
HY_FAST_DECAY_PCT = 0.3
HY_SLOW_DECAY_PCT = 1.5
SG_HEADS = 4
SG_WIDTH = 256
CHUNK = 128
DA_HEADS = 4
DA_WIDTH = 512
DA_V_DIM = 128
DA_HEAD_DIM = 64
ROPE_BASE = 10000.0
HY_IN = 3 * HY_WIDTH
SG_IN = 2 * SG_WIDTH
LANES = 128
DFT_N2 = 128
SLABS_PER_STEP = 16
VMEM_LIMIT = 56 * 1024 * 1024
NEG_BIG = -1e30
Q_SCALE = math.log2(math.e) * DA_HEAD_DIM ** -0.5
INPROJ_TM = 1024
FFN_TM = 512
FFN_CHUNK = 256
ATTN_TQ = 1024
ATTN_TK = 256


def _cparams(*sem):
    return pltpu.CompilerParams(dimension_semantics=sem, vmem_limit_bytes=VMEM_LIMIT)


def _dot(a, b):
    return jnp.dot(a, b, preferred_element_type=F32)


def _sigmoid(x):
    return 1.0 / (1.0 + jnp.exp(-x))


def _modulated_norm(x, g, sc, sh):
    y = x * lax.rsqrt(jnp.mean(x * x, axis=-1, keepdims=True) + EPS)
    return (y * g) * (1.0 + sc) + sh


def _ada_kernel(c_ref, w_ref, b_ref, o_ref):
    cv = c_ref[...]
    s = (cv * _sigmoid(cv)).astype(BF16)
    o_ref[...] = _dot(s, w_ref[...].astype(BF16)) + b_ref[...]


def _ada(cvecs, ada_w, ada_b):
    depth, d, n = ada_w.shape
    tn = 1536
    return pl.pallas_call(
        _ada_kernel,
        grid=(depth, n // tn),
        in_specs=[pl.BlockSpec((8, d), lambda i, j: (0, 0)),
                  pl.BlockSpec((None, d, tn), lambda i, j: (i, 0, j)),
                  pl.BlockSpec((None, 1, tn), lambda i, j: (i, 0, j))],
        out_specs=pl.BlockSpec((None, 8, tn), lambda i, j: (i, 0, j)),
        out_shape=jax.ShapeDtypeStruct((depth, 8, n), F32),
        compiler_params=_cparams("parallel", "parallel"),
        name="adaln",
    )(cvecs, ada_w, ada_b)


def _group_mean_sq(t, seg):
    sq = t * t
    hi = sq.astype(BF16)
    lo = (sq - hi.astype(F32)).astype(BF16)
    return _dot(jnp.concatenate([hi, lo], axis=1), seg)


def _norm_rope(t, gain, seg, cos, sin, swap_fwd):
    tn = t * lax.rsqrt(_group_mean_sq(t, seg) + EPS) * gain
    rot = jnp.where(swap_fwd, pltpu.roll(tn, LANES - 16, 1), pltpu.roll(tn, 16, 1))
    return tn * cos + rot * sin


def _inproj_kernel(x_ref, xp_ref, xn_ref, g_ref, sc_ref, sh_ref, w_ref, cw_ref, cb_ref, cos_ref, sin_ref, qg_ref,
                   kg_ref, seg_ref, sgg_ref, sgw_ref, sgb_ref, hx1_ref, hx2_ref, hxv_ref, sg_ref, q_ref, k_ref, v_ref,
                   *, per):
    tm = x_ref.shape[0]
    hb = _modulated_norm(x_ref[...], g_ref[...], sc_ref[...], sh_ref[...]).astype(BF16)

    halo = jnp.concatenate([xp_ref[...], xn_ref[...]], axis=0)
    hh = _modulated_norm(halo, g_ref[...], sc_ref[...], sh_ref[...]).astype(BF16)
    p_ext = _dot(jnp.concatenate([hb, hh], axis=0), w_ref[:, :HY_IN])
    pos = pl.program_id(0) % per
    before = jnp.where(pos == 0, 0.0, p_ext[tm + 7:tm + 8])
    after = jnp.where(pos == per - 1, 0.0, p_ext[tm + 8:tm + 9])
    rid = lax.broadcasted_iota(jnp.int32, (tm, 1), 0)
    for part, hx_ref in enumerate((hx1_ref, hx2_ref, hxv_ref)):
        cols = slice(part * HY_WIDTH, (part + 1) * HY_WIDTH)
        cur = p_ext[:tm, cols]
        prev = jnp.where(rid == 0, before[:, cols], pltpu.roll(cur, 1, 0))
        nxt = jnp.where(rid == tm - 1, after[:, cols], pltpu.roll(cur, tm - 1, 0))
        hx_ref[...] = prev * cw_ref[0:1, cols] + cur * cw_ref[1:2, cols] + nxt * cw_ref[2:3, cols] + cb_ref[:, cols]

    psg = _dot(hb, w_ref[:, HY_IN:HY_IN + SG_IN])
    ge = 0.5 * psg * (1.0 + lax.erf(psg * np.float32(math.sqrt(0.5))))
    u = ge[:, :SG_WIDTH]
    vv = ge[:, SG_WIDTH:]
    vn = (vv * lax.rsqrt(jnp.mean(vv * vv, axis=-1, keepdims=True) + EPS) * sgg_ref[...]).astype(BF16)
    head_of_lane = lax.broadcasted_iota(jnp.int32, (CHUNK, SG_WIDTH), 1) // (SG_WIDTH // SG_HEADS)
    for c in range(tm // CHUNK):
        rows = slice(c * CHUNK, (c + 1) * CHUNK)
        r = _dot(sgw_ref[...], vn[rows])
        mixed = sgb_ref[...]
        for h in range(SG_HEADS):
            mixed = mixed + jnp.where(head_of_lane == h, r[h * CHUNK:(h + 1) * CHUNK], 0.0)
        sg_ref[rows, :] = (u[rows] * mixed).astype(BF16)

    off = HY_IN + SG_IN
    lane = lax.broadcasted_iota(jnp.int32, (1, LANES), 1)
    swap_fwd = (lane % 32) < 16
    cos = cos_ref[...]
    sin = sin_ref[...]
    seg = seg_ref[...]
    pq = _dot(hb, w_ref[:, off:off + DA_WIDTH])
    pk = _dot(hb, w_ref[:, off + DA_WIDTH:off + 2 * DA_WIDTH])
    v_ref[...] = _dot(hb, w_ref[:, off + 2 * DA_WIDTH:]).astype(BF16)
    for h in range(DA_HEADS):
        cols = slice(h * LANES, (h + 1) * LANES)
        q = _norm_rope(pq[:, cols], qg_ref[...], seg, cos, sin, swap_fwd)
        q_ref[:, cols] = (q * Q_SCALE).astype(BF16)
        k_ref[:, cols] = _norm_rope(pk[:, cols], kg_ref[...], seg, cos, sin, swap_fwd).astype(BF16)


def _inproj(x2d, seq, layer, g, sc, sh, w, conv_w, conv_b, cos, sin, qg, kg, seg, sgg, sgw, sgb):
    m, d = x2d.shape
    n = w.shape[2]
    tm = min(INPROJ_TM, seq)
    per = seq // tm
    halo = 8
    row = lambda i: (i, 0)
    fixed = lambda i: (0, 0)
    bat = lambda i: (i // per, 0, 0)
    pos = lambda i: (i % per, 0)
    outs = [(HY_WIDTH, F32)] * 3 + [(SG_WIDTH, BF16)] + [(DA_WIDTH, BF16)] * 3
    return pl.pallas_call(
        functools.partial(_inproj_kernel, per=per),
        grid=(m // tm,),
        in_specs=[pl.BlockSpec((tm, d), row),
                  pl.BlockSpec((halo, d), lambda i: (jnp.maximum(i * (tm // halo) - 1, 0), 0)),
                  pl.BlockSpec((halo, d), lambda i: (jnp.minimum((i + 1) * (tm // halo), m // halo - 1), 0)),
                  pl.BlockSpec((1, d), fixed),
                  pl.BlockSpec((None, 1, d), bat),
                  pl.BlockSpec((None, 1, d), bat),
                  pl.BlockSpec((None, d, n), lambda i: (layer, 0, 0)),
                  pl.BlockSpec((3, HY_IN), fixed),
                  pl.BlockSpec((1, HY_IN), fixed),
                  pl.BlockSpec((tm, LANES), pos),
                  pl.BlockSpec((tm, LANES), pos),
                  pl.BlockSpec((1, LANES), fixed),
                  pl.BlockSpec((1, LANES), fixed),
                  pl.BlockSpec((2 * LANES, LANES), fixed),
                  pl.BlockSpec((1, SG_WIDTH), fixed),
                  pl.BlockSpec((SG_HEADS * CHUNK, CHUNK), fixed),
                  pl.BlockSpec((CHUNK, SG_WIDTH), fixed)],
        out_specs=[pl.BlockSpec((tm, c), row) for c, _ in outs],
        out_shape=[jax.ShapeDtypeStruct((m, c), dt) for c, dt in outs],
        compiler_params=_cparams("parallel"),
        name="inproj",
    )(x2d, x2d, x2d, g, sc, sh, w, conv_w, conv_b, cos, sin, qg, kg, seg, sgg, sgw, sgb)


_NT = (((1,), (1,)), ((), ()))


def _lambda(lp_ref, lam_init):
    lp = lp_ref[...]
    return (jnp.exp(jnp.sum(lp[0:1] * lp[1:2], axis=-1, keepdims=True))
            - jnp.exp(jnp.sum(lp[2:3] * lp[3:4], axis=-1, keepdims=True)) + lam_init)


def _stacked_components(q):
    lane = lax.broadcasted_iota(jnp.int32, (1, LANES), 1)
    zero = jnp.zeros_like(q)
    return jnp.concatenate([jnp.where(lane < DA_HEAD_DIM, q, zero), jnp.where(lane >= DA_HEAD_DIM, q, zero)],
                           axis=0)


def _attn_finish(o1, o2, lam, g_ref, o_ref, lam_init):
    o = o1 - lam * o2
    o = o * lax.rsqrt(jnp.mean(o * o, axis=-1, keepdims=True) + EPS) * g_ref[...]
    o_ref[...] = (o * (1.0 - lam_init)).astype(BF16)


def _attn_bounded_kernel(lp_ref, g_ref, q_ref, *refs, tiles, lam_init):
    tq = q_ref.shape[0]
    o_ref = refs[-1]
    qt = q_ref[...].T
    dim = lax.broadcasted_iota(jnp.int32, (LANES, 1), 0)
    zero = jnp.zeros_like(qt)
    qst = jnp.concatenate([jnp.where(dim < DA_HEAD_DIM, qt, zero), jnp.where(dim >= DA_HEAD_DIM, qt, zero)],
                          axis=1)
    acc = jnp.zeros((DA_V_DIM + 16, 2 * tq), F32)
    for j, tk in enumerate(tiles):
        k_ref, v_ref = refs[2 * j], refs[2 * j + 1]
        ones = jnp.ones((16, tk), BF16)
        for i in range(k_ref.shape[0] // tk):
            st = _dot(k_ref[i * tk:(i + 1) * tk, :], qst)
            vt = jnp.concatenate([v_ref[i * tk:(i + 1) * tk, :].T, ones], axis=0)
            acc = acc + _dot(vt, jnp.exp2(st).astype(BF16))
    ot = acc[:DA_V_DIM] / acc[DA_V_DIM:DA_V_DIM + 1]
    dt = ot[:, :tq] - _lambda(lp_ref, lam_init) * ot[:, tq:]
    dt = dt * lax.rsqrt(jnp.mean(dt * dt, axis=0, keepdims=True) + EPS)
    o_ref[...] = (dt.T * g_ref[...] * (1.0 - lam_init)).astype(BF16)


def _attn_online_kernel(lp_ref, g_ref, q_ref, *refs, tiles, lam_init):
    tq = q_ref.shape[0]
    o_ref = refs[-1]
    qs = _stacked_components(q_ref[...])
    carry = (jnp.full((2 * tq, 1), NEG_BIG, F32), jnp.zeros((2 * tq, 1), F32),
             jnp.zeros((2 * tq, DA_V_DIM), F32))
    for j, tk in enumerate(tiles):
        k_ref, v_ref = refs[2 * j], refs[2 * j + 1]

        def body(i, carry, k_ref=k_ref, v_ref=v_ref, tk=tk):
            m, l, acc = carry
            start = pl.multiple_of(i * tk, tk)
            kb = k_ref[pl.ds(start, tk), :]
            vb = v_ref[pl.ds(start, tk), :]
            s = lax.dot_general(qs, kb, _NT, preferred_element_type=F32)
            m_new = jnp.maximum(m, jnp.max(s, axis=-1, keepdims=True))
            alpha = jnp.exp2(m - m_new)
            p = jnp.exp2(s - m_new)
            l_new = alpha * l + jnp.sum(p, axis=-1, keepdims=True)
            return m_new, l_new, alpha * acc + _dot(p.astype(BF16), vb)

        carry = lax.fori_loop(0, k_ref.shape[0] // tk, body, carry)
    _, l, acc = carry
    o = acc / l
    _attn_finish(o[:tq], o[tq:], _lambda(lp_ref, lam_init), g_ref, o_ref, lam_init)


def _pick_tile(n, candidates):
    for c in candidates:
        if n % c == 0:
            return c
    raise ValueError(f"no tile for {n}")


BOUNDED_SCORE_LIMIT = 56.0


def _attention(q, kv, lam_p, subln_g, lam_init, qn_g, kn_g):
    bound = (math.sqrt(DA_HEAD_DIM) * math.log2(math.e) * 1.02) * jnp.max(jnp.abs(qn_g)) * jnp.max(jnp.abs(kn_g))
    flat = [a for pair in kv for a in pair]
    return lax.cond(bound <= BOUNDED_SCORE_LIMIT,
                    functools.partial(_attention_call, _attn_bounded_kernel, lam_init),
                    functools.partial(_attention_call, _attn_online_kernel, lam_init),
                    q, lam_p, subln_g, *flat)


def _attention_call(body, lam_init, q, lam_p, subln_g, *kv):
    b, lq, _ = q.shape
    tq = _pick_tile(lq, (ATTN_TQ, 512, 256, 128))
    tiles = tuple(_pick_tile(a.shape[1], (ATTN_TK, 512, 256, 128)) for a in kv[::2])
    whole = lambda a: pl.BlockSpec((None, a.shape[1], LANES), lambda bi, h, i: (bi, 0, h))
    return pl.pallas_call(
        functools.partial(body, tiles=tiles, lam_init=lam_init),
        grid=(b, DA_HEADS, lq // tq),
        in_specs=[pl.BlockSpec((4, DA_HEAD_DIM), lambda bi, h, i: (0, 0)),
                  pl.BlockSpec((1, DA_V_DIM), lambda bi, h, i: (0, 0)),
                  pl.BlockSpec((None, tq, LANES), lambda bi, h, i: (bi, i, h))] + [whole(a) for a in kv],
        out_specs=pl.BlockSpec((None, tq, LANES), lambda bi, h, i: (bi, i, h)),
        out_shape=jax.ShapeDtypeStruct((b, lq, DA_WIDTH), BF16),
        compiler_params=_cparams("parallel", "parallel", "parallel"),
        name="diff_attention",
    )(lam_p, subln_g, q, *kv)


def _filter_kernel(f_ref, w1_ref, b1_ref, w2_ref, b2_ref, w3_ref, fr_ref, dl_ref, o_ref, *, seq):
    tl = f_ref.shape[0]
    half = LANES // 2
    nout = HY_ORDER * HY_WIDTH
    feats = f_ref[...]
    h = jnp.sin(fr_ref[0:1, :] * (_dot(feats.astype(BF16), w1_ref[...]) + b1_ref[...]))
    h = jnp.sin(fr_ref[1:2, :] * (_dot(h.astype(BF16), w2_ref[...]) + b2_ref[...]))
    h = _dot(h.astype(BF16), w3_ref[...])
    row = pl.program_id(0) * tl + lax.broadcasted_iota(jnp.int32, (tl, 1), 0)
    o_ref[0] = h[:, :nout] * jnp.exp(-feats[:, 0:1] * dl_ref[...])
    bwd = h[:, nout:] * jnp.exp(-feats[:, half:half + 1] * dl_ref[...])
    o_ref[1] = jnp.where(row == 0, 0.0, bwd)


def _hyena_features(seq):
    t = np.linspace(0.0, 1.0, seq, dtype=np.float32)[:, None]
    bands = np.linspace(1e-4, HY_POS_BANDS - 1, HY_POS_BANDS, dtype=np.float32)
    ang = np.float32(2.0 * math.pi / seq) * np.arange(seq, dtype=np.float32)[:, None] * bands[None, :]
    feats = np.concatenate([t, np.cos(ang), -np.sin(ang)], axis=-1).astype(np.float32)
    back = np.concatenate([feats[:1], feats[:0:-1]], axis=0)
    pad = ((0, 0), (0, LANES // 2 - feats.shape[1]))
    return jnp.asarray(np.concatenate([np.pad(feats, pad), np.pad(back, pad)], axis=1))


def _hyena_filters(seq, w1, b1, w2, b2, w3, freq):
    feats = _hyena_features(seq)
    nfeat, hid = w1.shape
    half = LANES // 2
    nout = HY_ORDER * HY_WIDTH
    assert nfeat <= half and hid <= half
    w1p = jnp.zeros((LANES, LANES), F32).at[:nfeat, :hid].set(w1).at[half:half + nfeat, half:half + hid].set(w1)
    w2p = jnp.zeros((LANES, LANES), F32).at[:hid, :hid].set(w2).at[half:half + hid, half:half + hid].set(w2)
    w3d = w3.reshape(hid, HY_ORDER, 2, HY_WIDTH)
    w3p = (jnp.zeros((LANES, 2 * nout), F32)
           .at[:hid, :nout].set(w3d[:, :, 0].reshape(hid, nout))
           .at[half:half + hid, nout:].set(w3d[:, :, 1].reshape(hid, nout)))
    both = lambda a: jnp.tile(jnp.pad(a.reshape(-1, hid), ((0, 0), (0, half - hid))), (1, 2))
    min_decay = math.log(HY_DECAY_TARGET) / HY_SLOW_DECAY_PCT
    max_decay = math.log(HY_DECAY_TARGET) / HY_FAST_DECAY_PCT
    deltas = jnp.abs(jnp.linspace(min_decay, max_decay, HY_WIDTH, dtype=F32))[None, :]
    tl = min(512, seq)
    fixed = lambda i: (0, 0)
    kt = pl.pallas_call(
        functools.partial(_filter_kernel, seq=seq),
        grid=(seq // tl,),
        in_specs=[pl.BlockSpec((tl, LANES), lambda i: (i, 0)),
                  pl.BlockSpec((LANES, LANES), fixed), pl.BlockSpec((1, LANES), fixed),
                  pl.BlockSpec((LANES, LANES), fixed), pl.BlockSpec((1, LANES), fixed),
                  pl.BlockSpec((LANES, 2 * nout), fixed), pl.BlockSpec((2, LANES), fixed),
                  pl.BlockSpec((1, nout), fixed)],
        out_specs=pl.BlockSpec((2, tl, nout), lambda i: (0, i, 0)),
        out_shape=jax.ShapeDtypeStruct((2, seq, nout), F32),
        compiler_params=_cparams("parallel"),
        name="hyena_filter_mlp",
    )(feats, w1p.astype(BF16), both(b1), w2p.astype(BF16), both(b2), w3p.astype(BF16), both(freq),
      jnp.tile(deltas, (1, HY_ORDER)))
    return kt.reshape(2 * seq, nout)


def _dft_tables(seq):
    n = 2 * seq
    n1 = n // DFT_N2
    idx1 = np.arange(n1)
    idx2 = np.arange(DFT_N2)
    f1 = np.exp(-2j * np.pi * np.outer(idx1, idx1) / n1)
    k = idx1[:, None, None] + n1 * idx2[None, :, None]
    g = np.exp(-2j * np.pi * (k * idx2[None, None, :] % n) / n)
    return n1, f1, g


def _rows_block(nin):
    s_mm = max(8, LANES // nin)
    return s_mm, max(16, s_mm)


def _kron_rows(mat, s):
    return jnp.asarray(np.kron(mat, np.eye(s)), F32).astype(BF16)


def _rows_real_kernel(u_ref, kr_ref, ki_ref, ar_ref, ai_ref, *, s_mm):
    nin, s_blk, c = u_ref.shape
    n1 = ar_ref.shape[0]
    re, im = [], []
    for h in range(s_blk // s_mm):
        u = u_ref[:, h * s_mm:(h + 1) * s_mm, :].reshape(nin * s_mm, c).astype(BF16)
        re.append(_dot(kr_ref[...], u).reshape(n1, s_mm, c))
        im.append(_dot(ki_ref[...], u).reshape(n1, s_mm, c))
    ar_ref[...] = jnp.concatenate(re, axis=1).astype(BF16)
    ai_ref[...] = jnp.concatenate(im, axis=1).astype(BF16)


def _rows_real(u, seq):
    rows, c = u.shape
    n1, f1, _ = _dft_tables(seq)
    nin = rows // DFT_N2
    s_mm, s_blk = _rows_block(nin)
    fixed = lambda j: (0, 0)
    out_spec = pl.BlockSpec((n1, s_blk, c), lambda j: (0, j, 0))
    out_shape = jax.ShapeDtypeStruct((n1, DFT_N2, c), BF16)
    return pl.pallas_call(
        functools.partial(_rows_real_kernel, s_mm=s_mm),
        grid=(DFT_N2 // s_blk,),
        in_specs=[pl.BlockSpec((nin, s_blk, c), lambda j: (0, j, 0)),
                  pl.BlockSpec((n1 * s_mm, nin * s_mm), fixed),
                  pl.BlockSpec((n1 * s_mm, nin * s_mm), fixed)],
        out_specs=[out_spec, out_spec],
        out_shape=[out_shape, out_shape],
        compiler_params=_cparams("parallel"),
        name="hyena_filter_dft_rows",
    )(u.reshape(nin, DFT_N2, c), _kron_rows(f1.real[:, :nin], s_mm), _kron_rows(f1.imag[:, :nin], s_mm))


def _rows_pair_kernel(u_ref, kc_ref, ar_ref, ai_ref, *, s_mm):
    _, nin, s_blk, c = u_ref.shape
    m = kc_ref.shape[0] // 2
    re, im = [], []
    for h in range(s_blk // s_mm):
        sub = slice(h * s_mm, (h + 1) * s_mm)
        p = _dot(kc_ref[...], u_ref[0, :, sub, :].reshape(nin * s_mm, c).astype(BF16))
        q = _dot(kc_ref[...], u_ref[1, :, sub, :].reshape(nin * s_mm, c).astype(BF16))
        re.append((p[:m] - q[m:]).reshape(m // s_mm, s_mm, c))
        im.append((q[:m] + p[m:]).reshape(m // s_mm, s_mm, c))
    ar_ref[...] = jnp.concatenate(re, axis=1).astype(BF16)
    ai_ref[...] = jnp.concatenate(im, axis=1).astype(BF16)


def _rows_pair(u, seq):
    b, rows, c = u.shape
    n1, f1, _ = _dft_tables(seq)
    nin = rows // DFT_N2
    s_mm, s_blk = _rows_block(nin)
    kcat = jnp.concatenate([_kron_rows(f1.real[:, :nin], s_mm), _kron_rows(f1.imag[:, :nin], s_mm)], axis=0)
    out_spec = pl.BlockSpec((None, n1, s_blk, c), lambda p, j: (p, 0, j, 0))
    out_shape = jax.ShapeDtypeStruct((b // 2, n1, DFT_N2, c), BF16)
    return pl.pallas_call(
        functools.partial(_rows_pair_kernel, s_mm=s_mm),
        grid=(b // 2, DFT_N2 // s_blk),
        in_specs=[pl.BlockSpec((2, nin, s_blk, c), lambda p, j: (p, 0, j, 0)),
                  pl.BlockSpec(kcat.shape, lambda p, j: (0, 0))],
        out_specs=[out_spec, out_spec],
        out_shape=[out_shape, out_shape],
        compiler_params=_cparams("parallel", "parallel"),
        name="hyena_dft_rows",
    )(u.reshape(b, nin, DFT_N2, c), kcat)


def _complex_apply(mcat, xr, xi):
    p = _dot(mcat, xr)
    q = _dot(mcat, xi)
    return p[:DFT_N2] - q[DFT_N2:], q[:DFT_N2] + p[DFT_N2:]


def _spectrum_kernel(ar_ref, ai_ref, g_ref, xr_ref, xi_ref, *, scale):
    for j in range(ar_ref.shape[0]):
        xr, xi = _complex_apply(g_ref[j], ar_ref[j], ai_ref[j])
        xr_ref[j] = (xr * scale).astype(BF16)
        xi_ref[j] = (xi * scale).astype(BF16)


def _slab_conv_kernel(ar_ref, ai_ref, kr_ref, ki_ref, g_ref, h_ref, dr_ref, di_ref):
    for j in range(ar_ref.shape[0]):
        xr, xi = _complex_apply(g_ref[j], ar_ref[j], ai_ref[j])
        kr = kr_ref[j].astype(F32)
        ki = ki_ref[j].astype(F32)
        zr = (xr * kr - xi * ki).astype(BF16)
        zi = (xr * ki + xi * kr).astype(BF16)
        dr, di = _complex_apply(h_ref[j], zr, zi)
        dr_ref[j] = dr.astype(BF16)
        di_ref[j] = di.astype(BF16)


def _slab_consts(seq):
    n1, _, g = _dft_tables(seq)
    gcat = jnp.asarray(np.concatenate([g.real, g.imag], axis=1), F32).astype(BF16)
    gt = np.conj(g).transpose(0, 2, 1)
    hcat = jnp.asarray(np.concatenate([gt.real, gt.imag], axis=1), F32).astype(BF16)
    return n1, gcat, hcat


def _spectrum(ar, ai, seq):
    n1, gcat, _ = _slab_consts(seq)
    c = ar.shape[-1]
    g = min(SLABS_PER_STEP, n1)
    slab = pl.BlockSpec((g, DFT_N2, c), lambda i: (i, 0, 0))
    return pl.pallas_call(
        functools.partial(_spectrum_kernel, scale=1.0 / (2 * seq)),
        grid=(n1 // g,),
        in_specs=[slab, slab, pl.BlockSpec((g, 2 * DFT_N2, DFT_N2), lambda i: (i, 0, 0))],
        out_specs=[slab, slab],
        out_shape=[jax.ShapeDtypeStruct((n1, DFT_N2, c), BF16)] * 2,
        compiler_params=_cparams("parallel"),
        name="hyena_filter_spectrum",
    )(ar, ai, gcat)


def _slab_conv(ar, ai, kr, ki, order, seq):
    n1, gcat, hcat = _slab_consts(seq)
    npair, _, _, c = ar.shape
    g = min(SLABS_PER_STEP, n1)
    slab = pl.BlockSpec((None, g, DFT_N2, c), lambda p, i: (p, i, 0, 0))
    filt = pl.BlockSpec((g, DFT_N2, c), lambda p, i: (i, 0, order))
    mats = pl.BlockSpec((g, 2 * DFT_N2, DFT_N2), lambda p, i: (i, 0, 0))
    return pl.pallas_call(
        _slab_conv_kernel,
        grid=(npair, n1 // g),
        in_specs=[slab, slab, filt, filt, mats, mats],
        out_specs=[slab, slab],
        out_shape=[jax.ShapeDtypeStruct(ar.shape, BF16)] * 2,
        compiler_params=_cparams("parallel", "parallel"),
        name="hyena_dft_slabs",
    )(ar, ai, kr, ki, gcat, hcat)


def _rows_inverse_kernel(dr_ref, di_ref, kc_ref, u_ref, gate_ref, bias_ref, o_ref, *, s_mm):
    n1, s_blk, c = dr_ref.shape
    m = kc_ref.shape[0] // 2
    dr = dr_ref[...].astype(F32)
    di = di_ref[...].astype(F32)
    y0, y1 = [], []
    for h in range(s_blk // s_mm):
        sub = slice(h * s_mm, (h + 1) * s_mm)
        p = _dot(kc_ref[...], dr[:, sub, :].reshape(n1 * s_mm, c).astype(BF16))
        q = _dot(kc_ref[...], di[:, sub, :].reshape(n1 * s_mm, c).astype(BF16))
        y0.append((p[:m] + q[m:]).reshape(m // s_mm, s_mm, c))
        y1.append((q[:m] - p[m:]).reshape(m // s_mm, s_mm, c))
    for e, parts in enumerate((y0, y1)):
        y = jnp.concatenate(parts, axis=1)
        o_ref[e] = (gate_ref[e] * (y + u_ref[e] * bias_ref[...])).astype(o_ref.dtype)


def _rows_inverse(dr, di, u, gate, bias, seq, out_dtype):
    npair, n1, _, c = dr.shape
    _, f1, _ = _dft_tables(seq)
    nout = seq // DFT_N2
    s_mm, s_blk = _rows_block(nout)
    kcat = jnp.concatenate([_kron_rows(f1.real[:nout], s_mm), _kron_rows(f1.imag[:nout], s_mm)], axis=0)
    dspec = pl.BlockSpec((None, n1, s_blk, c), lambda p, j: (p, 0, j, 0))
    uspec = pl.BlockSpec((2, nout, s_blk, c), lambda p, j: (p, 0, j, 0))
    y = pl.pallas_call(
        functools.partial(_rows_inverse_kernel, s_mm=s_mm),
        grid=(npair, DFT_N2 // s_blk),
        in_specs=[dspec, dspec, pl.BlockSpec(kcat.shape, lambda p, j: (0, 0)),
                  uspec, uspec, pl.BlockSpec((1, 1, c), lambda p, j: (0, 0, 0))],
        out_specs=uspec,
        out_shape=jax.ShapeDtypeStruct((2 * npair, nout, DFT_N2, c), out_dtype),
        compiler_params=_cparams("parallel", "parallel"),
        name="hyena_idft_rows",
    )(dr, di, kcat, u.reshape(2 * npair, nout, DFT_N2, c), gate.reshape(2 * npair, nout, DFT_N2, c),
      bias.reshape(1, 1, c))
    return y.reshape(2 * npair, seq, c)


def _hyena(x1, x2, v, w1, b1, w2, b2, w3, freq, bias):
    b, seq, _ = v.shape
    assert b % 2 == 0, "batch elements are transformed in pairs"
    kt = _hyena_filters(seq, w1, b1, w2, b2, w3, freq)
    kr, ki = _spectrum(*_rows_real(kt, seq), seq)
    dr, di = _slab_conv(*_rows_pair(v, seq), kr, ki, 0, seq)
    z = _rows_inverse(dr, di, v, x1, bias[0], seq, F32)
    dr, di = _slab_conv(*_rows_pair(z, seq), kr, ki, 1, seq)
    return _rows_inverse(dr, di, z, x2, bias[1], seq, BF16)


def _mix_ffn_kernel(x_ref, hy_ref, sg_ref, at_ref, wo_ref, g1_ref, n2_ref, sc_ref, sh_ref, g2_ref, w1_ref,
                    w2_ref, o_ref, *, chunk):
    mix = _dot(hy_ref[...], wo_ref[:HY_WIDTH])
    mix = mix + _dot(sg_ref[...], wo_ref[HY_WIDTH:HY_WIDTH + SG_WIDTH])
    mix = mix + _dot(at_ref[...], wo_ref[HY_WIDTH + SG_WIDTH:])
    x = x_ref[...] + g1_ref[...] * mix
    hb = _modulated_norm(x, n2_ref[...], sc_ref[...], sh_ref[...]).astype(BF16)
    hidden = w2_ref.shape[0]
    acc = jnp.zeros(x.shape, F32)
    for c in range(hidden // chunk):
        a = _dot(hb, w1_ref[:, c * chunk:(c + 1) * chunk])
        up = _dot(hb, w1_ref[:, hidden + c * chunk:hidden + (c + 1) * chunk])
        act = (a * _sigmoid(a) * up).astype(BF16)
        acc = acc + _dot(act, w2_ref[c * chunk:(c + 1) * chunk, :])
    o_ref[...] = x + g2_ref[...] * acc


def _mix_ffn(x2d, seq, layer, hy, sg, att, w_out, g1, n2, sc, sh, g2, w1, w2):
    m, d = x2d.shape
    tm = min(FFN_TM, seq)
    per = seq // tm
    row = lambda i: (i, 0)
    fixed = lambda i: (0, 0)
    bat = lambda i: (i // per, 0, 0)
    resident = lambda a: pl.BlockSpec((None,) + a.shape[1:], lambda i: (layer, 0, 0), pipeline_mode=pl.Buffered(1))
    return pl.pallas_call(
        functools.partial(_mix_ffn_kernel, chunk=FFN_CHUNK),
        grid=(m // tm,),
        in_specs=[pl.BlockSpec((tm, d), row),
                  pl.BlockSpec((tm, HY_WIDTH), row),
                  pl.BlockSpec((tm, SG_WIDTH), row),
                  pl.BlockSpec((tm, DA_WIDTH), row),
                  resident(w_out),
                  pl.BlockSpec((None, 1, d), bat),
                  pl.BlockSpec((1, d), fixed),
                  pl.BlockSpec((None, 1, d), bat),
                  pl.BlockSpec((None, 1, d), bat),
                  pl.BlockSpec((None, 1, d), bat),
                  resident(w1),
                  resident(w2)],
        out_specs=pl.BlockSpec((tm, d), row),
        out_shape=jax.ShapeDtypeStruct((m, d), F32),
        compiler_params=_cparams("parallel"),
        name="outproj_ffn",
    )(x2d, hy, sg, att, w_out, g1, n2, sc, sh, g2, w1, w2)


def _rope_tables(seq):
    t = np.arange(seq)
    half = DA_HEAD_DIM // 4
    inv_freq = (np.float32(ROPE_BASE) ** (-np.arange(half, dtype=np.float32) / np.float32(half))).astype(np.float32)
    ang_row = (t // GRID_W).astype(np.float32)[:, None] * inv_freq[None, :]
    ang_col = (t % GRID_W).astype(np.float32)[:, None] * inv_freq[None, :]
    cos = np.concatenate([np.cos(ang_row)] * 2 + [np.cos(ang_col)] * 2, axis=-1)
    sin = np.concatenate([-np.sin(ang_row), np.sin(ang_row), -np.sin(ang_col), np.sin(ang_col)], axis=-1)
    return jnp.asarray(np.tile(cos, (1, 2)), F32), jnp.asarray(np.tile(sin, (1, 2)), F32)


def kernel(x, c, ctx, c_ctx, norm1_g, norm2_g, ada_w, ada_b, w_in, hy_conv_w, hy_conv_b, hy_w1, hy_b1,
           hy_w2, hy_b2, hy_w3, hy_freq, hy_bias, sg_norm_g, sg_w, sg_b, qn_g, kn_g, lam_p, subln_g,
           w_out, ffn_w1, ffn_w2):
    batch, seq, d = x.shape
    ctx_len = ctx.shape[1]
    depth = w_in.shape[0]
    assert batch + 1 <= 8 and seq % CHUNK == 0 and ctx_len % CHUNK == 0

    cvecs = jnp.zeros((8, d), F32).at[:batch].set(c).at[batch].set(c_ctx)
    mods = _ada(cvecs, ada_w, ada_b[:, None, :])

    cos_l, sin_l = _rope_tables(seq)
    cos_c = jnp.ones((ctx_len, LANES), F32)
    sin_c = jnp.zeros((ctx_len, LANES), F32)
    seg_np = np.kron(np.eye(LANES // DA_HEAD_DIM), np.full((DA_HEAD_DIM, DA_HEAD_DIM), 1.0 / DA_HEAD_DIM))
    seg = jnp.asarray(np.concatenate([seg_np, seg_np], axis=0), F32).astype(BF16)

    w_in_b = w_in.astype(BF16)
    w_out_b = w_out.astype(BF16)
    w1_b = ffn_w1.astype(BF16)
    w2_b = ffn_w2.astype(BF16)
    xs = x.reshape(batch * seq, d)
    cs = ctx.reshape(batch * ctx_len, d)
    for i in range(depth):
        last = i == depth - 1
        lam_init = 0.8 - 0.6 * math.exp(-0.3 * i)
        mod_l = [m[:, None, :] for m in jnp.split(mods[i, :batch], 6, axis=-1)]
        mod_c = [jnp.broadcast_to(m[None], (batch, 1, d)) for m in jnp.split(mods[i, batch:batch + 1], 6, axis=-1)]
        qg = jnp.tile(qn_g[i], 2)[None, :]
        kg = jnp.tile(kn_g[i], 2)[None, :]
        sgw = sg_w[i].reshape(SG_HEADS * CHUNK, CHUNK).astype(BF16)
        sgb = jnp.repeat(sg_b[i].T, SG_WIDTH // SG_HEADS, axis=1)
        hy_params = (hy_w1[i], hy_b1[i], hy_w2[i], hy_b2[i], hy_w3[i], hy_freq[i], hy_bias[i])

        def project(tokens, n_tok, mod, cos, sin):
            outs = _inproj(tokens, n_tok, i, norm1_g[i][None, :], mod[1], mod[0], w_in_b, hy_conv_w[i],
                           hy_conv_b[i][None, :], cos, sin, qg, kg, seg, sg_norm_g[i][None, :], sgw, sgb)
            hy = [a.reshape(batch, n_tok, HY_WIDTH) for a in outs[:3]]
            return [hy, outs[3]] + [a.reshape(batch, n_tok, DA_WIDTH) for a in outs[4:]]

        def finish(tokens, n_tok, mod, hy_parts, sg, att):
            hy = _hyena(*hy_parts, *hy_params).reshape(batch * n_tok, HY_WIDTH)
            return _mix_ffn(tokens, n_tok, i, hy, sg, att.reshape(batch * n_tok, DA_WIDTH), w_out_b, mod[2],
                            norm2_g[i][None, :], mod[4], mod[3], mod[5], w1_b, w2_b)

        hy_l, sg_l, q_l, k_l, v_l = project(xs, seq, mod_l, cos_l, sin_l)
        hy_c, sg_c, q_c, k_c, v_c = project(cs, ctx_len, mod_c, cos_c, sin_c)
        attend = functools.partial(_attention, lam_p=lam_p[i], subln_g=subln_g[i][None, :], lam_init=lam_init,
                                   qn_g=qn_g[i], kn_g=kn_g[i])
        att_l = attend(q_l, [(k_l, v_l), (k_c, v_c)])
        xs = finish(xs, seq, mod_l, hy_l, sg_l, att_l)
        if not last:
            cs = finish(cs, ctx_len, mod_c, hy_c, sg_c, attend(q_c, [(k_c, v_c)]))
    return xs.reshape(batch, seq, d)
```

```python
import functools
import math

import numpy as np
import jax
import jax.numpy as jnp
from jax import lax
from jax.experimental import pallas as pl
from jax.experimental.pallas import tpu as pltpu

F32 = jnp.float32
BF16 = jnp.bfloat16

GRID_W = 64
EPS = 1e-6
HY_WIDTH = 256
HY_ORDER = 2
HY_POS_BANDS = 16
HY_DECAY_TARGET = 1e-2
HY_FAST_DECAY_PCT = 0.3
HY_SLOW_DECAY_PCT = 1.5
SG_HEADS = 4
SG_WIDTH = 256
CHUNK = 128
DA_HEADS = 4
DA_WIDTH = 512
DA_V_DIM = 128
DA_HEAD_DIM = 64
ROPE_BASE = 10000.0
HY_IN = 3 * HY_WIDTH
SG_IN = 2 * SG_WIDTH
LANES = 128
DFT_N2 = 128
SLABS_PER_STEP = 16
VMEM_LIMIT = 56 * 1024 * 1024
NEG_BIG = -1e30
Q_SCALE = math.log2(math.e) * DA_HEAD_DIM ** -0.5
INPROJ_TM = 1024
FFN_TM = 512
FFN_CHUNK = 256
ATTN_TQ = 1024
ATTN_TK = 512


def _cparams(*sem):
    return pltpu.CompilerParams(dimension_semantics=sem, vmem_limit_bytes=VMEM_LIMIT)


def _dot(a, b):
    return jnp.dot(a, b, preferred_element_type=F32)


def _sigmoid(x):
    return 1.0 / (1.0 + jnp.exp(-x))


def _modulated_norm(x, g, sc, sh):
    y = x * lax.rsqrt(jnp.mean(x * x, axis=-1, keepdims=True) + EPS)
    return (y * g) * (1.0 + sc) + sh


def _ada_kernel(c_ref, w_ref, b_ref, o_ref):
    cv = c_ref[...]
    s = (cv * _sigmoid(cv)).astype(BF16)
    o_ref[...] = _dot(s, w_ref[...].astype(BF16)) + b_ref[...]


def _ada(cvecs, ada_w, ada_b):
    depth, d, n = ada_w.shape
    tn = 1536
    return pl.pallas_call(
        _ada_kernel,
        grid=(depth, n // tn),
        in_specs=[pl.BlockSpec((8, d), lambda i, j: (0, 0)),
                  pl.BlockSpec((None, d, tn), lambda i, j: (i, 0, j)),
                  pl.BlockSpec((None, 1, tn), lambda i, j: (i, 0, j))],
        out_specs=pl.BlockSpec((None, 8, tn), lambda i, j: (i, 0, j)),
        out_shape=jax.ShapeDtypeStruct((depth, 8, n), F32),
        compiler_params=_cparams("parallel", "parallel"),
        name="adaln",
    )(cvecs, ada_w, ada_b)


def _group_mean_sq(t, seg):
    sq = t * t
    hi = sq.astype(BF16)
    lo = (sq - hi.astype(F32)).astype(BF16)
    return _dot(jnp.concatenate([hi, lo], axis=1), seg)


def _norm_rope(t, gain, seg, cos, sin, swap_fwd):
    tn = t * lax.rsqrt(_group_mean_sq(t, seg) + EPS) * gain
    rot = jnp.where(swap_fwd, pltpu.roll(tn, LANES - 16, 1), pltpu.roll(tn, 16, 1))
    return tn * cos + rot * sin


def _inproj_kernel(x_ref, xp_ref, xn_ref, g_ref, sc_ref, sh_ref, w_ref, cw_ref, cb_ref, cos_ref, sin_ref, qg_ref,
                   kg_ref, seg_ref, sgg_ref, sgw_ref, sgb_ref, hx1_ref, hx2_ref, hxv_ref, sg_ref, q_ref, k_ref, v_ref,
                   *, per):
    tm = x_ref.shape[0]
    hb = _modulated_norm(x_ref[...], g_ref[...], sc_ref[...], sh_ref[...]).astype(BF16)

    halo = jnp.concatenate([xp_ref[...], xn_ref[...]], axis=0)
    hh = _modulated_norm(halo, g_ref[...], sc_ref[...], sh_ref[...]).astype(BF16)
    p_ext = _dot(jnp.concatenate([hb, hh], axis=0), w_ref[:, :HY_IN])
    pos = pl.program_id(0) % per
    before = jnp.where(pos == 0, 0.0, p_ext[tm + 7:tm + 8])
    after = jnp.where(pos == per - 1, 0.0, p_ext[tm + 8:tm + 9])
    rid = lax.broadcasted_iota(jnp.int32, (tm, 1), 0)
    for part, hx_ref in enumerate((hx1_ref, hx2_ref, hxv_ref)):
        cols = slice(part * HY_WIDTH, (part + 1) * HY_WIDTH)
        cur = p_ext[:tm, cols]
        prev = jnp.where(rid == 0, before[:, cols], pltpu.roll(cur, 1, 0))
        nxt = jnp.where(rid == tm - 1, after[:, cols], pltpu.roll(cur, tm - 1, 0))
        hx_ref[...] = prev * cw_ref[0:1, cols] + cur * cw_ref[1:2, cols] + nxt * cw_ref[2:3, cols] + cb_ref[:, cols]

    psg = _dot(hb, w_ref[:, HY_IN:HY_IN + SG_IN])
    ge = 0.5 * psg * (1.0 + lax.erf(psg * np.float32(math.sqrt(0.5))))
    u = ge[:, :SG_WIDTH]
    vv = ge[:, SG_WIDTH:]
    vn = (vv * lax.rsqrt(jnp.mean(vv * vv, axis=-1, keepdims=True) + EPS) * sgg_ref[...]).astype(BF16)
    head_of_lane = lax.broadcasted_iota(jnp.int32, (CHUNK, SG_WIDTH), 1) // (SG_WIDTH // SG_HEADS)
    for c in range(tm // CHUNK):
        rows = slice(c * CHUNK, (c + 1) * CHUNK)
        r = _dot(sgw_ref[...], vn[rows])
        mixed = sgb_ref[...]
        for h in range(SG_HEADS):
            mixed = mixed + jnp.where(head_of_lane == h, r[h * CHUNK:(h + 1) * CHUNK], 0.0)
        sg_ref[rows, :] = (u[rows] * mixed).astype(BF16)

    off = HY_IN + SG_IN
    lane = lax.broadcasted_iota(jnp.int32, (1, LANES), 1)
    swap_fwd = (lane % 32) < 16
    cos = cos_ref[...]
    sin = sin_ref[...]
    seg = seg_ref[...]
    pq = _dot(hb, w_ref[:, off:off + DA_WIDTH])
    pk = _dot(hb, w_ref[:, off + DA_WIDTH:off + 2 * DA_WIDTH])
    v_ref[...] = _dot(hb, w_ref[:, off + 2 * DA_WIDTH:]).astype(BF16)
    for h in range(DA_HEADS):
        cols = slice(h * LANES, (h + 1) * LANES)
        q = _norm_rope(pq[:, cols], qg_ref[...], seg, cos, sin, swap_fwd)
        q_ref[:, cols] = (q * Q_SCALE).astype(BF16)
        k_ref[:, cols] = _norm_rope(pk[:, cols], kg_ref[...], seg, cos, sin, swap_fwd).astype(BF16)


def _inproj(x2d, seq, layer, g, sc, sh, w, conv_w, conv_b, cos, sin, qg, kg, seg, sgg, sgw, sgb):
    m, d = x2d.shape
    n = w.shape[2]
    tm = min(INPROJ_TM, seq)
    per = seq // tm
    halo = 8
    row = lambda i: (i, 0)
    fixed = lambda i: (0, 0)
    bat = lambda i: (i // per, 0, 0)
    pos = lambda i: (i % per, 0)
    outs = [(HY_WIDTH, F32)] * 3 + [(SG_WIDTH, BF16)] + [(DA_WIDTH, BF16)] * 3
    return pl.pallas_call(
        functools.partial(_inproj_kernel, per=per),
        grid=(m // tm,),
        in_specs=[pl.BlockSpec((tm, d), row),
                  pl.BlockSpec((halo, d), lambda i: (jnp.maximum(i * (tm // halo) - 1, 0), 0)),
                  pl.BlockSpec((halo, d), lambda i: (jnp.minimum((i + 1) * (tm // halo), m // halo - 1), 0)),
                  pl.BlockSpec((1, d), fixed),
                  pl.BlockSpec((None, 1, d), bat),
                  pl.BlockSpec((None, 1, d), bat),
                  pl.BlockSpec((None, d, n), lambda i: (layer, 0, 0)),
                  pl.BlockSpec((3, HY_IN), fixed),
                  pl.BlockSpec((1, HY_IN), fixed),
                  pl.BlockSpec((tm, LANES), pos),
                  pl.BlockSpec((tm, LANES), pos),
                  pl.BlockSpec((1, LANES), fixed),
                  pl.BlockSpec((1, LANES), fixed),
                  pl.BlockSpec((2 * LANES, LANES), fixed),
                  pl.BlockSpec((1, SG_WIDTH), fixed),
                  pl.BlockSpec((SG_HEADS * CHUNK, CHUNK), fixed),
                  pl.BlockSpec((CHUNK, SG_WIDTH), fixed)],
        out_specs=[pl.BlockSpec((tm, c), row) for c, _ in outs],
        out_shape=[jax.ShapeDtypeStruct((m, c), dt) for c, dt in outs],
        compiler_params=_cparams("parallel"),
        name="inproj",
    )(x2d, x2d, x2d, g, sc, sh, w, conv_w, conv_b, cos, sin, qg, kg, seg, sgg, sgw, sgb)


_NT = (((1,), (1,)), ((), ()))


def _lambda(lp_ref, lam_init):
    lp = lp_ref[...]
    return (jnp.exp(jnp.sum(lp[0:1] * lp[1:2], axis=-1, keepdims=True))
            - jnp.exp(jnp.sum(lp[2:3] * lp[3:4], axis=-1, keepdims=True)) + lam_init)


def _stacked_components(q):
    lane = lax.broadcasted_iota(jnp.int32, (1, LANES), 1)
    zero = jnp.zeros_like(q)
    return jnp.concatenate([jnp.where(lane < DA_HEAD_DIM, q, zero), jnp.where(lane >= DA_HEAD_DIM, q, zero)],
                           axis=0)


def _attn_finish(o1, o2, lam, g_ref, o_ref, lam_init):
    o = o1 - lam * o2
    o = o * lax.rsqrt(jnp.mean(o * o, axis=-1, keepdims=True) + EPS) * g_ref[...]
    o_ref[...] = (o * (1.0 - lam_init)).astype(BF16)


def _attn_bounded_kernel(lp_ref, g_ref, q_ref, *refs, tiles, lam_init):
    tq = q_ref.shape[0]
    o_ref = refs[-1]
    qt = q_ref[...].T
    dim = lax.broadcasted_iota(jnp.int32, (LANES, 1), 0)
    zero = jnp.zeros_like(qt)
    qst = jnp.concatenate([jnp.where(dim < DA_HEAD_DIM, qt, zero), jnp.where(dim >= DA_HEAD_DIM, qt, zero)],
                          axis=1)
    acc = jnp.zeros((DA_V_DIM + 16, 2 * tq), F32)
    for j, tk in enumerate(tiles):
        k_ref, v_ref = refs[2 * j], refs[2 * j + 1]
        ones = jnp.ones((16, tk), BF16)
        for i in range(k_ref.shape[0] // tk):
            st = _dot(k_ref[i * tk:(i + 1) * tk, :], qst)
            vt = jnp.concatenate([v_ref[i * tk:(i + 1) * tk, :].T, ones], axis=0)
            acc = acc + _dot(vt, jnp.exp2(st).astype(BF16))
    ot = acc[:DA_V_DIM] / acc[DA_V_DIM:DA_V_DIM + 1]
    dt = ot[:, :tq] - _lambda(lp_ref, lam_init) * ot[:, tq:]
    dt = dt * lax.rsqrt(jnp.mean(dt * dt, axis=0, keepdims=True) + EPS)
    o_ref[...] = (dt.T * g_ref[...] * (1.0 - lam_init)).astype(BF16)


def _attn_online_kernel(lp_ref, g_ref, q_ref, *refs, tiles, lam_init):
    tq = q_ref.shape[0]
    o_ref = refs[-1]
    qs = _stacked_components(q_ref[...])
    carry = (jnp.full((2 * tq, 1), NEG_BIG, F32), jnp.zeros((2 * tq, 1), F32),
             jnp.zeros((2 * tq, DA_V_DIM), F32))
    for j, tk in enumerate(tiles):
        k_ref, v_ref = refs[2 * j], refs[2 * j + 1]

        def body(i, carry, k_ref=k_ref, v_ref=v_ref, tk=tk):
            m, l, acc = carry
            start = pl.multiple_of(i * tk, tk)
            kb = k_ref[pl.ds(start, tk), :]
            vb = v_ref[pl.ds(start, tk), :]
            s = lax.dot_general(qs, kb, _NT, preferred_element_type=F32)
            m_new = jnp.maximum(m, jnp.max(s, axis=-1, keepdims=True))
            alpha = jnp.exp2(m - m_new)
            p = jnp.exp2(s - m_new)
            l_new = alpha * l + jnp.sum(p, axis=-1, keepdims=True)
            return m_new, l_new, alpha * acc + _dot(p.astype(BF16), vb)

        carry = lax.fori_loop(0, k_ref.shape[0] // tk, body, carry)
    _, l, acc = carry
    o = acc / l
    _attn_finish(o[:tq], o[tq:], _lambda(lp_ref, lam_init), g_ref, o_ref, lam_init)


def _pick_tile(n, candidates):
    for c in candidates:
        if n % c == 0:
            return c
    raise ValueError(f"no tile for {n}")


BOUNDED_SCORE_LIMIT = 56.0


def _attention(q, kv, lam_p, subln_g, lam_init, qn_g, kn_g):
    bound = (math.sqrt(DA_HEAD_DIM) * math.log2(math.e) * 1.02) * jnp.max(jnp.abs(qn_g)) * jnp.max(jnp.abs(kn_g))
    flat = [a for pair in kv for a in pair]
    return lax.cond(bound <= BOUNDED_SCORE_LIMIT,
                    functools.partial(_attention_call, _attn_bounded_kernel, lam_init),
                    functools.partial(_attention_call, _attn_online_kernel, lam_init),
                    q, lam_p, subln_g, *flat)


def _attention_call(body, lam_init, q, lam_p, subln_g, *kv):
    b, lq, _ = q.shape
    tq = _pick_tile(lq, (ATTN_TQ, 512, 256, 128))
    tiles = tuple(_pick_tile(a.shape[1], (ATTN_TK, 512, 256, 128)) for a in kv[::2])
    whole = lambda a: pl.BlockSpec((None, a.shape[1], LANES), lambda bi, h, i: (bi, 0, h))
    return pl.pallas_call(
        functools.partial(body, tiles=tiles, lam_init=lam_init),
        grid=(b, DA_HEADS, lq // tq),
        in_specs=[pl.BlockSpec((4, DA_HEAD_DIM), lambda bi, h, i: (0, 0)),
                  pl.BlockSpec((1, DA_V_DIM), lambda bi, h, i: (0, 0)),
                  pl.BlockSpec((None, tq, LANES), lambda bi, h, i: (bi, i, h))] + [whole(a) for a in kv],
        out_specs=pl.BlockSpec((None, tq, LANES), lambda bi, h, i: (bi, i, h)),
        out_shape=jax.ShapeDtypeStruct((b, lq, DA_WIDTH), BF16),
        compiler_params=_cparams("parallel", "parallel", "parallel"),
        name="diff_attention",
    )(lam_p, subln_g, q, *kv)


def _filter_kernel(f_ref, w1_ref, b1_ref, w2_ref, b2_ref, w3_ref, fr_ref, dl_ref, o_ref, *, seq):
    tl = f_ref.shape[0]
    half = LANES // 2
    nout = HY_ORDER * HY_WIDTH
    feats = f_ref[...]
    h = jnp.sin(fr_ref[0:1, :] * (_dot(feats.astype(BF16), w1_ref[...]) + b1_ref[...]))
    h = jnp.sin(fr_ref[1:2, :] * (_dot(h.astype(BF16), w2_ref[...]) + b2_ref[...]))
    h = _dot(h.astype(BF16), w3_ref[...])
    row = pl.program_id(0) * tl + lax.broadcasted_iota(jnp.int32, (tl, 1), 0)
    o_ref[0] = h[:, :nout] * jnp.exp(-feats[:, 0:1] * dl_ref[...])
    bwd = h[:, nout:] * jnp.exp(-feats[:, half:half + 1] * dl_ref[...])
    o_ref[1] = jnp.where(row == 0, 0.0, bwd)


def _hyena_features(seq):
    t = np.linspace(0.0, 1.0, seq, dtype=np.float32)[:, None]
    bands = np.linspace(1e-4, HY_POS_BANDS - 1, HY_POS_BANDS, dtype=np.float32)
    ang = np.float32(2.0 * math.pi / seq) * np.arange(seq, dtype=np.float32)[:, None] * bands[None, :]
    feats = np.concatenate([t, np.cos(ang), -np.sin(ang)], axis=-1).astype(np.float32)
    back = np.concatenate([feats[:1], feats[:0:-1]], axis=0)
    pad = ((0, 0), (0, LANES // 2 - feats.shape[1]))
    return jnp.asarray(np.concatenate([np.pad(feats, pad), np.pad(back, pad)], axis=1))


def _hyena_filters(seq, w1, b1, w2, b2, w3, freq):
    feats = _hyena_features(seq)
    nfeat, hid = w1.shape
    half = LANES // 2
    nout = HY_ORDER * HY_WIDTH
    assert nfeat <= half and hid <= half
    w1p = jnp.zeros((LANES, LANES), F32).at[:nfeat, :hid].set(w1).at[half:half + nfeat, half:half + hid].set(w1)
    w2p = jnp.zeros((LANES, LANES), F32).at[:hid, :hid].set(w2).at[half:half + hid, half:half + hid].set(w2)
    w3d = w3.reshape(hid, HY_ORDER, 2, HY_WIDTH)
    w3p = (jnp.zeros((LANES, 2 * nout), F32)
           .at[:hid, :nout].set(w3d[:, :, 0].reshape(hid, nout))
           .at[half:half + hid, nout:].set(w3d[:, :, 1].reshape(hid, nout)))
    both = lambda a: jnp.tile(jnp.pad(a.reshape(-1, hid), ((0, 0), (0, half - hid))), (1, 2))
    min_decay = math.log(HY_DECAY_TARGET) / HY_SLOW_DECAY_PCT
    max_decay = math.log(HY_DECAY_TARGET) / HY_FAST_DECAY_PCT
    deltas = jnp.abs(jnp.linspace(min_decay, max_decay, HY_WIDTH, dtype=F32))[None, :]
    tl = min(512, seq)
    fixed = lambda i: (0, 0)
    kt = pl.pallas_call(
        functools.partial(_filter_kernel, seq=seq),
        grid=(seq // tl,),
        in_specs=[pl.BlockSpec((tl, LANES), lambda i: (i, 0)),
                  pl.BlockSpec((LANES, LANES), fixed), pl.BlockSpec((1, LANES), fixed),
                  pl.BlockSpec((LANES, LANES), fixed), pl.BlockSpec((1, LANES), fixed),
                  pl.BlockSpec((LANES, 2 * nout), fixed), pl.BlockSpec((2, LANES), fixed),
                  pl.BlockSpec((1, nout), fixed)],
        out_specs=pl.BlockSpec((2, tl, nout), lambda i: (0, i, 0)),
        out_shape=jax.ShapeDtypeStruct((2, seq, nout), F32),
        compiler_params=_cparams("parallel"),
        name="hyena_filter_mlp",
    )(feats, w1p.astype(BF16), both(b1), w2p.astype(BF16), both(b2), w3p.astype(BF16), both(freq),
      jnp.tile(deltas, (1, HY_ORDER)))
    return kt.reshape(2 * seq, nout)


def _dft_tables(seq):
    n = 2 * seq
    n1 = n // DFT_N2
    idx1 = np.arange(n1)
    idx2 = np.arange(DFT_N2)
    f1 = np.exp(-2j * np.pi * np.outer(idx1, idx1) / n1)
    k = idx1[:, None, None] + n1 * idx2[None, :, None]
    g = np.exp(-2j * np.pi * (k * idx2[None, None, :] % n) / n)
    return n1, f1, g


def _rows_block(nin):
    s_mm = max(8, LANES // nin)
    return s_mm, max(16, s_mm)


def _kron_rows(mat, s):
    return jnp.asarray(np.kron(mat, np.eye(s)), F32).astype(BF16)


def _rows_real_kernel(u_ref, kr_ref, ki_ref, ar_ref, ai_ref, *, s_mm):
    nin, s_blk, c = u_ref.shape
    n1 = ar_ref.shape[0]
    re, im = [], []
    for h in range(s_blk // s_mm):
        u = u_ref[:, h * s_mm:(h + 1) * s_mm, :].reshape(nin * s_mm, c).astype(BF16)
        re.append(_dot(kr_ref[...], u).reshape(n1, s_mm, c))
        im.append(_dot(ki_ref[...], u).reshape(n1, s_mm, c))
    ar_ref[...] = jnp.concatenate(re, axis=1).astype(BF16)
    ai_ref[...] = jnp.concatenate(im, axis=1).astype(BF16)


def _rows_real(u, seq):
    rows, c = u.shape
    n1, f1, _ = _dft_tables(seq)
    nin = rows // DFT_N2
    s_mm, s_blk = _rows_block(nin)
    fixed = lambda j: (0, 0)
    out_spec = pl.BlockSpec((n1, s_blk, c), lambda j: (0, j, 0))
    out_shape = jax.ShapeDtypeStruct((n1, DFT_N2, c), BF16)
    return pl.pallas_call(
        functools.partial(_rows_real_kernel, s_mm=s_mm),
        grid=(DFT_N2 // s_blk,),
        in_specs=[pl.BlockSpec((nin, s_blk, c), lambda j: (0, j, 0)),
                  pl.BlockSpec((n1 * s_mm, nin * s_mm), fixed),
                  pl.BlockSpec((n1 * s_mm, nin * s_mm), fixed)],
        out_specs=[out_spec, out_spec],
        out_shape=[out_shape, out_shape],
        compiler_params=_cparams("parallel"),
        name="hyena_filter_dft_rows",
    )(u.reshape(nin, DFT_N2, c), _kron_rows(f1.real[:, :nin], s_mm), _kron_rows(f1.imag[:, :nin], s_mm))


def _rows_pair_kernel(u_ref, kc_ref, ar_ref, ai_ref, *, s_mm):
    _, nin, s_blk, c = u_ref.shape
    m = kc_ref.shape[0] // 2
    re, im = [], []
    for h in range(s_blk // s_mm):
        sub = slice(h * s_mm, (h + 1) * s_mm)
        p = _dot(kc_ref[...], u_ref[0, :, sub, :].reshape(nin * s_mm, c).astype(BF16))
        q = _dot(kc_ref[...], u_ref[1, :, sub, :].reshape(nin * s_mm, c).astype(BF16))
        re.append((p[:m] - q[m:]).reshape(m // s_mm, s_mm, c))
        im.append((q[:m] + p[m:]).reshape(m // s_mm, s_mm, c))
    ar_ref[...] = jnp.concatenate(re, axis=1).astype(BF16)
    ai_ref[...] = jnp.concatenate(im, axis=1).astype(BF16)


def _rows_pair(u, seq):
    b, rows, c = u.shape
    n1, f1, _ = _dft_tables(seq)
    nin = rows // DFT_N2
    s_mm, s_blk = _rows_block(nin)
    kcat = jnp.concatenate([_kron_rows(f1.real[:, :nin], s_mm), _kron_rows(f1.imag[:, :nin], s_mm)], axis=0)
    out_spec = pl.BlockSpec((None, n1, s_blk, c), lambda p, j: (p, 0, j, 0))
    out_shape = jax.ShapeDtypeStruct((b // 2, n1, DFT_N2, c), BF16)
    return pl.pallas_call(
        functools.partial(_rows_pair_kernel, s_mm=s_mm),
        grid=(b // 2, DFT_N2 // s_blk),
        in_specs=[pl.BlockSpec((2, nin, s_blk, c), lambda p, j: (p, 0, j, 0)),
                  pl.BlockSpec(kcat.shape, lambda p, j: (0, 0))],
        out_specs=[out_spec, out_spec],
        out_shape=[out_shape, out_shape],
        compiler_params=_cparams("parallel", "parallel"),
        name="hyena_dft_rows",
    )(u.reshape(b, nin, DFT_N2, c), kcat)


def _complex_apply(mcat, xr, xi):
    p = _dot(mcat, xr)
    q = _dot(mcat, xi)
    return p[:DFT_N2] - q[DFT_N2:], q[:DFT_N2] + p[DFT_N2:]


def _spectrum_kernel(ar_ref, ai_ref, g_ref, xr_ref, xi_ref, *, scale):
    for j in range(ar_ref.shape[0]):
        xr, xi = _complex_apply(g_ref[j], ar_ref[j], ai_ref[j])
        xr_ref[j] = (xr * scale).astype(BF16)
        xi_ref[j] = (xi * scale).astype(BF16)


def _slab_conv_kernel(ar_ref, ai_ref, kr_ref, ki_ref, g_ref, h_ref, dr_ref, di_ref):
    for j in range(ar_ref.shape[0]):
        xr, xi = _complex_apply(g_ref[j], ar_ref[j], ai_ref[j])
        kr = kr_ref[j].astype(F32)
        ki = ki_ref[j].astype(F32)
        zr = (xr * kr - xi * ki).astype(BF16)
        zi = (xr * ki + xi * kr).astype(BF16)
        dr, di = _complex_apply(h_ref[j], zr, zi)
        dr_ref[j] = dr.astype(BF16)
        di_ref[j] = di.astype(BF16)


def _slab_consts(seq):
    n1, _, g = _dft_tables(seq)
    gcat = jnp.asarray(np.concatenate([g.real, g.imag], axis=1), F32).astype(BF16)
    gt = np.conj(g).transpose(0, 2, 1)
    hcat = jnp.asarray(np.concatenate([gt.real, gt.imag], axis=1), F32).astype(BF16)
    return n1, gcat, hcat


def _spectrum(ar, ai, seq):
    n1, gcat, _ = _slab_consts(seq)
    c = ar.shape[-1]
    g = min(SLABS_PER_STEP, n1)
    slab = pl.BlockSpec((g, DFT_N2, c), lambda i: (i, 0, 0))
    return pl.pallas_call(
        functools.partial(_spectrum_kernel, scale=1.0 / (2 * seq)),
        grid=(n1 // g,),
        in_specs=[slab, slab, pl.BlockSpec((g, 2 * DFT_N2, DFT_N2), lambda i: (i, 0, 0))],
        out_specs=[slab, slab],
        out_shape=[jax.ShapeDtypeStruct((n1, DFT_N2, c), BF16)] * 2,
        compiler_params=_cparams("parallel"),
        name="hyena_filter_spectrum",
    )(ar, ai, gcat)


def _slab_conv(ar, ai, kr, ki, order, seq):
    n1, gcat, hcat = _slab_consts(seq)
    npair, _, _, c = ar.shape
    g = min(SLABS_PER_STEP, n1)
    slab = pl.BlockSpec((None, g, DFT_N2, c), lambda p, i: (p, i, 0, 0))
    filt = pl.BlockSpec((g, DFT_N2, c), lambda p, i: (i, 0, order))
    mats = pl.BlockSpec((g, 2 * DFT_N2, DFT_N2), lambda p, i: (i, 0, 0))
    return pl.pallas_call(
        _slab_conv_kernel,
        grid=(npair, n1 // g),
        in_specs=[slab, slab, filt, filt, mats, mats],
        out_specs=[slab, slab],
        out_shape=[jax.ShapeDtypeStruct(ar.shape, BF16)] * 2,
        compiler_params=_cparams("parallel", "parallel"),
        name="hyena_dft_slabs",
    )(ar, ai, kr, ki, gcat, hcat)


def _rows_inverse_kernel(dr_ref, di_ref, kc_ref, u_ref, gate_ref, bias_ref, o_ref, *, s_mm):
    n1, s_blk, c = dr_ref.shape
    m = kc_ref.shape[0] // 2
    dr = dr_ref[...].astype(F32)
    di = di_ref[...].astype(F32)
    y0, y1 = [], []
    for h in range(s_blk // s_mm):
        sub = slice(h * s_mm, (h + 1) * s_mm)
        p = _dot(kc_ref[...], dr[:, sub, :].reshape(n1 * s_mm, c).astype(BF16))
        q = _dot(kc_ref[...], di[:, sub, :].reshape(n1 * s_mm, c).astype(BF16))
        y0.append((p[:m] + q[m:]).reshape(m // s_mm, s_mm, c))
        y1.append((q[:m] - p[m:]).reshape(m // s_mm, s_mm, c))
    for e, parts in enumerate((y0, y1)):
        y = jnp.concatenate(parts, axis=1)
        o_ref[e] = (gate_ref[e] * (y + u_ref[e] * bias_ref[...])).astype(o_ref.dtype)


def _rows_inverse(dr, di, u, gate, bias, seq, out_dtype):
    npair, n1, _, c = dr.shape
    _, f1, _ = _dft_tables(seq)
    nout = seq // DFT_N2
    s_mm, s_blk = _rows_block(nout)
    kcat = jnp.concatenate([_kron_rows(f1.real[:nout], s_mm), _kron_rows(f1.imag[:nout], s_mm)], axis=0)
    dspec = pl.BlockSpec((None, n1, s_blk, c), lambda p, j: (p, 0, j, 0))
    uspec = pl.BlockSpec((2, nout, s_blk, c), lambda p, j: (p, 0, j, 0))
    y = pl.pallas_call(
        functools.partial(_rows_inverse_kernel, s_mm=s_mm),
        grid=(npair, DFT_N2 // s_blk),
        in_specs=[dspec, dspec, pl.BlockSpec(kcat.shape, lambda p, j: (0, 0)),
                  uspec, uspec, pl.BlockSpec((1, 1, c), lambda p, j: (0, 0, 0))],
        out_specs=uspec,
        out_shape=jax.ShapeDtypeStruct((2 * npair, nout, DFT_N2, c), out_dtype),
        compiler_params=_cparams("parallel", "parallel"),
        name="hyena_idft_rows",
    )(dr, di, kcat, u.reshape(2 * npair, nout, DFT_N2, c), gate.reshape(2 * npair, nout, DFT_N2, c),
      bias.reshape(1, 1, c))
    return y.reshape(2 * npair, seq, c)


def _hyena(x1, x2, v, w1, b1, w2, b2, w3, freq, bias):
    b, seq, _ = v.shape
    assert b % 2 == 0, "batch elements are transformed in pairs"
    kt = _hyena_filters(seq, w1, b1, w2, b2, w3, freq)
    kr, ki = _spectrum(*_rows_real(kt, seq), seq)
    dr, di = _slab_conv(*_rows_pair(v, seq), kr, ki, 0, seq)
    z = _rows_inverse(dr, di, v, x1, bias[0], seq, F32)
    dr, di = _slab_conv(*_rows_pair(z, seq), kr, ki, 1, seq)
    return _rows_inverse(dr, di, z, x2, bias[1], seq, BF16)


def _mix_ffn_kernel(x_ref, hy_ref, sg_ref, at_ref, wo_ref, g1_ref, n2_ref, sc_ref, sh_ref, g2_ref, w1_ref,
                    w2_ref, o_ref, *, chunk):
    mix = _dot(hy_ref[...], wo_ref[:HY_WIDTH])
    mix = mix + _dot(sg_ref[...], wo_ref[HY_WIDTH:HY_WIDTH + SG_WIDTH])
    mix = mix + _dot(at_ref[...], wo_ref[HY_WIDTH + SG_WIDTH:])
    x = x_ref[...] + g1_ref[...] * mix
    hb = _modulated_norm(x, n2_ref[...], sc_ref[...], sh_ref[...]).astype(BF16)
    hidden = w2_ref.shape[0]
    acc = jnp.zeros(x.shape, F32)
    for c in range(hidden // chunk):
        a = _dot(hb, w1_ref[:, c * chunk:(c + 1) * chunk])
        up = _dot(hb, w1_ref[:, hidden + c * chunk:hidden + (c + 1) * chunk])
        act = (a * _sigmoid(a) * up).astype(BF16)
        acc = acc + _dot(act, w2_ref[c * chunk:(c + 1) * chunk, :])
    o_ref[...] = x + g2_ref[...] * acc


def _mix_ffn(x2d, seq, layer, hy, sg, att, w_out, g1, n2, sc, sh, g2, w1, w2):
    m, d = x2d.shape
    tm = min(FFN_TM, seq)
    per = seq // tm
    row = lambda i: (i, 0)
    fixed = lambda i: (0, 0)
    bat = lambda i: (i // per, 0, 0)
    resident = lambda a: pl.BlockSpec((None,) + a.shape[1:], lambda i: (layer, 0, 0), pipeline_mode=pl.Buffered(1))
    return pl.pallas_call(
        functools.partial(_mix_ffn_kernel, chunk=FFN_CHUNK),
        grid=(m // tm,),
        in_specs=[pl.BlockSpec((tm, d), row),
                  pl.BlockSpec((tm, HY_WIDTH), row),
                  pl.BlockSpec((tm, SG_WIDTH), row),
                  pl.BlockSpec((tm, DA_WIDTH), row),
                  resident(w_out),
                  pl.BlockSpec((None, 1, d), bat),
                  pl.BlockSpec((1, d), fixed),
                  pl.BlockSpec((None, 1, d), bat),
                  pl.BlockSpec((None, 1, d), bat),
                  pl.BlockSpec((None, 1, d), bat),
                  resident(w1),
                  resident(w2)],
        out_specs=pl.BlockSpec((tm, d), row),
        out_shape=jax.ShapeDtypeStruct((m, d), F32),
        compiler_params=_cparams("parallel"),
        name="outproj_ffn",
    )(x2d, hy, sg, att, w_out, g1, n2, sc, sh, g2, w1, w2)


def _rope_tables(seq):
    t = np.arange(seq)
    half = DA_HEAD_DIM // 4
    inv_freq = (np.float32(ROPE_BASE) ** (-np.arange(half, dtype=np.float32) / np.float32(half))).astype(np.float32)
    ang_row = (t // GRID_W).astype(np.float32)[:, None] * inv_freq[None, :]
    ang_col = (t % GRID_W).astype(np.float32)[:, None] * inv_freq[None, :]
    cos = np.concatenate([np.cos(ang_row)] * 2 + [np.cos(ang_col)] * 2, axis=-1)
    sin = np.concatenate([-np.sin(ang_row), np.sin(ang_row), -np.sin(ang_col), np.sin(ang_col)], axis=-1)
    return jnp.asarray(np.tile(cos, (1, 2)), F32), jnp.asarray(np.tile(sin, (1, 2)), F32)


def kernel(x, c, ctx, c_ctx, norm1_g, norm2_g, ada_w, ada_b, w_in, hy_conv_w, hy_conv_b, hy_w1, hy_b1,
           hy_w2, hy_b2, hy_w3, hy_freq, hy_bias, sg_norm_g, sg_w, sg_b, qn_g, kn_g, lam_p, subln_g,
           w_out, ffn_w1, ffn_w2):
    batch, seq, d = x.shape
    ctx_len = ctx.shape[1]
    depth = w_in.shape[0]
    assert batch + 1 <= 8 and seq % CHUNK == 0 and ctx_len % CHUNK == 0

    cvecs = jnp.zeros((8, d), F32).at[:batch].set(c).at[batch].set(c_ctx)
    mods = _ada(cvecs, ada_w, ada_b[:, None, :])

    cos_l, sin_l = _rope_tables(seq)
    cos_c = jnp.ones((ctx_len, LANES), F32)
    sin_c = jnp.zeros((ctx_len, LANES), F32)
    seg_np = np.kron(np.eye(LANES // DA_HEAD_DIM), np.full((DA_HEAD_DIM, DA_HEAD_DIM), 1.0 / DA_HEAD_DIM))
    seg = jnp.asarray(np.concatenate([seg_np, seg_np], axis=0), F32).astype(BF16)

    w_in_b = w_in.astype(BF16)
    w_out_b = w_out.astype(BF16)
    w1_b = ffn_w1.astype(BF16)
    w2_b = ffn_w2.astype(BF16)
    xs = x.reshape(batch * seq, d)
    cs = ctx.reshape(batch * ctx_len, d)
    for i in range(depth):
        last = i == depth - 1
        lam_init = 0.8 - 0.6 * math.exp(-0.3 * i)
        mod_l = [m[:, None, :] for m in jnp.split(mods[i, :batch], 6, axis=-1)]
        mod_c = [jnp.broadcast_to(m[None], (batch, 1, d)) for m in jnp.split(mods[i, batch:batch + 1], 6, axis=-1)]
        qg = jnp.tile(qn_g[i], 2)[None, :]
        kg = jnp.tile(kn_g[i], 2)[None, :]
        sgw = sg_w[i].reshape(SG_HEADS * CHUNK, CHUNK).astype(BF16)
        sgb = jnp.repeat(sg_b[i].T, SG_WIDTH // SG_HEADS, axis=1)
        hy_params = (hy_w1[i], hy_b1[i], hy_w2[i], hy_b2[i], hy_w3[i], hy_freq[i], hy_bias[i])

        def project(tokens, n_tok, mod, cos, sin):
            outs = _inproj(tokens, n_tok, i, norm1_g[i][None, :], mod[1], mod[0], w_in_b, hy_conv_w[i],
                           hy_conv_b[i][None, :], cos, sin, qg, kg, seg, sg_norm_g[i][None, :], sgw, sgb)
            hy = [a.reshape(batch, n_tok, HY_WIDTH) for a in outs[:3]]
            return [hy, outs[3]] + [a.reshape(batch, n_tok, DA_WIDTH) for a in outs[4:]]

        def finish(tokens, n_tok, mod, hy_parts, sg, att):
            hy = _hyena(*hy_parts, *hy_params).reshape(batch * n_tok, HY_WIDTH)
            return _mix_ffn(tokens, n_tok, i, hy, sg, att.reshape(batch * n_tok, DA_WIDTH), w_out_b, mod[2],
                            norm2_g[i][None, :], mod[4], mod[3], mod[5], w1_b, w2_b)

        hy_l, sg_l, q_l, k_l, v_l = project(xs, seq, mod_l, cos_l, sin_l)
        hy_c, sg_c, q_c, k_c, v_c = project(cs, ctx_len, mod_c, cos_c, sin_c)
        attend = functools.partial(_attention, lam_p=lam_p[i], subln_g=subln_g[i][None, :], lam_init=lam_init,
                                   qn_g=qn_g[i], kn_g=kn_g[i])
        att_l = attend(q_l, [(k_l, v_l), (k_c, v_c)])
        xs = finish(xs, seq, mod_l, hy_l, sg_l, att_l)
        if not last:
            cs = finish(cs, ctx_len, mod_c, hy_c, sg_c, attend(q_c, [(k_c, v_c)]))
    return xs.reshape(batch, seq, d)
```

```python
import functools
import math

import numpy as np
import jax
import jax.numpy as jnp
from jax import lax
from jax.experimental import pallas as pl
from jax.experimental.pallas import tpu as pltpu

F32 = jnp.float32
BF16 = jnp.bfloat16

GRID_W = 64
EPS = 1e-6
HY_WIDTH = 256
HY_ORDER = 2
HY_POS_BANDS = 16
HY_DECAY_TARGET = 1e-2
HY_FAST_DECAY_PCT = 0.3
HY_SLOW_DECAY_PCT = 1.5
SG_HEADS = 4
SG_WIDTH = 256
CHUNK = 128
DA_HEADS = 4
DA_WIDTH = 512
DA_V_DIM = 128
DA_HEAD_DIM = 64
ROPE_BASE = 10000.0
HY_IN = 3 * HY_WIDTH
SG_IN = 2 * SG_WIDTH
LANES = 128
DFT_N2 = 128
SLABS_PER_STEP = 16
VMEM_LIMIT = 56 * 1024 * 1024
NEG_BIG = -1e30
Q_SCALE = math.log2(math.e) * DA_HEAD_DIM ** -0.5
INPROJ_TM = 1024
FFN_TM = 512
FFN_CHUNK = 256
ATTN_TQ = 1024
ATTN_TK = 1024


def _cparams(*sem):
    return pltpu.CompilerParams(dimension_semantics=sem, vmem_limit_bytes=VMEM_LIMIT)


def _dot(a, b):
    return jnp.dot(a, b, preferred_element_type=F32)


def _sigmoid(x):
    return 1.0 / (1.0 + jnp.exp(-x))


def _modulated_norm(x, g, sc, sh):
    y = x * lax.rsqrt(jnp.mean(x * x, axis=-1, keepdims=True) + EPS)
    return (y * g) * (1.0 + sc) + sh


def _ada_kernel(c_ref, w_ref, b_ref, o_ref):
    cv = c_ref[...]
    s = (cv * _sigmoid(cv)).astype(BF16)
    o_ref[...] = _dot(s, w_ref[...].astype(BF16)) + b_ref[...]


def _ada(cvecs, ada_w, ada_b):
    depth, d, n = ada_w.shape
    tn = 1536
    return pl.pallas_call(
        _ada_kernel,
        grid=(depth, n // tn),
        in_specs=[pl.BlockSpec((8, d), lambda i, j: (0, 0)),
                  pl.BlockSpec((None, d, tn), lambda i, j: (i, 0, j)),
                  pl.BlockSpec((None, 1, tn), lambda i, j: (i, 0, j))],
        out_specs=pl.BlockSpec((None, 8, tn), lambda i, j: (i, 0, j)),
        out_shape=jax.ShapeDtypeStruct((depth, 8, n), F32),
        compiler_params=_cparams("parallel", "parallel"),
        name="adaln",
    )(cvecs, ada_w, ada_b)


def _group_mean_sq(t, seg):
    sq = t * t
    hi = sq.astype(BF16)
    lo = (sq - hi.astype(F32)).astype(BF16)
    return _dot(jnp.concatenate([hi, lo], axis=1), seg)


def _norm_rope(t, gain, seg, cos, sin, swap_fwd):
    tn = t * lax.rsqrt(_group_mean_sq(t, seg) + EPS) * gain
    rot = jnp.where(swap_fwd, pltpu.roll(tn, LANES - 16, 1), pltpu.roll(tn, 16, 1))
    return tn * cos + rot * sin


def _inproj_kernel(x_ref, xp_ref, xn_ref, g_ref, sc_ref, sh_ref, w_ref, cw_ref, cb_ref, cos_ref, sin_ref, qg_ref,
                   kg_ref, seg_ref, sgg_ref, sgw_ref, sgb_ref, hx1_ref, hx2_ref, hxv_ref, sg_ref, q_ref, k_ref, v_ref,
                   *, per):
    tm = x_ref.shape[0]
    hb = _modulated_norm(x_ref[...], g_ref[...], sc_ref[...], sh_ref[...]).astype(BF16)

    halo = jnp.concatenate([xp_ref[...], xn_ref[...]], axis=0)
    hh = _modulated_norm(halo, g_ref[...], sc_ref[...], sh_ref[...]).astype(BF16)
    p_ext = _dot(jnp.concatenate([hb, hh], axis=0), w_ref[:, :HY_IN])
    pos = pl.program_id(0) % per
    before = jnp.where(pos == 0, 0.0, p_ext[tm + 7:tm + 8])
    after = jnp.where(pos == per - 1, 0.0, p_ext[tm + 8:tm + 9])
    rid = lax.broadcasted_iota(jnp.int32, (tm, 1), 0)
    for part, hx_ref in enumerate((hx1_ref, hx2_ref, hxv_ref)):
        cols = slice(part * HY_WIDTH, (part + 1) * HY_WIDTH)
        cur = p_ext[:tm, cols]
        prev = jnp.where(rid == 0, before[:, cols], pltpu.roll(cur, 1, 0))
        nxt = jnp.where(rid == tm - 1, after[:, cols], pltpu.roll(cur, tm - 1, 0))
        hx_ref[...] = prev * cw_ref[0:1, cols] + cur * cw_ref[1:2, cols] + nxt * cw_ref[2:3, cols] + cb_ref[:, cols]

    psg = _dot(hb, w_ref[:, HY_IN:HY_IN + SG_IN])
    ge = 0.5 * psg * (1.0 + lax.erf(psg * np.float32(math.sqrt(0.5))))
    u = ge[:, :SG_WIDTH]
    vv = ge[:, SG_WIDTH:]
    vn = (vv * lax.rsqrt(jnp.mean(vv * vv, axis=-1, keepdims=True) + EPS) * sgg_ref[...]).astype(BF16)
    head_of_lane = lax.broadcasted_iota(jnp.int32, (CHUNK, SG_WIDTH), 1) // (SG_WIDTH // SG_HEADS)
    for c in range(tm // CHUNK):
        rows = slice(c * CHUNK, (c + 1) * CHUNK)
        r = _dot(sgw_ref[...], vn[rows])
        mixed = sgb_ref[...]
        for h in range(SG_HEADS):
            mixed = mixed + jnp.where(head_of_lane == h, r[h * CHUNK:(h + 1) * CHUNK], 0.0)
        sg_ref[rows, :] = (u[rows] * mixed).astype(BF16)

    off = HY_IN + SG_IN
    lane = lax.broadcasted_iota(jnp.int32, (1, LANES), 1)
    swap_fwd = (lane % 32) < 16
    cos = cos_ref[...]
    sin = sin_ref[...]
    seg = seg_ref[...]
    pq = _dot(hb, w_ref[:, off:off + DA_WIDTH])
    pk = _dot(hb, w_ref[:, off + DA_WIDTH:off + 2 * DA_WIDTH])
    v_ref[...] = _dot(hb, w_ref[:, off + 2 * DA_WIDTH:]).astype(BF16)
    for h in range(DA_HEADS):
        cols = slice(h * LANES, (h + 1) * LANES)
        q = _norm_rope(pq[:, cols], qg_ref[...], seg, cos, sin, swap_fwd)
        q_ref[:, cols] = (q * Q_SCALE).astype(BF16)
        k_ref[:, cols] = _norm_rope(pk[:, cols], kg_ref[...], seg, cos, sin, swap_fwd).astype(BF16)


def _inproj(x2d, seq, layer, g, sc, sh, w, conv_w, conv_b, cos, sin, qg, kg, seg, sgg, sgw, sgb):
    m, d = x2d.shape
    n = w.shape[2]
    tm = min(INPROJ_TM, seq)
    per = seq // tm
    halo = 8
    row = lambda i: (i, 0)
    fixed = lambda i: (0, 0)
    bat = lambda i: (i // per, 0, 0)
    pos = lambda i: (i % per, 0)
    outs = [(HY_WIDTH, F32)] * 3 + [(SG_WIDTH, BF16)] + [(DA_WIDTH, BF16)] * 3
    return pl.pallas_call(
        functools.partial(_inproj_kernel, per=per),
        grid=(m // tm,),
        in_specs=[pl.BlockSpec((tm, d), row),
                  pl.BlockSpec((halo, d), lambda i: (jnp.maximum(i * (tm // halo) - 1, 0), 0)),
                  pl.BlockSpec((halo, d), lambda i: (jnp.minimum((i + 1) * (tm // halo), m // halo - 1), 0)),
                  pl.BlockSpec((1, d), fixed),
                  pl.BlockSpec((None, 1, d), bat),
                  pl.BlockSpec((None, 1, d), bat),
                  pl.BlockSpec((None, d, n), lambda i: (layer, 0, 0)),
                  pl.BlockSpec((3, HY_IN), fixed),
                  pl.BlockSpec((1, HY_IN), fixed),
                  pl.BlockSpec((tm, LANES), pos),
                  pl.BlockSpec((tm, LANES), pos),
                  pl.BlockSpec((1, LANES), fixed),
                  pl.BlockSpec((1, LANES), fixed),
                  pl.BlockSpec((2 * LANES, LANES), fixed),
                  pl.BlockSpec((1, SG_WIDTH), fixed),
                  pl.BlockSpec((SG_HEADS * CHUNK, CHUNK), fixed),
                  pl.BlockSpec((CHUNK, SG_WIDTH), fixed)],
        out_specs=[pl.BlockSpec((tm, c), row) for c, _ in outs],
        out_shape=[jax.ShapeDtypeStruct((m, c), dt) for c, dt in outs],
        compiler_params=_cparams("parallel"),
        name="inproj",
    )(x2d, x2d, x2d, g, sc, sh, w, conv_w, conv_b, cos, sin, qg, kg, seg, sgg, sgw, sgb)


_NT = (((1,), (1,)), ((), ()))


def _lambda(lp_ref, lam_init):
    lp = lp_ref[...]
    return (jnp.exp(jnp.sum(lp[0:1] * lp[1:2], axis=-1, keepdims=True))
            - jnp.exp(jnp.sum(lp[2:3] * lp[3:4], axis=-1, keepdims=True)) + lam_init)


def _stacked_components(q):
    lane = lax.broadcasted_iota(jnp.int32, (1, LANES), 1)
    zero = jnp.zeros_like(q)
    return jnp.concatenate([jnp.where(lane < DA_HEAD_DIM, q, zero), jnp.where(lane >= DA_HEAD_DIM, q, zero)],
                           axis=0)


def _attn_finish(o1, o2, lam, g_ref, o_ref, lam_init):
    o = o1 - lam * o2
    o = o * lax.rsqrt(jnp.mean(o * o, axis=-1, keepdims=True) + EPS) * g_ref[...]
    o_ref[...] = (o * (1.0 - lam_init)).astype(BF16)


def _attn_bounded_kernel(lp_ref, g_ref, q_ref, *refs, tiles, lam_init):
    tq = q_ref.shape[0]
    o_ref = refs[-1]
    qt = q_ref[...].T
    dim = lax.broadcasted_iota(jnp.int32, (LANES, 1), 0)
    zero = jnp.zeros_like(qt)
    qst = jnp.concatenate([jnp.where(dim < DA_HEAD_DIM, qt, zero), jnp.where(dim >= DA_HEAD_DIM, qt, zero)],
                          axis=1)
    acc = jnp.zeros((DA_V_DIM + 16, 2 * tq), F32)
    for j, tk in enumerate(tiles):
        k_ref, v_ref = refs[2 * j], refs[2 * j + 1]
        ones = jnp.ones((16, tk), BF16)
        for i in range(k_ref.shape[0] // tk):
            st = _dot(k_ref[i * tk:(i + 1) * tk, :], qst)
            vt = jnp.concatenate([v_ref[i * tk:(i + 1) * tk, :].T, ones], axis=0)
            acc = acc + _dot(vt, jnp.exp2(st).astype(BF16))
    ot = acc[:DA_V_DIM] / acc[DA_V_DIM:DA_V_DIM + 1]
    dt = ot[:, :tq] - _lambda(lp_ref, lam_init) * ot[:, tq:]
    dt = dt * lax.rsqrt(jnp.mean(dt * dt, axis=0, keepdims=True) + EPS)
    o_ref[...] = (dt.T * g_ref[...] * (1.0 - lam_init)).astype(BF16)


def _attn_online_kernel(lp_ref, g_ref, q_ref, *refs, tiles, lam_init):
    tq = q_ref.shape[0]
    o_ref = refs[-1]
    qs = _stacked_components(q_ref[...])
    carry = (jnp.full((2 * tq, 1), NEG_BIG, F32), jnp.zeros((2 * tq, 1), F32),
             jnp.zeros((2 * tq, DA_V_DIM), F32))
    for j, tk in enumerate(tiles):
        k_ref, v_ref = refs[2 * j], refs[2 * j + 1]

        def body(i, carry, k_ref=k_ref, v_ref=v_ref, tk=tk):
            m, l, acc = carry
            start = pl.multiple_of(i * tk, tk)
            kb = k_ref[pl.ds(start, tk), :]
            vb = v_ref[pl.ds(start, tk), :]
            s = lax.dot_general(qs, kb, _NT, preferred_element_type=F32)
            m_new = jnp.maximum(m, jnp.max(s, axis=-1, keepdims=True))
            alpha = jnp.exp2(m - m_new)
            p = jnp.exp2(s - m_new)
            l_new = alpha * l + jnp.sum(p, axis=-1, keepdims=True)
            return m_new, l_new, alpha * acc + _dot(p.astype(BF16), vb)

        carry = lax.fori_loop(0, k_ref.shape[0] // tk, body, carry)
    _, l, acc = carry
    o = acc / l
    _attn_finish(o[:tq], o[tq:], _lambda(lp_ref, lam_init), g_ref, o_ref, lam_init)


def _pick_tile(n, candidates):
    for c in candidates:
        if n % c == 0:
            return c
    raise ValueError(f"no tile for {n}")


BOUNDED_SCORE_LIMIT = 56.0


def _attention(q, kv, lam_p, subln_g, lam_init, qn_g, kn_g):
    bound = (math.sqrt(DA_HEAD_DIM) * math.log2(math.e) * 1.02) * jnp.max(jnp.abs(qn_g)) * jnp.max(jnp.abs(kn_g))
    flat = [a for pair in kv for a in pair]
    return lax.cond(bound <= BOUNDED_SCORE_LIMIT,
                    functools.partial(_attention_call, _attn_bounded_kernel, lam_init),
                    functools.partial(_attention_call, _attn_online_kernel, lam_init),
                    q, lam_p, subln_g, *flat)


def _attention_call(body, lam_init, q, lam_p, subln_g, *kv):
    b, lq, _ = q.shape
    tq = _pick_tile(lq, (ATTN_TQ, 512, 256, 128))
    tiles = tuple(_pick_tile(a.shape[1], (ATTN_TK, 512, 256, 128)) for a in kv[::2])
    whole = lambda a: pl.BlockSpec((None, a.shape[1], LANES), lambda bi, h, i: (bi, 0, h))
    return pl.pallas_call(
        functools.partial(body, tiles=tiles, lam_init=lam_init),
        grid=(b, DA_HEADS, lq // tq),
        in_specs=[pl.BlockSpec((4, DA_HEAD_DIM), lambda bi, h, i: (0, 0)),
                  pl.BlockSpec((1, DA_V_DIM), lambda bi, h, i: (0, 0)),
                  pl.BlockSpec((None, tq, LANES), lambda bi, h, i: (bi, i, h))] + [whole(a) for a in kv],
        out_specs=pl.BlockSpec((None, tq, LANES), lambda bi, h, i: (bi, i, h)),
        out_shape=jax.ShapeDtypeStruct((b, lq, DA_WIDTH), BF16),
        compiler_params=_cparams("parallel", "parallel", "parallel"),
        name="diff_attention",
    )(lam_p, subln_g, q, *kv)


def _filter_kernel(f_ref, w1_ref, b1_ref, w2_ref, b2_ref, w3_ref, fr_ref, dl_ref, o_ref, *, seq):
    tl = f_ref.shape[0]
    half = LANES // 2
    nout = HY_ORDER * HY_WIDTH
    feats = f_ref[...]
    h = jnp.sin(fr_ref[0:1, :] * (_dot(feats.astype(BF16), w1_ref[...]) + b1_ref[...]))
    h = jnp.sin(fr_ref[1:2, :] * (_dot(h.astype(BF16), w2_ref[...]) + b2_ref[...]))
    h = _dot(h.astype(BF16), w3_ref[...])
    row = pl.program_id(0) * tl + lax.broadcasted_iota(jnp.int32, (tl, 1), 0)
    o_ref[0] = h[:, :nout] * jnp.exp(-feats[:, 0:1] * dl_ref[...])
    bwd = h[:, nout:] * jnp.exp(-feats[:, half:half + 1] * dl_ref[...])
    o_ref[1] = jnp.where(row == 0, 0.0, bwd)


def _hyena_features(seq):
    t = np.linspace(0.0, 1.0, seq, dtype=np.float32)[:, None]
    bands = np.linspace(1e-4, HY_POS_BANDS - 1, HY_POS_BANDS, dtype=np.float32)
    ang = np.float32(2.0 * math.pi / seq) * np.arange(seq, dtype=np.float32)[:, None] * bands[None, :]
    feats = np.concatenate([t, np.cos(ang), -np.sin(ang)], axis=-1).astype(np.float32)
    back = np.concatenate([feats[:1], feats[:0:-1]], axis=0)
    pad = ((0, 0), (0, LANES // 2 - feats.shape[1]))
    return jnp.asarray(np.concatenate([np.pad(feats, pad), np.pad(back, pad)], axis=1))


def _hyena_filters(seq, w1, b1, w2, b2, w3, freq):
    feats = _hyena_features(seq)
    nfeat, hid = w1.shape
    half = LANES // 2
    nout = HY_ORDER * HY_WIDTH
    assert nfeat <= half and hid <= half
    w1p = jnp.zeros((LANES, LANES), F32).at[:nfeat, :hid].set(w1).at[half:half + nfeat, half:half + hid].set(w1)
    w2p = jnp.zeros((LANES, LANES), F32).at[:hid, :hid].set(w2).at[half:half + hid, half:half + hid].set(w2)
    w3d = w3.reshape(hid, HY_ORDER, 2, HY_WIDTH)
    w3p = (jnp.zeros((LANES, 2 * nout), F32)
           .at[:hid, :nout].set(w3d[:, :, 0].reshape(hid, nout))
           .at[half:half + hid, nout:].set(w3d[:, :, 1].reshape(hid, nout)))
    both = lambda a: jnp.tile(jnp.pad(a.reshape(-1, hid), ((0, 0), (0, half - hid))), (1, 2))
    min_decay = math.log(HY_DECAY_TARGET) / HY_SLOW_DECAY_PCT
    max_decay = math.log(HY_DECAY_TARGET) / HY_FAST_DECAY_PCT
    deltas = jnp.abs(jnp.linspace(min_decay, max_decay, HY_WIDTH, dtype=F32))[None, :]
    tl = min(512, seq)
    fixed = lambda i: (0, 0)
    kt = pl.pallas_call(
        functools.partial(_filter_kernel, seq=seq),
        grid=(seq // tl,),
        in_specs=[pl.BlockSpec((tl, LANES), lambda i: (i, 0)),
                  pl.BlockSpec((LANES, LANES), fixed), pl.BlockSpec((1, LANES), fixed),
                  pl.BlockSpec((LANES, LANES), fixed), pl.BlockSpec((1, LANES), fixed),
                  pl.BlockSpec((LANES, 2 * nout), fixed), pl.BlockSpec((2, LANES), fixed),
                  pl.BlockSpec((1, nout), fixed)],
        out_specs=pl.BlockSpec((2, tl, nout), lambda i: (0, i, 0)),
        out_shape=jax.ShapeDtypeStruct((2, seq, nout), F32),
        compiler_params=_cparams("parallel"),
        name="hyena_filter_mlp",
    )(feats, w1p.astype(BF16), both(b1), w2p.astype(BF16), both(b2), w3p.astype(BF16), both(freq),
      jnp.tile(deltas, (1, HY_ORDER)))
    return kt.reshape(2 * seq, nout)


def _dft_tables(seq):
    n = 2 * seq
    n1 = n // DFT_N2
    idx1 = np.arange(n1)
    idx2 = np.arange(DFT_N2)
    f1 = np.exp(-2j * np.pi * np.outer(idx1, idx1) / n1)
    k = idx1[:, None, None] + n1 * idx2[None, :, None]
    g = np.exp(-2j * np.pi * (k * idx2[None, None, :] % n) / n)
    return n1, f1, g


def _rows_block(nin):
    s_mm = max(8, LANES // nin)
    return s_mm, max(16, s_mm)


def _kron_rows(mat, s):
    return jnp.asarray(np.kron(mat, np.eye(s)), F32).astype(BF16)


def _rows_real_kernel(u_ref, kr_ref, ki_ref, ar_ref, ai_ref, *, s_mm):
    nin, s_blk, c = u_ref.shape
    n1 = ar_ref.shape[0]
    re, im = [], []
    for h in range(s_blk // s_mm):
        u = u_ref[:, h * s_mm:(h + 1) * s_mm, :].reshape(nin * s_mm, c).astype(BF16)
        re.append(_dot(kr_ref[...], u).reshape(n1, s_mm, c))
        im.append(_dot(ki_ref[...], u).reshape(n1, s_mm, c))
    ar_ref[...] = jnp.concatenate(re, axis=1).astype(BF16)
    ai_ref[...] = jnp.concatenate(im, axis=1).astype(BF16)


def _rows_real(u, seq):
    rows, c = u.shape
    n1, f1, _ = _dft_tables(seq)
    nin = rows // DFT_N2
    s_mm, s_blk = _rows_block(nin)
    fixed = lambda j: (0, 0)
    out_spec = pl.BlockSpec((n1, s_blk, c), lambda j: (0, j, 0))
    out_shape = jax.ShapeDtypeStruct((n1, DFT_N2, c), BF16)
    return pl.pallas_call(
        functools.partial(_rows_real_kernel, s_mm=s_mm),
        grid=(DFT_N2 // s_blk,),
        in_specs=[pl.BlockSpec((nin, s_blk, c), lambda j: (0, j, 0)),
                  pl.BlockSpec((n1 * s_mm, nin * s_mm), fixed),
                  pl.BlockSpec((n1 * s_mm, nin * s_mm), fixed)],
        out_specs=[out_spec, out_spec],
        out_shape=[out_shape, out_shape],
        compiler_params=_cparams("parallel"),
        name="hyena_filter_dft_rows",
    )(u.reshape(nin, DFT_N2, c), _kron_rows(f1.real[:, :nin], s_mm), _kron_rows(f1.imag[:, :nin], s_mm))


def _rows_pair_kernel(u_ref, kc_ref, ar_ref, ai_ref, *, s_mm):
    _, nin, s_blk, c = u_ref.shape
    m = kc_ref.shape[0] // 2
    re, im = [], []
    for h in range(s_blk // s_mm):
        sub = slice(h * s_mm, (h + 1) * s_mm)
        p = _dot(kc_ref[...], u_ref[0, :, sub, :].reshape(nin * s_mm, c).astype(BF16))
        q = _dot(kc_ref[...], u_ref[1, :, sub, :].reshape(nin * s_mm, c).astype(BF16))
        re.append((p[:m] - q[m:]).reshape(m // s_mm, s_mm, c))
        im.append((q[:m] + p[m:]).reshape(m // s_mm, s_mm, c))
    ar_ref[...] = jnp.concatenate(re, axis=1).astype(BF16)
    ai_ref[...] = jnp.concatenate(im, axis=1).astype(BF16)


def _rows_pair(u, seq):
    b, rows, c = u.shape
    n1, f1, _ = _dft_tables(seq)
    nin = rows // DFT_N2
    s_mm, s_blk = _rows_block(nin)
    kcat = jnp.concatenate([_kron_rows(f1.real[:, :nin], s_mm), _kron_rows(f1.imag[:, :nin], s_mm)], axis=0)
    out_spec = pl.BlockSpec((None, n1, s_blk, c), lambda p, j: (p, 0, j, 0))
    out_shape = jax.ShapeDtypeStruct((b // 2, n1, DFT_N2, c), BF16)
    return pl.pallas_call(
        functools.partial(_rows_pair_kernel, s_mm=s_mm),
        grid=(b // 2, DFT_N2 // s_blk),
        in_specs=[pl.BlockSpec((2, nin, s_blk, c), lambda p, j: (p, 0, j, 0)),
                  pl.BlockSpec(kcat.shape, lambda p, j: (0, 0))],
        out_specs=[out_spec, out_spec],
        out_shape=[out_shape, out_shape],
        compiler_params=_cparams("parallel", "parallel"),
        name="hyena_dft_rows",
    )(u.reshape(b, nin, DFT_N2, c), kcat)


def _complex_apply(mcat, xr, xi):
    p = _dot(mcat, xr)
    q = _dot(mcat, xi)
    return p[:DFT_N2] - q[DFT_N2:], q[:DFT_N2] + p[DFT_N2:]


def _spectrum_kernel(ar_ref, ai_ref, g_ref, xr_ref, xi_ref, *, scale):
    for j in range(ar_ref.shape[0]):
        xr, xi = _complex_apply(g_ref[j], ar_ref[j], ai_ref[j])
        xr_ref[j] = (xr * scale).astype(BF16)
        xi_ref[j] = (xi * scale).astype(BF16)


def _slab_conv_kernel(ar_ref, ai_ref, kr_ref, ki_ref, g_ref, h_ref, dr_ref, di_ref):
    for j in range(ar_ref.shape[0]):
        xr, xi = _complex_apply(g_ref[j], ar_ref[j], ai_ref[j])
        kr = kr_ref[j].astype(F32)
        ki = ki_ref[j].astype(F32)
        zr = (xr * kr - xi * ki).astype(BF16)
        zi = (xr * ki + xi * kr).astype(BF16)
        dr, di = _complex_apply(h_ref[j], zr, zi)
        dr_ref[j] = dr.astype(BF16)
        di_ref[j] = di.astype(BF16)


def _slab_consts(seq):
    n1, _, g = _dft_tables(seq)
    gcat = jnp.asarray(np.concatenate([g.real, g.imag], axis=1), F32).astype(BF16)
    gt = np.conj(g).transpose(0, 2, 1)
    hcat = jnp.asarray(np.concatenate([gt.real, gt.imag], axis=1), F32).astype(BF16)
    return n1, gcat, hcat


def _spectrum(ar, ai, seq):
    n1, gcat, _ = _slab_consts(seq)
    c = ar.shape[-1]
    g = min(SLABS_PER_STEP, n1)
    slab = pl.BlockSpec((g, DFT_N2, c), lambda i: (i, 0, 0))
    return pl.pallas_call(
        functools.partial(_spectrum_kernel, scale=1.0 / (2 * seq)),
        grid=(n1 // g,),
        in_specs=[slab, slab, pl.BlockSpec((g, 2 * DFT_N2, DFT_N2), lambda i: (i, 0, 0))],
        out_specs=[slab, slab],
        out_shape=[jax.ShapeDtypeStruct((n1, DFT_N2, c), BF16)] * 2,
        compiler_params=_cparams("parallel"),
        name="hyena_filter_spectrum",
    )(ar, ai, gcat)


def _slab_conv(ar, ai, kr, ki, order, seq):
    n1, gcat, hcat = _slab_consts(seq)
    npair, _, _, c = ar.shape
    g = min(SLABS_PER_STEP, n1)
    slab = pl.BlockSpec((None, g, DFT_N2, c), lambda p, i: (p, i, 0, 0))
    filt = pl.BlockSpec((g, DFT_N2, c), lambda p, i: (i, 0, order))
    mats = pl.BlockSpec((g, 2 * DFT_N2, DFT_N2), lambda p, i: (i, 0, 0))
    return pl.pallas_call(
        _slab_conv_kernel,
        grid=(npair, n1 // g),
        in_specs=[slab, slab, filt, filt, mats, mats],
        out_specs=[slab, slab],
        out_shape=[jax.ShapeDtypeStruct(ar.shape, BF16)] * 2,
        compiler_params=_cparams("parallel", "parallel"),
        name="hyena_dft_slabs",
    )(ar, ai, kr, ki, gcat, hcat)


def _rows_inverse_kernel(dr_ref, di_ref, kc_ref, u_ref, gate_ref, bias_ref, o_ref, *, s_mm):
    n1, s_blk, c = dr_ref.shape
    m = kc_ref.shape[0] // 2
    dr = dr_ref[...].astype(F32)
    di = di_ref[...].astype(F32)
    y0, y1 = [], []
    for h in range(s_blk // s_mm):
        sub = slice(h * s_mm, (h + 1) * s_mm)
        p = _dot(kc_ref[...], dr[:, sub, :].reshape(n1 * s_mm, c).astype(BF16))
        q = _dot(kc_ref[...], di[:, sub, :].reshape(n1 * s_mm, c).astype(BF16))
        y0.append((p[:m] + q[m:]).reshape(m // s_mm, s_mm, c))
        y1.append((q[:m] - p[m:]).reshape(m // s_mm, s_mm, c))
    for e, parts in enumerate((y0, y1)):
        y = jnp.concatenate(parts, axis=1)
        o_ref[e] = (gate_ref[e] * (y + u_ref[e] * bias_ref[...])).astype(o_ref.dtype)


def _rows_inverse(dr, di, u, gate, bias, seq, out_dtype):
    npair, n1, _, c = dr.shape
    _, f1, _ = _dft_tables(seq)
    nout = seq // DFT_N2
    s_mm, s_blk = _rows_block(nout)
    kcat = jnp.concatenate([_kron_rows(f1.real[:nout], s_mm), _kron_rows(f1.imag[:nout], s_mm)], axis=0)
    dspec = pl.BlockSpec((None, n1, s_blk, c), lambda p, j: (p, 0, j, 0))
    uspec = pl.BlockSpec((2, nout, s_blk, c), lambda p, j: (p, 0, j, 0))
    y = pl.pallas_call(
        functools.partial(_rows_inverse_kernel, s_mm=s_mm),
        grid=(npair, DFT_N2 // s_blk),
        in_specs=[dspec, dspec, pl.BlockSpec(kcat.shape, lambda p, j: (0, 0)),
                  uspec, uspec, pl.BlockSpec((1, 1, c), lambda p, j: (0, 0, 0))],
        out_specs=uspec,
        out_shape=jax.ShapeDtypeStruct((2 * npair, nout, DFT_N2, c), out_dtype),
        compiler_params=_cparams("parallel", "parallel"),
        name="hyena_idft_rows",
    )(dr, di, kcat, u.reshape(2 * npair, nout, DFT_N2, c), gate.reshape(2 * npair, nout, DFT_N2, c),
      bias.reshape(1, 1, c))
    return y.reshape(2 * npair, seq, c)


def _hyena(x1, x2, v, w1, b1, w2, b2, w3, freq, bias):
    b, seq, _ = v.shape
    assert b % 2 == 0, "batch elements are transformed in pairs"
    kt = _hyena_filters(seq, w1, b1, w2, b2, w3, freq)
    kr, ki = _spectrum(*_rows_real(kt, seq), seq)
    dr, di = _slab_conv(*_rows_pair(v, seq), kr, ki, 0, seq)
    z = _rows_inverse(dr, di, v, x1, bias[0], seq, F32)
    dr, di = _slab_conv(*_rows_pair(z, seq), kr, ki, 1, seq)
    return _rows_inverse(dr, di, z, x2, bias[1], seq, BF16)


def _mix_ffn_kernel(x_ref, hy_ref, sg_ref, at_ref, wo_ref, g1_ref, n2_ref, sc_ref, sh_ref, g2_ref, w1_ref,
                    w2_ref, o_ref, *, chunk):
    mix = _dot(hy_ref[...], wo_ref[:HY_WIDTH])
    mix = mix + _dot(sg_ref[...], wo_ref[HY_WIDTH:HY_WIDTH + SG_WIDTH])
    mix = mix + _dot(at_ref[...], wo_ref[HY_WIDTH + SG_WIDTH:])
    x = x_ref[...] + g1_ref[...] * mix
    hb = _modulated_norm(x, n2_ref[...], sc_ref[...], sh_ref[...]).astype(BF16)
    hidden = w2_ref.shape[0]
    acc = jnp.zeros(x.shape, F32)
    for c in range(hidden // chunk):
        a = _dot(hb, w1_ref[:, c * chunk:(c + 1) * chunk])
        up = _dot(hb, w1_ref[:, hidden + c * chunk:hidden + (c + 1) * chunk])
        act = (a * _sigmoid(a) * up).astype(BF16)
        acc = acc + _dot(act, w2_ref[c * chunk:(c + 1) * chunk, :])
    o_ref[...] = x + g2_ref[...] * acc


def _mix_ffn(x2d, seq, layer, hy, sg, att, w_out, g1, n2, sc, sh, g2, w1, w2):
    m, d = x2d.shape
    tm = min(FFN_TM, seq)
    per = seq // tm
    row = lambda i: (i, 0)
    fixed = lambda i: (0, 0)
    bat = lambda i: (i // per, 0, 0)
    resident = lambda a: pl.BlockSpec((None,) + a.shape[1:], lambda i: (layer, 0, 0), pipeline_mode=pl.Buffered(1))
    return pl.pallas_call(
        functools.partial(_mix_ffn_kernel, chunk=FFN_CHUNK),
        grid=(m // tm,),
        in_specs=[pl.BlockSpec((tm, d), row),
                  pl.BlockSpec((tm, HY_WIDTH), row),
                  pl.BlockSpec((tm, SG_WIDTH), row),
                  pl.BlockSpec((tm, DA_WIDTH), row),
                  resident(w_out),
                  pl.BlockSpec((None, 1, d), bat),
                  pl.BlockSpec((1, d), fixed),
                  pl.BlockSpec((None, 1, d), bat),
                  pl.BlockSpec((None, 1, d), bat),
                  pl.BlockSpec((None, 1, d), bat),
                  resident(w1),
                  resident(w2)],
        out_specs=pl.BlockSpec((tm, d), row),
        out_shape=jax.ShapeDtypeStruct((m, d), F32),
        compiler_params=_cparams("parallel"),
        name="outproj_ffn",
    )(x2d, hy, sg, att, w_out, g1, n2, sc, sh, g2, w1, w2)


def _rope_tables(seq):
    t = np.arange(seq)
    half = DA_HEAD_DIM // 4
    inv_freq = (np.float32(ROPE_BASE) ** (-np.arange(half, dtype=np.float32) / np.float32(half))).astype(np.float32)
    ang_row = (t // GRID_W).astype(np.float32)[:, None] * inv_freq[None, :]
    ang_col = (t % GRID_W).astype(np.float32)[:, None] * inv_freq[None, :]
    cos = np.concatenate([np.cos(ang_row)] * 2 + [np.cos(ang_col)] * 2, axis=-1)
    sin = np.concatenate([-np.sin(ang_row), np.sin(ang_row), -np.sin(ang_col), np.sin(ang_col)], axis=-1)
    return jnp.asarray(np.tile(cos, (1, 2)), F32), jnp.asarray(np.tile(sin, (1, 2)), F32)


def kernel(x, c, ctx, c_ctx, norm1_g, norm2_g, ada_w, ada_b, w_in, hy_conv_w, hy_conv_b, hy_w1, hy_b1,
           hy_w2, hy_b2, hy_w3, hy_freq, hy_bias, sg_norm_g, sg_w, sg_b, qn_g, kn_g, lam_p, subln_g,
           w_out, ffn_w1, ffn_w2):
    batch, seq, d = x.shape
    ctx_len = ctx.shape[1]
    depth = w_in.shape[0]
    assert batch + 1 <= 8 and seq % CHUNK == 0 and ctx_len % CHUNK == 0

    cvecs = jnp.zeros((8, d), F32).at[:batch].set(c).at[batch].set(c_ctx)
    mods = _ada(cvecs, ada_w, ada_b[:, None, :])

    cos_l, sin_l = _rope_tables(seq)
    cos_c = jnp.ones((ctx_len, LANES), F32)
    sin_c = jnp.zeros((ctx_len, LANES), F32)
    seg_np = np.kron(np.eye(LANES // DA_HEAD_DIM), np.full((DA_HEAD_DIM, DA_HEAD_DIM), 1.0 / DA_HEAD_DIM))
    seg = jnp.asarray(np.concatenate([seg_np, seg_np], axis=0), F32).astype(BF16)

    w_in_b = w_in.astype(BF16)
    w_out_b = w_out.astype(BF16)
    w1_b = ffn_w1.astype(BF16)
    w2_b = ffn_w2.astype(BF16)
    xs = x.reshape(batch * seq, d)
    cs = ctx.reshape(batch * ctx_len, d)
    for i in range(depth):
        last = i == depth - 1
        lam_init = 0.8 - 0.6 * math.exp(-0.3 * i)
        mod_l = [m[:, None, :] for m in jnp.split(mods[i, :batch], 6, axis=-1)]
        mod_c = [jnp.broadcast_to(m[None], (batch, 1, d)) for m in jnp.split(mods[i, batch:batch + 1], 6, axis=-1)]
        qg = jnp.tile(qn_g[i], 2)[None, :]
        kg = jnp.tile(kn_g[i], 2)[None, :]
        sgw = sg_w[i].reshape(SG_HEADS * CHUNK, CHUNK).astype(BF16)
        sgb = jnp.repeat(sg_b[i].T, SG_WIDTH // SG_HEADS, axis=1)
        hy_params = (hy_w1[i], hy_b1[i], hy_w2[i], hy_b2[i], hy_w3[i], hy_freq[i], hy_bias[i])

        def project(tokens, n_tok, mod, cos, sin):
            outs = _inproj(tokens, n_tok, i, norm1_g[i][None, :], mod[1], mod[0], w_in_b, hy_conv_w[i],
                           hy_conv_b[i][None, :], cos, sin, qg, kg, seg, sg_norm_g[i][None, :], sgw, sgb)
            hy = [a.reshape(batch, n_tok, HY_WIDTH) for a in outs[:3]]
            return [hy, outs[3]] + [a.reshape(batch, n_tok, DA_WIDTH) for a in outs[4:]]

        def finish(tokens, n_tok, mod, hy_parts, sg, att):
            hy = _hyena(*hy_parts, *hy_params).reshape(batch * n_tok, HY_WIDTH)
            return _mix_ffn(tokens, n_tok, i, hy, sg, att.reshape(batch * n_tok, DA_WIDTH), w_out_b, mod[2],
                            norm2_g[i][None, :], mod[4], mod[3], mod[5], w1_b, w2_b)

        hy_l, sg_l, q_l, k_l, v_l = project(xs, seq, mod_l, cos_l, sin_l)
        hy_c, sg_c, q_c, k_c, v_c = project(cs, ctx_len, mod_c, cos_c, sin_c)
        attend = functools.partial(_attention, lam_p=lam_p[i], subln_g=subln_g[i][None, :], lam_init=lam_init,
                                   qn_g=qn_g[i], kn_g=kn_g[i])
        att_l = attend(q_l, [(k_l, v_l), (k_c, v_c)])
        xs = finish(xs, seq, mod_l, hy_l, sg_l, att_l)
        if not last:
            cs = finish(cs, ctx_len, mod_c, hy_c, sg_c, attend(q_c, [(k_c, v_c)]))
    return xs.reshape(batch, seq, d)
```
